```python
import jax, jax.numpy as jnp
from jax import lax
import numpy as np

D_MODEL = 1024
BATCH = 2
SEQ = 8192
DEPTH = 2

CHUNK = 64
GLA_HEADS = 4
GLA_DK = D_MODEL // 2
GLA_DV = D_MODEL
GLA_HK = GLA_DK // GLA_HEADS
GLA_HV = GLA_DV // GLA_HEADS
GLA_GATE_RANK = 16
GLA_GATE_TEMP = 16.0
GDN_HEADS = 4
GDN_HD = 128
GDN_WIDTH = GDN_HEADS * GDN_HD
GDN_CONV = 4
RWKV_HD = 64
RWKV_WIDTH = D_MODEL // 2
RWKV_HEADS = RWKV_WIDTH // RWKV_HD
RWKV_DECAY_RANK = 64
RWKV_A_RANK = 64
RWKV_GATE_RANK = 128
RWKV_GN_EPS = 64e-5
RWKV_IN = 3 * RWKV_WIDTH + RWKV_DECAY_RANK + RWKV_A_RANK + RWKV_GATE_RANK
RWKV_WIDTHS = (RWKV_WIDTH, RWKV_WIDTH, RWKV_WIDTH, RWKV_DECAY_RANK, RWKV_A_RANK, RWKV_GATE_RANK)
RWKV_SPLITS = tuple(int(s) for s in np.cumsum(RWKV_WIDTHS)[:-1])
N_BRANCH = 3
D_FF = 4 * D_MODEL
DN_ALPHA = (2 * DEPTH) ** 0.25
DN_BETA = (8 * DEPTH) ** -0.25
LN_EPS = 1e-5
RMS_EPS = 1e-6
L2_EPS = 1e-6
IN_WIDTHS = (GLA_DK, GLA_DK, GLA_DV, GLA_GATE_RANK, GLA_DV,
             3 * GDN_WIDTH, GDN_HEADS, GDN_HEADS, GDN_WIDTH,
             RWKV_IN,
             N_BRANCH * D_MODEL)
N_IN = sum(IN_WIDTHS)
IN_SPLITS = tuple(int(s) for s in np.cumsum(IN_WIDTHS)[:-1])

kernel_name = "hybrid_gla_gdn_rwkv7_deepnorm"


def layer_norm(x, g, b):
    xf = x.astype(jnp.float32)
    mu = jnp.mean(xf, -1, keepdims=True)
    var = jnp.mean(jnp.square(xf - mu), -1, keepdims=True)
    return ((xf - mu) * lax.rsqrt(var + LN_EPS) * g + b).astype(x.dtype)


def rms_norm(x, g):
    return x * lax.rsqrt(jnp.mean(x * x, -1, keepdims=True) + RMS_EPS) * g


def l2norm(x):
    return x * lax.rsqrt(jnp.sum(x * x, -1, keepdims=True) + L2_EPS)


def to_chunks(x, heads):
    b, s, _ = x.shape
    return x.reshape(b, s // CHUNK, CHUNK, heads, -1).transpose(1, 0, 3, 2, 4)


def from_chunks(o):
    n, b, h, c, d = o.shape
    return o.transpose(1, 0, 3, 2, 4).reshape(b, n * c, h * d)


def gla_mixer(q, k, v, gate_down, out_gate, gate_up, gate_bias, norm_g):
    f32 = jnp.float32
    bsz = q.shape[0]
    q = q.astype(f32) * GLA_HK ** -0.5
    log_a = jax.nn.log_sigmoid(gate_down.astype(f32) @ gate_up.astype(f32) + gate_bias) / GLA_GATE_TEMP
    qc = to_chunks(q, GLA_HEADS)
    kc = to_chunks(k.astype(f32), GLA_HEADS)
    vc = to_chunks(v.astype(f32), GLA_HEADS)
    bc = jnp.cumsum(to_chunks(log_a, GLA_HEADS), axis=3)
    causal = jnp.tril(jnp.ones((CHUNK, CHUNK), bool))

    def step(state, inp):
        q_, k_, v_, b_ = inp
        diff = b_[:, :, :, None, :] - b_[:, :, None, :, :]
        dec = jnp.exp(jnp.where(causal[:, :, None], diff, -jnp.inf))
        attn = jnp.einsum('bhid,bhijd,bhjd->bhij', q_, dec, k_)
        o = (jnp.einsum('bhid,bhde->bhie', q_ * jnp.exp(b_), state)
             + jnp.einsum('bhij,bhje->bhie', attn, v_))
        b_last = b_[:, :, -1, :]
        state = (state * jnp.exp(b_last)[..., None]
                 + jnp.einsum('bhjd,bhje->bhde', k_ * jnp.exp(b_last[:, :, None, :] - b_), v_))
        return state, o

    s0 = jnp.zeros((bsz, GLA_HEADS, GLA_HK, GLA_HV), f32)
    _, o = lax.scan(step, s0, (qc, kc, vc, bc))
    o = from_chunks(rms_norm(o, norm_g))
    return o * jax.nn.silu(out_gate.astype(f32))


def causal_depthwise_conv(x, w):
    c = x.shape[-1]
    return lax.conv_general_dilated(x, w[:, None, :], window_strides=(1,),
                                    padding=[(GDN_CONV - 1, 0)],
                                    dimension_numbers=('NWC', 'WIO', 'NWC'),
                                    feature_group_count=c)


def gdn_mixer(qkv, a, b, z, conv_w, a_log, dt_bias, norm_g):
    f32 = jnp.float32
    bsz, s, _ = qkv.shape
    n = s // CHUNK
    qkv = jax.nn.silu(causal_depthwise_conv(qkv.astype(f32), conv_w.astype(f32)))
    q, k, v = jnp.split(qkv, 3, axis=-1)
    qc = l2norm(to_chunks(q, GDN_HEADS)) * GDN_HD ** -0.5
    kc = l2norm(to_chunks(k, GDN_HEADS))
    vc = to_chunks(v, GDN_HEADS)
    g = -jnp.exp(a_log) * jax.nn.softplus(a.astype(f32) + dt_bias)
    beta = jax.nn.sigmoid(b.astype(f32))
    gc = g.reshape(bsz, n, CHUNK, GDN_HEADS).transpose(1, 0, 3, 2)
    betac = beta.reshape(bsz, n, CHUNK, GDN_HEADS).transpose(1, 0, 3, 2)
    gam = jnp.cumsum(gc, axis=-1)
    incl = jnp.tril(jnp.ones((CHUNK, CHUNK), bool))
    strict = jnp.tril(jnp.ones((CHUNK, CHUNK), bool), k=-1)
    decay = jnp.exp(jnp.where(incl, gam[..., :, None] - gam[..., None, :], -jnp.inf))
    kb = kc * betac[..., None]
    m = jnp.where(strict, jnp.einsum('nbhid,nbhjd->nbhij', kb, kc) * decay, 0.0)
    lhs = m + jnp.eye(CHUNK, dtype=f32)
    rhs = jnp.concatenate([vc * betac[..., None], kb * jnp.exp(gam)[..., None]], axis=-1)
    sol = lax.linalg.triangular_solve(lhs, rhs, left_side=True, lower=True, unit_diagonal=True)
    u, w = sol[..., :GDN_HD], sol[..., GDN_HD:]
    aqk = jnp.einsum('nbhid,nbhjd->nbhij', qc, kc) * decay

    def step(state, inp):
        q_, k_, u_, w_, aqk_, gam_ = inp
        v_new = u_ - w_ @ state
        o = (q_ * jnp.exp(gam_)[..., None]) @ state + aqk_ @ v_new
        g_last = gam_[..., -1]
        state = (state * jnp.exp(g_last)[..., None, None]
                 + jnp.einsum('bhjd,bhje->bhde', k_ * jnp.exp(g_last[..., None] - gam_)[..., None], v_new))
        return state, o

    s0 = jnp.zeros((bsz, GDN_HEADS, GDN_HD, GDN_HD), f32)
    _, o = lax.scan(step, s0, (qc, kc, u, w, aqk, gam))
    o = from_chunks(rms_norm(o, norm_g))
    return o * jax.nn.silu(z.astype(f32))


def rwkv7_mixer(p, mu, w0, w2, a0, a2, g2, k_k, k_a, r_k, ln_g, ln_b):
    f32 = jnp.float32
    bsz, s, _ = p.shape
    p = p.astype(f32)
    p_prev = jnp.pad(p, ((0, 0), (1, 0), (0, 0)))[:, :-1]
    p = p + (p_prev - p) * mu
    r, k, v, wd, ad, gd = jnp.split(p, RWKV_SPLITS, axis=-1)
    log_w = -jnp.exp(-jax.nn.softplus(-(w0 + jnp.tanh(wd) @ w2)) - 0.5)
    a = jax.nn.sigmoid(a0 + ad @ a2)
    g = jax.nn.sigmoid(gd) @ g2
    heads = lambda t: t.reshape(bsz, s, RWKV_HEADS, RWKV_HD)
    kk = l2norm(heads(k * k_k))
    k = k * (1.0 + (a - 1.0) * k_a)
    r, k, v, a, log_w = heads(r), heads(k), heads(v), heads(a), heads(log_w)

    def step(state, inp):
        r_, w_, k_, v_, kk_, a_ = inp
        sa = jnp.einsum('bhij,bhj->bhi', state, -kk_)
        state = (state * jnp.exp(w_)[:, :, None, :]
                 + sa[..., None] * (kk_ * a_)[:, :, None, :]
                 + v_[..., None] * k_[:, :, None, :])
        return state, jnp.einsum('bhij,bhj->bhi', state, r_)

    xs = tuple(t.transpose(1, 0, 2, 3) for t in (r, log_w, k, v, kk, a))
    s0 = jnp.zeros((bsz, RWKV_HEADS, RWKV_HD, RWKV_HD), f32)
    _, o = lax.scan(step, s0, xs)
    o = o.transpose(1, 0, 2, 3)
    m = jnp.mean(o, -1, keepdims=True)
    var = jnp.mean(jnp.square(o - m), -1, keepdims=True)
    o = ((o - m) * lax.rsqrt(var + RWKV_GN_EPS)).reshape(bsz, s, RWKV_WIDTH) * ln_g + ln_b
    bonus = jnp.sum(r * k * r_k.reshape(RWKV_HEADS, RWKV_HD), -1, keepdims=True) * v
    o = o + bonus.reshape(bsz, s, RWKV_WIDTH)
    return o * g


def setup_inputs(seed: int = 0) -> dict:
    key = jax.random.key(seed)
    ks = iter(jax.random.split(key, 40))
    L = DEPTH

    def nrm(shape, scale):
        return jax.random.normal(next(ks), shape, jnp.float32) * scale

    def unif(shape, lo, hi):
        return jax.random.uniform(next(ks), shape, jnp.float32, lo, hi)

    dt = jnp.exp(unif((L, GDN_HEADS), float(np.log(1e-3)), float(np.log(1e-1))))
    return {
        "x": nrm((BATCH, SEQ, D_MODEL), 1.0),
        "w_in": nrm((L, D_MODEL, N_IN), D_MODEL ** -0.5),
        "gla_gate_up": nrm((L, GLA_GATE_RANK, GLA_DK), GLA_GATE_RANK ** -0.5),
        "gla_gate_bias": nrm((L, GLA_DK), 0.5) + 2.0,
        "gla_norm_g": 1.0 + nrm((L, GLA_HV), 0.02),
        "gdn_conv": nrm((L, GDN_CONV, 3 * GDN_WIDTH), GDN_CONV ** -0.5),
        "gdn_a_log": jnp.log(unif((L, GDN_HEADS), 1.0, 16.0)),
        "gdn_dt_bias": dt + jnp.log(-jnp.expm1(-dt)),
        "gdn_norm_g": 1.0 + nrm((L, GDN_HD), 0.02),
        "rwkv_mu": unif((L, RWKV_IN), 0.0, 1.0),
        "rwkv_w0": unif((L, RWKV_WIDTH), -6.0, 0.0),
        "rwkv_w2": nrm((L, RWKV_DECAY_RANK, RWKV_WIDTH), 0.5 * RWKV_DECAY_RANK ** -0.5),
        "rwkv_a0": nrm((L, RWKV_WIDTH), 0.1),
        "rwkv_a2": nrm((L, RWKV_A_RANK, RWKV_WIDTH), RWKV_A_RANK ** -0.5),
        "rwkv_g2": nrm((L, RWKV_GATE_RANK, RWKV_WIDTH), RWKV_GATE_RANK ** -0.5),
        "rwkv_k_k": 0.85 + nrm((L, RWKV_WIDTH), 0.02),
        "rwkv_k_a": 1.0 + nrm((L, RWKV_WIDTH), 0.02),
        "rwkv_r_k": nrm((L, RWKV_WIDTH), 0.1),
        "rwkv_ln_g": 1.0 + nrm((L, RWKV_WIDTH), 0.02),
        "rwkv_ln_b": nrm((L, RWKV_WIDTH), 0.02),
        "w_br_gla": nrm((L, GLA_DV, D_MODEL), GLA_DV ** -0.5),
        "w_br_gdn": nrm((L, GDN_WIDTH, D_MODEL), GDN_WIDTH ** -0.5),
        "w_br_rwkv": nrm((L, RWKV_WIDTH, D_MODEL), RWKV_WIDTH ** -0.5),
        "w_out": nrm((L, D_MODEL, D_MODEL), DN_BETA * D_MODEL ** -0.5),
        "ln1_g": 1.0 + nrm((L, D_MODEL), 0.02),
        "ln1_b": nrm((L, D_MODEL), 0.02),
        "w_up": nrm((L, D_MODEL, D_FF), D_MODEL ** -0.5),
        "w_down": nrm((L, D_FF, D_MODEL), DN_BETA * D_FF ** -0.5),
        "ln2_g": 1.0 + nrm((L, D_MODEL), 0.02),
        "ln2_b": nrm((L, D_MODEL), 0.02),
    }


def reference(x, w_in, gla_gate_up, gla_gate_bias, gla_norm_g, gdn_conv, gdn_a_log, gdn_dt_bias,
              gdn_norm_g, rwkv_mu, rwkv_w0, rwkv_w2, rwkv_a0, rwkv_a2, rwkv_g2, rwkv_k_k, rwkv_k_a,
              rwkv_r_k, rwkv_ln_g, rwkv_ln_b, w_br_gla, w_br_gdn, w_br_rwkv, w_out, ln1_g, ln1_b,
              w_up, w_down, ln2_g, ln2_b):
    bsz, s, _ = x.shape
    for l in range(DEPTH):
        proj = x @ w_in[l]
        (gq, gk, gv, gga, ggr, dqkv, da, db, dz, rp, gates) = jnp.split(proj, IN_SPLITS, axis=-1)
        o_gla = gla_mixer(gq, gk, gv, gga, ggr, gla_gate_up[l], gla_gate_bias[l], gla_norm_g[l])
        o_gdn = gdn_mixer(dqkv, da, db, dz, gdn_conv[l], gdn_a_log[l], gdn_dt_bias[l], gdn_norm_g[l])
        o_rwkv = rwkv7_mixer(rp, rwkv_mu[l], rwkv_w0[l], rwkv_w2[l], rwkv_a0[l], rwkv_a2[l], rwkv_g2[l],
                             rwkv_k_k[l], rwkv_k_a[l], rwkv_r_k[l], rwkv_ln_g[l], rwkv_ln_b[l])
        gate = jax.nn.sigmoid(gates.astype(jnp.float32)).reshape(bsz, s, N_BRANCH, D_MODEL)
        merged = (gate[:, :, 0] * (o_gla @ w_br_gla[l])
                  + gate[:, :, 1] * (o_gdn @ w_br_gdn[l])
                  + gate[:, :, 2] * (o_rwkv @ w_br_rwkv[l]))
        mix = merged.astype(x.dtype) @ w_out[l]
        x = layer_norm(DN_ALPHA * x + mix, ln1_g[l], ln1_b[l])
        h = jnp.square(jax.nn.relu(x @ w_up[l])) @ w_down[l]
        x = layer_norm(DN_ALPHA * x + h, ln2_g[l], ln2_b[l])
    return x
```

```python
import functools

import jax
import jax.numpy as jnp
from jax import lax
from jax.experimental import pallas as pl
from jax.experimental.pallas import tpu as pltpu

F32 = jnp.float32
BF16 = jnp.bfloat16
HIGHEST = lax.Precision.HIGHEST

D_MODEL = 1024
DEPTH = 2
CHUNK = 64
GLA_HEADS = 4
GLA_DK = 512
GLA_DV = 1024
GLA_HK = 128
GLA_HV = 256
GLA_GATE_RANK = 16
GLA_GATE_TEMP = 16.0
GDN_HEADS = 4
GDN_HD = 128
GDN_WIDTH = 512
GDN_CONV = 4
RWKV_HD = 64
RWKV_WIDTH = 512
RWKV_HEADS = 8
RWKV_DECAY_RANK = 64
RWKV_A_RANK = 64
RWKV_GATE_RANK = 128
RWKV_GN_EPS = 64e-5
N_BRANCH = 3
D_FF = 4 * D_MODEL
DN_ALPHA = (2 * DEPTH) ** 0.25
LN_EPS = 1e-5
RMS_EPS = 1e-6
L2_EPS = 1e-6

LANE = 128
SUBLANE = 8
VMEM_LIMIT = 56 * 1024 * 1024

_IN_WIDTHS = (GLA_DK, GLA_DK, GLA_DV, GLA_GATE_RANK, GLA_DV,
              3 * GDN_WIDTH, GDN_HEADS, GDN_HEADS, GDN_WIDTH,
              3 * RWKV_WIDTH + RWKV_DECAY_RANK + RWKV_A_RANK + RWKV_GATE_RANK,
              N_BRANCH * D_MODEL)
_IN_OFFS = [0]
for _w in _IN_WIDTHS:
    _IN_OFFS.append(_IN_OFFS[-1] + _w)


def _dot(a, b):
    return jnp.dot(a.astype(BF16), b.astype(BF16), preferred_element_type=F32)


def _dot_nt(a, b):
    return lax.dot_general(a.astype(BF16), b.astype(BF16), (((1,), (1,)), ((), ())),
                           preferred_element_type=F32)


def _dot_tn(a, b):
    return lax.dot_general(a.astype(BF16), b.astype(BF16), (((0,), (0,)), ((), ())),
                           preferred_element_type=F32)


def _dot_hi(a, b):
    return jnp.dot(a, b, precision=HIGHEST, preferred_element_type=F32)


def _sigmoid(x):
    return 1.0 / (1.0 + jnp.exp(-x))


def _silu(x):
    return x * _sigmoid(x)


def _softplus(x):
    return jnp.maximum(x, 0.0) + jnp.log1p(jnp.exp(-jnp.abs(x)))


def _log_sigmoid(x):
    return -_softplus(-x)


def _chunk_masks():
    row = lax.broadcasted_iota(jnp.int32, (CHUNK, CHUNK), 0)
    col = lax.broadcasted_iota(jnp.int32, (CHUNK, CHUNK), 1)
    return row >= col, row > col, row == col


def _unit_lower_inverse(a_strict, eye):
    n = -a_strict
    t = eye + n
    p = n
    steps = CHUNK.bit_length() - 2
    for _ in range(steps):
        p = _dot_hi(p, p)
        t = t + _dot_hi(t, p)
    return t


def _layer_norm(y, g, b):
    mu = jnp.mean(y, -1, keepdims=True)
    d = y - mu
    var = jnp.mean(d * d, -1, keepdims=True)
    return d * lax.rsqrt(var + LN_EPS) * g + b


def _proj_kernel(x_ref, w_ref, o_ref):
    o_ref[...] = jnp.dot(x_ref[...].astype(BF16), w_ref[...],
                         preferred_element_type=F32).astype(o_ref.dtype)


def _project(x, w, tm=512):
    m, k = x.shape
    n = w.shape[1]
    return pl.pallas_call(
        _proj_kernel,
        grid=(m // tm,),
        in_specs=[pl.BlockSpec((tm, k), lambda i: (i, 0)),
                  pl.BlockSpec((k, n), lambda i: (0, 0))],
        out_specs=pl.BlockSpec((tm, n), lambda i: (i, 0)),
        out_shape=jax.ShapeDtypeStruct((m, n), F32),
        compiler_params=pltpu.CompilerParams(dimension_semantics=("parallel",),
                                             vmem_limit_bytes=VMEM_LIMIT),
        name="in_proj",
    )(x, w)


def _gla_kernel(q_ref, k_ref, v_ref, og_ref, gd_ref, gup_ref, gb_ref, ng_ref, o_ref, st_ref, *, tb):
    @pl.when(pl.program_id(2) == 0)
    def _():
        st_ref[...] = jnp.zeros_like(st_ref)

    incl, _, _ = _chunk_masks()
    tril = incl.astype(F32)
    log_a = _log_sigmoid(_dot(gd_ref[...], gup_ref[...]) + gb_ref[...]) * (1.0 / GLA_GATE_TEMP)
    mid = CHUNK // 2
    for c in range(tb // CHUNK):
        sl = slice(c * CHUNK, (c + 1) * CHUNK)
        b = _dot_hi(tril, log_a[sl])
        b_mid = b[mid:mid + 1]
        b_last = b[CHUNK - 1:CHUNK]
        qs = q_ref[sl, :] * (GLA_HK ** -0.5) * jnp.exp(b - b_mid)
        ks = k_ref[sl, :] * jnp.exp(b_mid - b)
        v = v_ref[sl, :]
        attn = jnp.where(incl, _dot_nt(qs, ks), 0.0)
        st = st_ref[...]
        o = _dot_nt(qs * jnp.exp(b_mid), st) + _dot(attn, v)
        st_ref[...] = st * jnp.exp(b_last) + _dot_tn(v, ks * jnp.exp(b_last - b_mid))
        o = o * lax.rsqrt(jnp.mean(o * o, -1, keepdims=True) + RMS_EPS) * ng_ref[...]
        o_ref[sl, :] = o * _silu(og_ref[sl, :])


def _gla(proj, gup, gb, ng, bsz, seq, tb=256):
    nt = seq // tb
    qoff = 0
    koff = GLA_DK // GLA_HK
    voff = (2 * GLA_DK) // GLA_HV
    ogoff = (2 * GLA_DK + GLA_DV) // GLA_HV
    gdoff = (2 * GLA_DK + 2 * GLA_DV) // LANE
    row = lambda b, h, t: b * nt + t
    return pl.pallas_call(
        functools.partial(_gla_kernel, tb=tb),
        grid=(bsz, GLA_HEADS, nt),
        in_specs=[
            pl.BlockSpec((tb, GLA_HK), lambda b, h, t: (row(b, h, t), qoff + h)),
            pl.BlockSpec((tb, GLA_HK), lambda b, h, t: (row(b, h, t), koff + h)),
            pl.BlockSpec((tb, GLA_HV), lambda b, h, t: (row(b, h, t), voff + h)),
            pl.BlockSpec((tb, GLA_HV), lambda b, h, t: (row(b, h, t), ogoff + h)),
            pl.BlockSpec((tb, LANE), lambda b, h, t: (row(b, h, t), gdoff)),
            pl.BlockSpec((LANE, GLA_HK), lambda b, h, t: (0, h)),
            pl.BlockSpec((1, GLA_HK), lambda b, h, t: (0, h)),
            pl.BlockSpec((1, GLA_HV), lambda b, h, t: (0, 0)),
        ],
        out_specs=pl.BlockSpec((tb, GLA_HV), lambda b, h, t: (row(b, h, t), h)),
        out_shape=jax.ShapeDtypeStruct((bsz * seq, GLA_DV), F32),
        scratch_shapes=[pltpu.VMEM((GLA_HV, GLA_HK), F32)],
        compiler_params=pltpu.CompilerParams(
            dimension_semantics=("parallel", "parallel", "arbitrary"),
            vmem_limit_bytes=VMEM_LIMIT),
        name="gla",
    )(proj, proj, proj, proj, proj, gup, gb, ng)


def _gdn_kernel(alog_ref, dtb_ref, q_ref, k_ref, v_ref, z_ref, ab_ref, cw_ref, ng_ref,
                o_ref, st_ref, buf_ref, *, tb):
    h = pl.program_id(1)

    @pl.when(pl.program_id(2) == 0)
    def _():
        st_ref[...] = jnp.zeros_like(st_ref)
        buf_ref[:, 0:SUBLANE, :] = jnp.zeros((3, SUBLANE, GDN_HD), F32)

    conv = []
    for idx, x_ref in enumerate((q_ref, k_ref, v_ref)):
        x = x_ref[...]
        buf_ref[idx, SUBLANE:SUBLANE + tb, :] = x
        w = cw_ref[idx]
        y = x * w[GDN_CONV - 1:GDN_CONV]
        for j in range(1, GDN_CONV):
            y = y + buf_ref[idx, SUBLANE - j:SUBLANE - j + tb, :] * w[GDN_CONV - 1 - j:GDN_CONV - j]
        buf_ref[idx, 0:SUBLANE, :] = x[tb - SUBLANE:tb]
        conv.append(_silu(y))
    q_all, k_all, v_all = conv

    incl, strict, diag = _chunk_masks()
    tril = incl.astype(F32)
    eye = diag.astype(F32)
    ones = jnp.ones((CHUNK, CHUNK), F32)

    ab = ab_ref[...]
    lane = lax.broadcasted_iota(jnp.int32, ab.shape, 1)
    a_col = jnp.sum(jnp.where(lane == h, ab, 0.0), -1, keepdims=True)
    b_col = jnp.sum(jnp.where(lane == h + GDN_HEADS, ab, 0.0), -1, keepdims=True)
    neg_rate = -jnp.exp(jnp.full((1, 1), alog_ref[h], F32))
    g_all = neg_rate * _softplus(a_col + dtb_ref[h])
    beta_all = _sigmoid(b_col)

    for c in range(tb // CHUNK):
        sl = slice(c * CHUNK, (c + 1) * CHUNK)
        q = q_all[sl]
        k = k_all[sl]
        v = v_all[sl]
        beta = beta_all[sl]
        q = q * lax.rsqrt(jnp.sum(q * q, -1, keepdims=True) + L2_EPS) * (GDN_HD ** -0.5)
        k = k * lax.rsqrt(jnp.sum(k * k, -1, keepdims=True) + L2_EPS)
        gam_i = _dot_hi(tril, jnp.broadcast_to(g_all[sl], (CHUNK, CHUNK)))
        gam_j = _dot_hi(ones, gam_i * eye)
        decay = jnp.where(incl, jnp.exp(jnp.minimum(gam_i - gam_j, 0.0)), 0.0)
        gam = gam_i[:, 0:1]
        g_last = gam_i[CHUNK - 1:CHUNK, 0:1]
        e_gam = jnp.exp(gam)
        kb = k * beta
        a_mat = jnp.where(strict, _dot_nt(kb, k) * decay, 0.0)
        t_inv = _unit_lower_inverse(a_mat, eye)
        u = _dot_hi(t_inv, v * beta)
        w = _dot_hi(t_inv, kb * e_gam)
        aqk = _dot_nt(q, k) * decay
        st = st_ref[...]
        v_new = u - _dot(w, st)
        o = _dot(q * e_gam, st) + _dot(aqk, v_new)
        st_ref[...] = st * jnp.exp(g_last) + _dot_tn(k * jnp.exp(g_last - gam), v_new)
        o = o * lax.rsqrt(jnp.mean(o * o, -1, keepdims=True) + RMS_EPS) * ng_ref[...]
        o_ref[sl, :] = o * _silu(z_ref[sl, :])


def _gdn(proj, alog, dtb, cw, ng, bsz, seq, tb=256):
    nt = seq // tb
    row = lambda b, h, t: b * nt + t
    nh = GDN_HEADS
    smem = pl.BlockSpec(memory_space=pltpu.SMEM)
    return pl.pallas_call(
        functools.partial(_gdn_kernel, tb=tb),
        grid=(bsz, nh, nt),
        in_specs=[
            smem, smem,
            pl.BlockSpec((tb, GDN_HD), lambda b, h, t: (row(b, h, t), h)),
            pl.BlockSpec((tb, GDN_HD), lambda b, h, t: (row(b, h, t), nh + h)),
            pl.BlockSpec((tb, GDN_HD), lambda b, h, t: (row(b, h, t), 2 * nh + h)),
            pl.BlockSpec((tb, GDN_HD), lambda b, h, t: (row(b, h, t), 3 * nh + h)),
            pl.BlockSpec((tb, LANE), lambda b, h, t: (row(b, h, t), 4 * nh)),
            pl.BlockSpec((3, GDN_CONV, GDN_HD), lambda b, h, t: (0, 0, h)),
            pl.BlockSpec((1, GDN_HD), lambda b, h, t: (0, 0)),
        ],
        out_specs=pl.BlockSpec((tb, GDN_HD), lambda b, h, t: (row(b, h, t), h)),
        out_shape=jax.ShapeDtypeStruct((bsz * seq, GDN_WIDTH), F32),
        scratch_shapes=[pltpu.VMEM((GDN_HD, GDN_HD), F32),
                        pltpu.VMEM((3, tb + SUBLANE, GDN_HD), F32)],
        compiler_params=pltpu.CompilerParams(
            dimension_semantics=("parallel", "parallel", "arbitrary"),
            vmem_limit_bytes=VMEM_LIMIT),
        name="gdn",
    )(alog, dtb, proj, proj, proj, proj, proj, cw, ng)


def _rwkv_prep_kernel(p_ref, mu_ref, w0_ref, w2_ref, a0_ref, a2_ref, g2_ref, kk_ref, ka_ref, hsum_ref,
                      r_out, w_out, k_out, v_out, kk_out, a_out, g_out, buf_ref, *, tb):
    @pl.when(pl.program_id(1) == 0)
    def _():
        buf_ref[0:SUBLANE, :] = jnp.zeros((SUBLANE, buf_ref.shape[1]), F32)

    p = p_ref[...]
    buf_ref[SUBLANE:SUBLANE + tb, :] = p
    prev = buf_ref[SUBLANE - 1:SUBLANE - 1 + tb, :]
    buf_ref[0:SUBLANE, :] = p[tb - SUBLANE:tb]
    p = p + (prev - p) * mu_ref[...]
    wd = RWKV_WIDTH
    r = p[:, 0:wd]
    k = p[:, wd:2 * wd]
    v = p[:, 2 * wd:3 * wd]
    d_in = p[:, 3 * wd:3 * wd + LANE]
    a_in = p[:, 3 * wd + LANE:3 * wd + 2 * LANE]
    g_in = p[:, 3 * wd + 2 * LANE:3 * wd + 3 * LANE]
    log_w = -jnp.exp(-_softplus(-(w0_ref[...] + _dot(jnp.tanh(d_in), w2_ref[...]))) - 0.5)
    a = _sigmoid(a0_ref[...] + _dot(a_in, a2_ref[...]))
    g = _dot(_sigmoid(g_in), g2_ref[...])
    kk = k * kk_ref[...]
    ss = _dot_hi(kk * kk, hsum_ref[...])
    kk = kk * lax.rsqrt(ss + L2_EPS)
    k = k * (1.0 + (a - 1.0) * ka_ref[...])
    r_out[...] = r
    w_out[...] = log_w
    k_out[...] = k
    v_out[...] = v
    kk_out[...] = kk
    a_out[...] = a
    g_out[...] = g


def _rwkv_prep(proj, mu, w0, w2, a0, a2, g2, k_k, k_a, hsum, bsz, seq, tb=256):
    nt = seq // tb
    width = proj.shape[1]
    wd = RWKV_WIDTH
    full = lambda shape: pl.BlockSpec(shape, lambda b, t: (0, 0))
    out_spec = pl.BlockSpec((tb, wd), lambda b, t: (b * nt + t, 0))
    out_sds = jax.ShapeDtypeStruct((bsz * seq, wd), F32)
    return pl.pallas_call(
        functools.partial(_rwkv_prep_kernel, tb=tb),
        grid=(bsz, nt),
        in_specs=[pl.BlockSpec((tb, width), lambda b, t: (b * nt + t, 0)),
                  full((1, width)), full((1, wd)), full((LANE, wd)), full((1, wd)), full((LANE, wd)),
                  full((LANE, wd)), full((1, wd)), full((1, wd)), full((wd, wd))],
        out_specs=[out_spec] * 7,
        out_shape=[out_sds] * 7,
        scratch_shapes=[pltpu.VMEM((tb + SUBLANE, width), F32)],
        compiler_params=pltpu.CompilerParams(dimension_semantics=("parallel", "arbitrary"),
                                             vmem_limit_bytes=VMEM_LIMIT),
        name="rwkv_prep",
    )(proj, mu, w0, w2, a0, a2, g2, k_k, k_a, hsum)


def _rwkv_kernel(r_ref, w_ref, k_ref, v_ref, kk_ref, a_ref, g_ref, rk_ref, lng_ref, lnb_ref,
                 o_ref, st_ref, *, tb):
    @pl.when(pl.program_id(2) == 0)
    def _():
        st_ref[...] = jnp.zeros_like(st_ref)

    incl, strict, diag = _chunk_masks()
    tril = incl.astype(F32)
    eye = diag.astype(F32)
    mid = CHUNK // 2
    hd = RWKV_HD
    for c in range(tb // CHUNK):
        sl = slice(c * CHUNK, (c + 1) * CHUNK)
        outs = []
        for j in range(LANE // hd):
            hs = slice(j * hd, (j + 1) * hd)
            r = r_ref[sl, hs]
            lw = w_ref[sl, hs]
            k = k_ref[sl, hs]
            v = v_ref[sl, hs]
            kk = kk_ref[sl, hs]
            a = a_ref[sl, hs]
            cs = _dot_hi(tril, lw)
            c_mid = cs[mid:mid + 1]
            c_last = cs[CHUNK - 1:CHUNK]
            e_out = jnp.exp(c_mid - cs)
            r_s = r * jnp.exp(cs - c_mid)
            kk_s = kk * jnp.exp(cs - lw - c_mid)
            al_s = kk * a * e_out
            k_s = k * e_out
            e_mid = jnp.exp(c_mid)
            a_mat = jnp.where(strict, _dot_nt(kk_s, al_s), 0.0)
            b_mat = jnp.where(strict, _dot_nt(kk_s, k_s), 0.0)
            t_inv = _unit_lower_inverse(a_mat, eye)
            st = st_ref[j]
            rhs = -_dot_nt(kk_s * e_mid, st) - _dot(b_mat, v)
            u = _dot_hi(t_inv, rhs)
            o = (_dot_nt(r_s * e_mid, st)
                 + _dot(jnp.where(incl, _dot_nt(r_s, al_s), 0.0), u)
                 + _dot(jnp.where(incl, _dot_nt(r_s, k_s), 0.0), v))
            e_last = jnp.exp(c_last - c_mid)
            st_ref[j] = st * jnp.exp(c_last) + _dot_tn(u, al_s * e_last) + _dot_tn(v, k_s * e_last)
            m = jnp.mean(o, -1, keepdims=True)
            d = o - m
            var = jnp.mean(d * d, -1, keepdims=True)
            o = d * lax.rsqrt(var + RWKV_GN_EPS) * lng_ref[:, hs] + lnb_ref[:, hs]
            bonus = jnp.sum(r * k * rk_ref[:, hs], -1, keepdims=True) * v
            outs.append((o + bonus) * g_ref[sl, hs])
        o_ref[sl, :] = jnp.concatenate(outs, axis=-1)


def _rwkv(r, w, k, v, kk, a, g, r_k, ln_g, ln_b, bsz, seq, tb=256):
    nt = seq // tb
    npair = RWKV_WIDTH // LANE
    tok = pl.BlockSpec((tb, LANE), lambda b, h, t: (b * nt + t, h))
    par = pl.BlockSpec((1, LANE), lambda b, h, t: (0, h))
    return pl.pallas_call(
        functools.partial(_rwkv_kernel, tb=tb),
        grid=(bsz, npair, nt),
        in_specs=[tok] * 7 + [par] * 3,
        out_specs=tok,
        out_shape=jax.ShapeDtypeStruct((bsz * seq, RWKV_WIDTH), F32),
        scratch_shapes=[pltpu.VMEM((LANE // RWKV_HD, RWKV_HD, RWKV_HD), F32)],
        compiler_params=pltpu.CompilerParams(
            dimension_semantics=("parallel", "parallel", "arbitrary"),
            vmem_limit_bytes=VMEM_LIMIT),
        name="rwkv",
    )(r, w, k, v, kk, a, g, r_k, ln_g, ln_b)


def _merge_kernel(x_ref, gt_ref, ogla_ref, ogdn_ref, orwkv_ref, wgla_ref, wgdn_ref, wrwkv_ref, wout_ref,
                  g_ref, b_ref, o_ref):
    d = D_MODEL
    gt = gt_ref[...]
    merged = (_sigmoid(gt[:, 0:d]) * _dot(ogla_ref[...], wgla_ref[...])
              + _sigmoid(gt[:, d:2 * d]) * _dot(ogdn_ref[...], wgdn_ref[...])
              + _sigmoid(gt[:, 2 * d:3 * d]) * _dot(orwkv_ref[...], wrwkv_ref[...]))
    mix = _dot(merged, wout_ref[...])
    o_ref[...] = _layer_norm(DN_ALPHA * x_ref[...] + mix, g_ref[...], b_ref[...])


def _merge(x, gates, ogla, ogdn, orwkv, wgla, wgdn, wrwkv, wout, g, b, tm=256):
    t = x.shape[0]
    d = D_MODEL
    tok = lambda w: pl.BlockSpec((tm, w), lambda i: (i, 0))
    full = lambda shape: pl.BlockSpec(shape, lambda i: (0, 0))
    return pl.pallas_call(
        _merge_kernel,
        grid=(t // tm,),
        in_specs=[tok(d), tok(N_BRANCH * d), tok(GLA_DV), tok(GDN_WIDTH), tok(RWKV_WIDTH),
                  full((GLA_DV, d)), full((GDN_WIDTH, d)), full((RWKV_WIDTH, d)), full((d, d)),
                  full((1, d)), full((1, d))],
        out_specs=tok(d),
        out_shape=jax.ShapeDtypeStruct((t, d), F32),
        compiler_params=pltpu.CompilerParams(dimension_semantics=("parallel",),
                                             vmem_limit_bytes=VMEM_LIMIT),
        name="merge",
    )(x, gates, ogla, ogdn, orwkv, wgla, wgdn, wrwkv, wout, g, b)


def _mlp_kernel(x_ref, wu_ref, wd_ref, g_ref, b_ref, o_ref, acc_ref):
    j = pl.program_id(1)
    x = x_ref[...]
    hid = jnp.maximum(_dot(x, wu_ref[...]), 0.0)
    part = _dot(hid * hid, wd_ref[...])

    @pl.when(j == 0)
    def _():
        acc_ref[...] = part

    @pl.when(j > 0)
    def _():
        acc_ref[...] += part

    @pl.when(j == pl.num_programs(1) - 1)
    def _():
        o_ref[...] = _layer_norm(DN_ALPHA * x + acc_ref[...], g_ref[...], b_ref[...])


def _mlp(x, wu, wd, g, b, tm=512, tf=1024):
    t = x.shape[0]
    d = D_MODEL
    return pl.pallas_call(
        _mlp_kernel,
        grid=(t // tm, D_FF // tf),
        in_specs=[pl.BlockSpec((tm, d), lambda i, j: (i, 0)),
                  pl.BlockSpec((d, tf), lambda i, j: (0, j)),
                  pl.BlockSpec((tf, d), lambda i, j: (j, 0)),
                  pl.BlockSpec((1, d), lambda i, j: (0, 0)),
                  pl.BlockSpec((1, d), lambda i, j: (0, 0))],
        out_specs=pl.BlockSpec((tm, d), lambda i, j: (i, 0)),
        out_shape=jax.ShapeDtypeStruct((t, d), F32),
        scratch_shapes=[pltpu.VMEM((tm, d), F32)],
        compiler_params=pltpu.CompilerParams(dimension_semantics=("parallel", "arbitrary"),
                                             vmem_limit_bytes=VMEM_LIMIT),
        name="mlp",
    )(x, wu, wd, g, b)


def _pad_cols(w, width):
    return jnp.pad(w, ((0, 0), (0, width - w.shape[1])))


def _pad_rows(w, height):
    return jnp.pad(w, ((0, height - w.shape[0]), (0, 0)))


def _split_w_in(w_in):
    o = _IN_OFFS
    col = lambda i: w_in[:, o[i]:o[i + 1]]
    gq, gk, gv, gga, ggr, dqkv, da, db, dz, rp, gates = (col(i) for i in range(11))
    w_gla = jnp.concatenate([gq, gk, gv, ggr, _pad_cols(gga, LANE)], axis=1)
    w_gdn = jnp.concatenate([dqkv, dz, _pad_cols(jnp.concatenate([da, db], axis=1), LANE)], axis=1)
    s = 3 * RWKV_WIDTH
    w_rwkv = jnp.concatenate([rp[:, :s],
                              _pad_cols(rp[:, s:s + RWKV_DECAY_RANK], LANE),
                              _pad_cols(rp[:, s + RWKV_DECAY_RANK:s + RWKV_DECAY_RANK + RWKV_A_RANK], LANE),
                              rp[:, s + RWKV_DECAY_RANK + RWKV_A_RANK:]], axis=1)
    return tuple(w.astype(BF16) for w in (w_gla, w_gdn, w_rwkv, gates))


def _pad_mu(mu):
    s = 3 * RWKV_WIDTH
    z = jnp.zeros((LANE - RWKV_DECAY_RANK,), F32)
    return jnp.concatenate([mu[:s], mu[s:s + RWKV_DECAY_RANK], z,
                            mu[s + RWKV_DECAY_RANK:s + RWKV_DECAY_RANK + RWKV_A_RANK], z,
                            mu[s + RWKV_DECAY_RANK + RWKV_A_RANK:]])[None, :]


def kernel(x, w_in, gla_gate_up, gla_gate_bias, gla_norm_g, gdn_conv, gdn_a_log, gdn_dt_bias, gdn_norm_g, rwkv_mu, rwkv_w0, rwkv_w2, rwkv_a0, rwkv_a2, rwkv_g2, rwkv_k_k, rwkv_k_a, rwkv_r_k, rwkv_ln_g, rwkv_ln_b, w_br_gla, w_br_gdn, w_br_rwkv, w_out, ln1_g, ln1_b, w_up, w_down, ln2_g, ln2_b):
    bsz, seq, d = x.shape
    xt = x.reshape(bsz * seq, d)
    head_id = jnp.arange(RWKV_WIDTH) // RWKV_HD
    hsum = (head_id[:, None] == head_id[None, :]).astype(F32)
    row = lambda v: v[None, :]
    for l in range(DEPTH):
        w_gla, w_gdn, w_rwkv, w_gates = _split_w_in(w_in[l])
        p_gla = _project(xt, w_gla)
        p_gdn = _project(xt, w_gdn)
        p_rwkv = _project(xt, w_rwkv)
        p_gates = _project(xt, w_gates)

        o_gla = _gla(p_gla, _pad_rows(gla_gate_up[l], LANE).astype(BF16), row(gla_gate_bias[l]),
                     row(gla_norm_g[l]), bsz, seq)

        cw = gdn_conv[l].reshape(GDN_CONV, 3, GDN_WIDTH).transpose(1, 0, 2)
        o_gdn = _gdn(p_gdn, gdn_a_log[l], gdn_dt_bias[l], cw, row(gdn_norm_g[l]), bsz, seq)

        r, lw, k, v, kk, a, g = _rwkv_prep(
            p_rwkv, _pad_mu(rwkv_mu[l]), row(rwkv_w0[l]), _pad_rows(rwkv_w2[l], LANE).astype(BF16),
            row(rwkv_a0[l]), _pad_rows(rwkv_a2[l], LANE).astype(BF16), rwkv_g2[l].astype(BF16),
            row(rwkv_k_k[l]), row(rwkv_k_a[l]), hsum, bsz, seq)
        o_rwkv = _rwkv(r, lw, k, v, kk, a, g, row(rwkv_r_k[l]), row(rwkv_ln_g[l]), row(rwkv_ln_b[l]),
                       bsz, seq)

        xt = _merge(xt, p_gates, o_gla, o_gdn, o_rwkv, w_br_gla[l].astype(BF16), w_br_gdn[l].astype(BF16),
                    w_br_rwkv[l].astype(BF16), w_out[l].astype(BF16), row(ln1_g[l]), row(ln1_b[l]))
        xt = _mlp(xt, w_up[l].astype(BF16), w_down[l].astype(BF16), row(ln2_g[l]), row(ln2_b[l]))
    return xt.reshape(bsz, seq, d)
```

```python
import functools

import jax
import jax.numpy as jnp
from jax import lax
from jax.experimental import pallas as pl
from jax.experimental.pallas import tpu as pltpu

F32 = jnp.float32
BF16 = jnp.bfloat16

D_MODEL = 1024
DEPTH = 2
CHUNK = 64
GLA_HEADS = 4
GLA_DK = 512
GLA_DV = 1024
GLA_HK = 128
GLA_HV = 256
GLA_GATE_RANK = 16
GLA_GATE_TEMP = 16.0
GDN_HEADS = 4
GDN_HD = 128
GDN_WIDTH = 512
GDN_CONV = 4
RWKV_HD = 64
RWKV_WIDTH = 512
RWKV_HEADS = 8
RWKV_DECAY_RANK = 64
RWKV_A_RANK = 64
RWKV_GATE_RANK = 128
RWKV_GN_EPS = 64e-5
N_BRANCH = 3
D_FF = 4 * D_MODEL
DN_ALPHA = (2 * DEPTH) ** 0.25
LN_EPS = 1e-5
RMS_EPS = 1e-6
L2_EPS = 1e-6

LANE = 128
SUBLANE = 8
VMEM_LIMIT = 56 * 1024 * 1024
UNIT = 2 * CHUNK

_IN_WIDTHS = (GLA_DK, GLA_DK, GLA_DV, GLA_GATE_RANK, GLA_DV,
              3 * GDN_WIDTH, GDN_HEADS, GDN_HEADS, GDN_WIDTH,
              3 * RWKV_WIDTH + RWKV_DECAY_RANK + RWKV_A_RANK + RWKV_GATE_RANK,
              N_BRANCH * D_MODEL)
_IN_OFFS = [0]
for _w in _IN_WIDTHS:
    _IN_OFFS.append(_IN_OFFS[-1] + _w)


def _dot(a, b):
    return jnp.dot(a.astype(BF16), b.astype(BF16), preferred_element_type=F32)


def _dot_nt(a, b):
    return lax.dot_general(a.astype(BF16), b.astype(BF16), (((1,), (1,)), ((), ())),
                           preferred_element_type=F32)


def _dot_tn(a, b):
    return lax.dot_general(a.astype(BF16), b.astype(BF16), (((0,), (0,)), ((), ())),
                           preferred_element_type=F32)


def _split_bf16(x):
    hi = x.astype(BF16)
    return hi, (x - hi.astype(F32)).astype(BF16)


def _cumsum_rows(tri, x):
    hi, lo = _split_bf16(x)
    return (jnp.dot(tri, hi, preferred_element_type=F32)
            + jnp.dot(tri, lo, preferred_element_type=F32))


def _cumsum_lanes(x, tri):
    hi, lo = _split_bf16(x)
    dims = (((1,), (1,)), ((), ()))
    return (lax.dot_general(hi, tri, dims, preferred_element_type=F32)
            + lax.dot_general(lo, tri, dims, preferred_element_type=F32))


def _sigmoid(x):
    return 1.0 / (1.0 + jnp.exp(-x))


def _silu(x):
    return x * _sigmoid(x)


def _softplus(x):
    return jnp.maximum(x, 0.0) + jnp.log1p(jnp.exp(-jnp.abs(x)))


def _log_sigmoid(x):
    return -_softplus(-x)


def _chunk_masks(n):
    row = lax.broadcasted_iota(jnp.int32, (n, n), 0)
    col = lax.broadcasted_iota(jnp.int32, (n, n), 1)
    same = (row // CHUNK) == (col // CHUNK)
    return same & (row >= col), same & (row > col), row == col


def _per_chunk_rows(x, offset):
    w = x.shape[1]
    return jnp.concatenate(
        [jnp.broadcast_to(x[c * CHUNK + offset:c * CHUNK + offset + 1], (CHUNK, w))
         for c in range(UNIT // CHUNK)], axis=0)


def _unit_lower_inverse(a_list, eye):
    n = eye.shape[0]
    ps = [-a for a in a_list]
    ts = [eye + p for p in ps]
    ps = [_dot(p, p) for p in ps]
    for _ in range(CHUNK.bit_length() - 3):
        prods = [_dot(jnp.concatenate([t, p], axis=0), p) for t, p in zip(ts, ps)]
        ts = [t + pr[0:n] for t, pr in zip(ts, prods)]
        ps = [pr[n:2 * n] for pr in prods]
    return [t + _dot(t, p) for t, p in zip(ts, ps)]


def _layer_norm(y, g, b):
    mu = jnp.mean(y, -1, keepdims=True)
    d = y - mu
    var = jnp.mean(d * d, -1, keepdims=True)
    return d * lax.rsqrt(var + LN_EPS) * g + b


def _block_tril(n):
    idx = jnp.arange(n)
    same = (idx[:, None] // CHUNK) == (idx[None, :] // CHUNK)
    return (same & (idx[:, None] >= idx[None, :])).astype(BF16)


def _proj_kernel(x_ref, w_ref, o_ref):
    o_ref[...] = jnp.dot(x_ref[...].astype(BF16), w_ref[...],
                         preferred_element_type=F32).astype(o_ref.dtype)


def _project(x, w, tm=512):
    m, k = x.shape
    n = w.shape[1]
    return pl.pallas_call(
        _proj_kernel,
        grid=(m // tm,),
        in_specs=[pl.BlockSpec((tm, k), lambda i: (i, 0)),
                  pl.BlockSpec((k, n), lambda i: (0, 0))],
        out_specs=pl.BlockSpec((tm, n), lambda i: (i, 0)),
        out_shape=jax.ShapeDtypeStruct((m, n), F32),
        compiler_params=pltpu.CompilerParams(dimension_semantics=("parallel",),
                                             vmem_limit_bytes=VMEM_LIMIT),
        name="in_proj",
    )(x, w)


def _gla_kernel(p_ref, gup_ref, gb_ref, ng_ref, tri_ref, o_ref, st_ref, *, tb):
    @pl.when(pl.program_id(1) == 0)
    def _():
        st_ref[...] = jnp.zeros_like(st_ref)

    dk, dv, hk, hv = GLA_DK, GLA_DV, GLA_HK, GLA_HV
    incl, _, _ = _chunk_masks(UNIT)
    gd = p_ref[:, 2 * dk + 2 * dv:2 * dk + 2 * dv + LANE]
    log_a = _log_sigmoid(_dot(gd, gup_ref[...]) + gb_ref[...]) * (1.0 / GLA_GATE_TEMP)
    b_all = _cumsum_rows(tri_ref[...], log_a)
    mid = CHUNK // 2
    nchunk = UNIT // CHUNK
    probs = [(h, u) for u in range(tb // UNIT) for h in range(GLA_HEADS)]

    pre = []
    for h, u in probs:
        rows = slice(u * UNIT, (u + 1) * UNIT)
        b = b_all[rows, h * hk:(h + 1) * hk]
        b_mid = _per_chunk_rows(b, mid)
        b_last = _per_chunk_rows(b, CHUNK - 1)
        qs = p_ref[rows, h * hk:(h + 1) * hk] * (hk ** -0.5) * jnp.exp(b - b_mid)
        ks = p_ref[rows, dk + h * hk:dk + (h + 1) * hk] * jnp.exp(b_mid - b)
        pre.append(dict(qs=qs, ks=ks, qe=qs * jnp.exp(b_mid), kd=ks * jnp.exp(b_last - b_mid),
                        e_last=jnp.exp(b_last),
                        v=p_ref[rows, 2 * dk + h * hv:2 * dk + (h + 1) * hv]))

    attns = [jnp.where(incl, _dot_nt(d["qs"], d["ks"]), 0.0) for d in pre]
    intras = [_dot(a, d["v"]) for a, d in zip(attns, pre)]
    kvs = [[_dot_tn(d["v"][c * CHUNK:(c + 1) * CHUNK], d["kd"][c * CHUNK:(c + 1) * CHUNK])
            for c in range(nchunk)] for d in pre]

    states = []
    for i, (h, u) in enumerate(probs):
        st = st_ref[h]
        per_chunk = []
        for c in range(nchunk):
            per_chunk.append(st)
            st = st * pre[i]["e_last"][c * CHUNK:c * CHUNK + 1] + kvs[i][c]
        st_ref[h] = st
        states.append(per_chunk)

    for i, (h, u) in enumerate(probs):
        for c in range(nchunk):
            rc = slice(c * CHUNK, (c + 1) * CHUNK)
            o = _dot_nt(pre[i]["qe"][rc], states[i][c]) + intras[i][rc]
            o = o * lax.rsqrt(jnp.mean(o * o, -1, keepdims=True) + RMS_EPS) * ng_ref[...]
            out_rows = slice(u * UNIT + c * CHUNK, u * UNIT + (c + 1) * CHUNK)
            og = p_ref[out_rows, 2 * dk + dv + h * hv:2 * dk + dv + (h + 1) * hv]
            o_ref[out_rows, h * hv:(h + 1) * hv] = o * _silu(og)


def _gla(proj, gup, gb, ng, bsz, seq, tb=256):
    nt = seq // tb
    width = proj.shape[1]
    full = lambda shape: pl.BlockSpec(shape, lambda b, t: (0,) * len(shape))
    return pl.pallas_call(
        functools.partial(_gla_kernel, tb=tb),
        grid=(bsz, nt),
        in_specs=[pl.BlockSpec((tb, width), lambda b, t: (b * nt + t, 0)),
                  full((LANE, GLA_DK)), full((1, GLA_DK)), full((1, GLA_HV)), full((tb, tb))],
        out_specs=pl.BlockSpec((tb, GLA_DV), lambda b, t: (b * nt + t, 0)),
        out_shape=jax.ShapeDtypeStruct((bsz * seq, GLA_DV), F32),
        scratch_shapes=[pltpu.VMEM((GLA_HEADS, GLA_HV, GLA_HK), F32)],
        compiler_params=pltpu.CompilerParams(dimension_semantics=("parallel", "arbitrary"),
                                             vmem_limit_bytes=VMEM_LIMIT),
        name="gla",
    )(proj, gup, gb, ng, _block_tril(tb))


def _gdn_kernel(p_ref, cw_ref, alog_row_ref, dtb_row_ref, alog_col_ref, dtb_col_ref, ng_ref, tri_ref,
                o_ref, st_ref, buf_ref, *, tb):
    @pl.when(pl.program_id(1) == 0)
    def _():
        st_ref[...] = jnp.zeros_like(st_ref)
        buf_ref[0:SUBLANE, :] = jnp.zeros((SUBLANE, buf_ref.shape[1]), F32)

    wq = 3 * GDN_WIDTH
    hd = GDN_HD
    x = p_ref[:, 0:wq]
    buf_ref[SUBLANE:SUBLANE + tb, :] = x
    y = x * cw_ref[GDN_CONV - 1:GDN_CONV, :]
    for j in range(1, GDN_CONV):
        y = y + buf_ref[SUBLANE - j:SUBLANE - j + tb, :] * cw_ref[GDN_CONV - 1 - j:GDN_CONV - j, :]
    buf_ref[0:SUBLANE, :] = x[tb - SUBLANE:tb]
    qkv = _silu(y)

    ab = p_ref[:, wq + GDN_WIDTH:wq + GDN_WIDTH + LANE]
    g_cols = -jnp.exp(alog_row_ref[...]) * _softplus(ab + dtb_row_ref[...])
    g_rows = -jnp.exp(alog_col_ref[...]) * _softplus(ab.T[0:SUBLANE] + dtb_col_ref[...])
    gam_cols = _cumsum_rows(tri_ref[...], g_cols)
    gam_rows = _cumsum_lanes(g_rows, tri_ref[...])
    beta_cols = _sigmoid(ab)

    incl, strict, diag = _chunk_masks(UNIT)
    eye = diag.astype(F32)
    nchunk = UNIT // CHUNK
    probs = [(h, u) for u in range(tb // UNIT) for h in range(GDN_HEADS)]

    pre = []
    for h, u in probs:
        rows = slice(u * UNIT, (u + 1) * UNIT)
        q = qkv[rows, h * hd:(h + 1) * hd]
        k = qkv[rows, GDN_WIDTH + h * hd:GDN_WIDTH + (h + 1) * hd]
        v = qkv[rows, 2 * GDN_WIDTH + h * hd:2 * GDN_WIDTH + (h + 1) * hd]
        q = q * lax.rsqrt(jnp.sum(q * q, -1, keepdims=True) + L2_EPS) * (hd ** -0.5)
        k = k * lax.rsqrt(jnp.sum(k * k, -1, keepdims=True) + L2_EPS)
        gam = gam_cols[rows, h:h + 1]
        gam_r = gam_rows[h:h + 1, rows]
        beta = beta_cols[rows, GDN_HEADS + h:GDN_HEADS + h + 1]
        g_last = _per_chunk_rows(gam, CHUNK - 1)
        e_gam = jnp.exp(gam)
        kb = k * beta
        pre.append(dict(
            q=q, k=k, kb=kb, qe=q * e_gam,
            decay=jnp.where(incl, jnp.exp(jnp.minimum(gam - gam_r, 0.0)), 0.0),
            rhs=jnp.concatenate([v * beta, kb * e_gam], axis=1),
            kd=k * jnp.exp(g_last - gam), e_last=jnp.exp(g_last)))

    kqs = [_dot_nt(jnp.concatenate([d["kb"], d["q"]], axis=0), d["k"]) for d in pre]
    t_invs = _unit_lower_inverse(
        [jnp.where(strict, kq[0:UNIT] * d["decay"], 0.0) for kq, d in zip(kqs, pre)], eye)
    uws = [_dot(t, d["rhs"]) for t, d in zip(t_invs, pre)]
    ros = [_dot(kq[UNIT:2 * UNIT] * d["decay"], uw) for kq, d, uw in zip(kqs, pre, uws)]
    r_mats = [d["qe"] - ro[:, hd:2 * hd] for d, ro in zip(pre, ros)]
    qps = [[_dot_tn(d["kd"][c * CHUNK:(c + 1) * CHUNK],
                    jnp.concatenate([uw[c * CHUNK:(c + 1) * CHUNK, 0:hd],
                                     -uw[c * CHUNK:(c + 1) * CHUNK, hd:2 * hd]], axis=1))
            for c in range(nchunk)] for d, uw in zip(pre, uws)]

    outs = {}
    for u in range(tb // UNIT):
        for c in range(nchunk):
            rc = slice(c * CHUNK, (c + 1) * CHUNK)
            for h in range(GDN_HEADS):
                i = u * GDN_HEADS + h
                st = st_ref[h]
                outs[(i, c)] = _dot(r_mats[i][rc], st) + ros[i][rc, 0:hd]
                qp = qps[i][c]
                st_ref[h] = st * pre[i]["e_last"][c * CHUNK:c * CHUNK + 1] + _dot(qp[:, hd:2 * hd], st) + qp[:, 0:hd]

    for i, (h, u) in enumerate(probs):
        for c in range(nchunk):
            o = outs[(i, c)]
            o = o * lax.rsqrt(jnp.mean(o * o, -1, keepdims=True) + RMS_EPS) * ng_ref[...]
            out_rows = slice(u * UNIT + c * CHUNK, u * UNIT + (c + 1) * CHUNK)
            z = p_ref[out_rows, wq + h * hd:wq + (h + 1) * hd]
            o_ref[out_rows, h * hd:(h + 1) * hd] = o * _silu(z)


def _gdn(proj, cw, alog, dtb, ng, bsz, seq, tb=256):
    nt = seq // tb
    width = proj.shape[1]
    full = lambda shape: pl.BlockSpec(shape, lambda b, t: (0,) * len(shape))
    lane_row = lambda v: jnp.pad(v, (0, LANE - v.shape[0]))[None, :]
    sub_col = lambda v: jnp.pad(v, (0, SUBLANE - v.shape[0]))[:, None]
    return pl.pallas_call(
        functools.partial(_gdn_kernel, tb=tb),
        grid=(bsz, nt),
        in_specs=[pl.BlockSpec((tb, width), lambda b, t: (b * nt + t, 0)),
                  full((GDN_CONV, 3 * GDN_WIDTH)),
                  full((1, LANE)), full((1, LANE)), full((SUBLANE, 1)), full((SUBLANE, 1)),
                  full((1, GDN_HD)), full((tb, tb))],
        out_specs=pl.BlockSpec((tb, GDN_WIDTH), lambda b, t: (b * nt + t, 0)),
        out_shape=jax.ShapeDtypeStruct((bsz * seq, GDN_WIDTH), F32),
        scratch_shapes=[pltpu.VMEM((GDN_HEADS, GDN_HD, GDN_HD), F32),
                        pltpu.VMEM((tb + SUBLANE, 3 * GDN_WIDTH), F32)],
        compiler_params=pltpu.CompilerParams(dimension_semantics=("parallel", "arbitrary"),
                                             vmem_limit_bytes=VMEM_LIMIT),
        name="gdn",
    )(proj, cw, lane_row(alog), lane_row(dtb), sub_col(alog), sub_col(dtb), ng, _block_tril(tb))


def _stack_heads(x, head0):
    return jnp.concatenate([jnp.where(head0, x, 0.0), jnp.where(head0, 0.0, x)], axis=0)


def _rwkv_kernel(p_ref, mu_ref, w0_ref, w2_ref, a0_ref, a2_ref, g2_ref, kk_ref, ka_ref,
                 rk_ref, lng_ref, lnb_ref, tri_ref, o_ref, st_ref, buf_ref, *, tb):
    @pl.when(pl.program_id(1) == 0)
    def _():
        st_ref[...] = jnp.zeros_like(st_ref)
        buf_ref[0:SUBLANE, :] = jnp.zeros((SUBLANE, buf_ref.shape[1]), F32)

    p = p_ref[...]
    buf_ref[SUBLANE:SUBLANE + tb, :] = p
    prev = buf_ref[SUBLANE - 1:SUBLANE - 1 + tb, :]
    buf_ref[0:SUBLANE, :] = p[tb - SUBLANE:tb]
    p = p + (prev - p) * mu_ref[...]
    wd = RWKV_WIDTH
    r_all = p[:, 0:wd]
    k_in = p[:, wd:2 * wd]
    v_all = p[:, 2 * wd:3 * wd]
    d_in = p[:, 3 * wd:3 * wd + LANE]
    a_in = p[:, 3 * wd + LANE:3 * wd + 2 * LANE]
    g_in = p[:, 3 * wd + 2 * LANE:3 * wd + 3 * LANE]
    lw_all = -jnp.exp(-_softplus(-(w0_ref[...] + _dot(jnp.tanh(d_in), w2_ref[...]))) - 0.5)
    a_all = _sigmoid(a0_ref[...] + _dot(a_in, a2_ref[...]))
    g_all = _dot(_sigmoid(g_in), g2_ref[...])
    kkraw_all = k_in * kk_ref[...]
    k_all = k_in * (1.0 + (a_all - 1.0) * ka_ref[...])
    cs_all = _cumsum_rows(tri_ref[...], lw_all)

    incl, strict, diag = _chunk_masks(UNIT)
    eye = diag.astype(F32)
    head0 = lax.broadcasted_iota(jnp.int32, (CHUNK, LANE), 1) < RWKV_HD
    srow = lax.broadcasted_iota(jnp.int32, (UNIT, LANE), 0) < CHUNK
    slane = lax.broadcasted_iota(jnp.int32, (UNIT, LANE), 1) < RWKV_HD
    own = srow == slane
    mid = CHUNK // 2
    inv_hd = 1.0 / RWKV_HD

    npair = RWKV_WIDTH // LANE
    nchunk = tb // CHUNK
    probs = [(pair, c) for c in range(nchunk) for pair in range(npair)]
    pslice = lambda pair: slice(pair * LANE, (pair + 1) * LANE)
    cslice = lambda c: slice(c * CHUNK, (c + 1) * CHUNK)

    pre = []
    for pair, c in probs:
        rs, ps = cslice(c), pslice(pair)
        r = r_all[rs, ps]
        k = k_all[rs, ps]
        lw = lw_all[rs, ps]
        cs = cs_all[rs, ps]
        c_mid = cs[mid:mid + 1]
        c_last = cs[CHUNK - 1:CHUNK]
        e_out = jnp.exp(c_mid - cs)
        kk_st = _stack_heads(kkraw_all[rs, ps], head0)
        kk_st = kk_st * lax.rsqrt(jnp.sum(kk_st * kk_st, -1, keepdims=True) + L2_EPS)
        kk = kk_st[0:CHUNK] + kk_st[CHUNK:UNIT]
        pre.append(dict(
            r_s=_stack_heads(r * jnp.exp(cs - c_mid), head0),
            kk_s=_stack_heads(kk * jnp.exp(cs - lw - c_mid), head0),
            al_s=_stack_heads(kk * a_all[rs, ps] * e_out, head0),
            k_s=_stack_heads(k * e_out, head0),
            v_s=_stack_heads(v_all[rs, ps], head0),
            e_mid=jnp.exp(c_mid), e_last_mid=jnp.exp(c_last - c_mid), e_last=jnp.exp(c_last),
            rkr=_stack_heads(r * k * rk_ref[:, ps], head0)))

    grams = [_dot_nt(jnp.concatenate([d["kk_s"], d["r_s"]], axis=0),
                     jnp.concatenate([d["al_s"], d["k_s"]], axis=0)) for d in pre]
    t_invs = _unit_lower_inverse([jnp.where(strict, g[0:UNIT, 0:UNIT], 0.0) for g in grams], eye)
    bvs = [_dot(jnp.where(strict, g[0:UNIT, UNIT:2 * UNIT], 0.0), d["v_s"]) for g, d in zip(grams, pre)]
    tkws = [_dot(t, jnp.concatenate([d["kk_s"] * d["e_mid"], bv], axis=1))
            for t, d, bv in zip(t_invs, pre, bvs)]
    zs = [_dot_tn(tkw, d["al_s"] * d["e_last_mid"]) for tkw, d in zip(tkws, pre)]
    q_mats = [_dot_tn(d["v_s"], d["k_s"] * d["e_last_mid"]) - z[UNIT:2 * UNIT] for d, z in zip(pre, zs)]
    x2s = [_dot(jnp.where(incl, g[UNIT:2 * UNIT, 0:UNIT], 0.0), tkw) for g, tkw in zip(grams, tkws)]
    r_mats = [d["r_s"] * d["e_mid"] - x2[:, 0:UNIT] for d, x2 in zip(pre, x2s)]
    o_intras = [_dot(jnp.where(incl, g[UNIT:2 * UNIT, UNIT:2 * UNIT], 0.0), d["v_s"]) - x2[:, UNIT:2 * UNIT]
                for g, d, x2 in zip(grams, pre, x2s)]

    o_sts = []
    for i, (pair, c) in enumerate(probs):
        st = st_ref[pair]
        o_sts.append(_dot_nt(r_mats[i], st) + o_intras[i])
        st_ref[pair] = st * pre[i]["e_last"] - _dot(st, zs[i][0:UNIT]) + q_mats[i]

    for i, (pair, c) in enumerate(probs):
        rs, ps = cslice(c), pslice(pair)
        o_st = o_sts[i]
        m = jnp.sum(o_st, -1, keepdims=True) * inv_hd
        d = jnp.where(own, o_st - m, 0.0)
        var = jnp.sum(d * d, -1, keepdims=True) * inv_hd
        y_st = d * lax.rsqrt(var + RWKV_GN_EPS)
        y_bonus = jnp.sum(pre[i]["rkr"], -1, keepdims=True) * pre[i]["v_s"]
        y = y_st[0:CHUNK] + y_st[CHUNK:UNIT]
        bonus = y_bonus[0:CHUNK] + y_bonus[CHUNK:UNIT]
        o_ref[rs, ps] = (y * lng_ref[:, ps] + lnb_ref[:, ps] + bonus) * g_all[rs, ps]


def _rwkv(proj, mu, w0, w2, a0, a2, g2, k_k, k_a, r_k, ln_g, ln_b, bsz, seq, tb=256):
    nt = seq // tb
    width = proj.shape[1]
    wd = RWKV_WIDTH
    full = lambda shape: pl.BlockSpec(shape, lambda b, t: (0,) * len(shape))
    return pl.pallas_call(
        functools.partial(_rwkv_kernel, tb=tb),
        grid=(bsz, nt),
        in_specs=[pl.BlockSpec((tb, width), lambda b, t: (b * nt + t, 0)),
                  full((1, width)), full((1, wd)), full((LANE, wd)), full((1, wd)), full((LANE, wd)),
                  full((LANE, wd)), full((1, wd)), full((1, wd)),
                  full((1, wd)), full((1, wd)), full((1, wd)), full((tb, tb))],
        out_specs=pl.BlockSpec((tb, wd), lambda b, t: (b * nt + t, 0)),
        out_shape=jax.ShapeDtypeStruct((bsz * seq, wd), F32),
        scratch_shapes=[pltpu.VMEM((wd // LANE, UNIT, LANE), F32),
                        pltpu.VMEM((tb + SUBLANE, width), F32)],
        compiler_params=pltpu.CompilerParams(dimension_semantics=("parallel", "arbitrary"),
                                             vmem_limit_bytes=VMEM_LIMIT),
        name="rwkv",
    )(proj, mu, w0, w2, a0, a2, g2, k_k, k_a, r_k, ln_g, ln_b, _block_tril(tb))


def _merge_kernel(x_ref, gt_ref, ogla_ref, ogdn_ref, orwkv_ref, wgla_ref, wgdn_ref, wrwkv_ref, wout_ref,
                  g_ref, b_ref, o_ref):
    d = D_MODEL
    gt = gt_ref[...]
    merged = (_sigmoid(gt[:, 0:d]) * _dot(ogla_ref[...], wgla_ref[...])
              + _sigmoid(gt[:, d:2 * d]) * _dot(ogdn_ref[...], wgdn_ref[...])
              + _sigmoid(gt[:, 2 * d:3 * d]) * _dot(orwkv_ref[...], wrwkv_ref[...]))
    mix = _dot(merged, wout_ref[...])
    o_ref[...] = _layer_norm(DN_ALPHA * x_ref[...] + mix, g_ref[...], b_ref[...])


def _merge(x, gates, ogla, ogdn, orwkv, wgla, wgdn, wrwkv, wout, g, b, tm=256):
    t = x.shape[0]
    d = D_MODEL
    tok = lambda w: pl.BlockSpec((tm, w), lambda i: (i, 0))
    full = lambda shape: pl.BlockSpec(shape, lambda i: (0, 0))
    return pl.pallas_call(
        _merge_kernel,
        grid=(t // tm,),
        in_specs=[tok(d), tok(N_BRANCH * d), tok(GLA_DV), tok(GDN_WIDTH), tok(RWKV_WIDTH),
                  full((GLA_DV, d)), full((GDN_WIDTH, d)), full((RWKV_WIDTH, d)), full((d, d)),
                  full((1, d)), full((1, d))],
        out_specs=tok(d),
        out_shape=jax.ShapeDtypeStruct((t, d), F32),
        compiler_params=pltpu.CompilerParams(dimension_semantics=("parallel",),
                                             vmem_limit_bytes=VMEM_LIMIT),
        name="merge",
    )(x, gates, ogla, ogdn, orwkv, wgla, wgdn, wrwkv, wout, g, b)


def _mlp_kernel(x_ref, wu_ref, wd_ref, g_ref, b_ref, o_ref, acc_ref):
    j = pl.program_id(1)
    x = x_ref[...]
    hid = jnp.maximum(_dot(x, wu_ref[...]), 0.0)
    part = _dot(hid * hid, wd_ref[...])

    @pl.when(j == 0)
    def _():
        acc_ref[...] = part

    @pl.when(j > 0)
    def _():
        acc_ref[...] += part

    @pl.when(j == pl.num_programs(1) - 1)
    def _():
        o_ref[...] = _layer_norm(DN_ALPHA * x + acc_ref[...], g_ref[...], b_ref[...])


def _mlp(x, wu, wd, g, b, tm=512, tf=1024):
    t = x.shape[0]
    d = D_MODEL
    return pl.pallas_call(
        _mlp_kernel,
        grid=(t // tm, D_FF // tf),
        in_specs=[pl.BlockSpec((tm, d), lambda i, j: (i, 0)),
                  pl.BlockSpec((d, tf), lambda i, j: (0, j)),
                  pl.BlockSpec((tf, d), lambda i, j: (j, 0)),
                  pl.BlockSpec((1, d), lambda i, j: (0, 0)),
                  pl.BlockSpec((1, d), lambda i, j: (0, 0))],
        out_specs=pl.BlockSpec((tm, d), lambda i, j: (i, 0)),
        out_shape=jax.ShapeDtypeStruct((t, d), F32),
        scratch_shapes=[pltpu.VMEM((tm, d), F32)],
        compiler_params=pltpu.CompilerParams(dimension_semantics=("parallel", "arbitrary"),
                                             vmem_limit_bytes=VMEM_LIMIT),
        name="mlp",
    )(x, wu, wd, g, b)


def _pad_cols(w, width):
    return jnp.pad(w, ((0, 0), (0, width - w.shape[1])))


def _pad_rows(w, height):
    return jnp.pad(w, ((0, height - w.shape[0]), (0, 0)))


def _split_w_in(w_in):
    o = _IN_OFFS
    col = lambda i: w_in[:, o[i]:o[i + 1]]
    gq, gk, gv, gga, ggr, dqkv, da, db, dz, rp, gates = (col(i) for i in range(11))
    w_gla = jnp.concatenate([gq, gk, gv, ggr, _pad_cols(gga, LANE)], axis=1)
    w_gdn = jnp.concatenate([dqkv, dz, _pad_cols(jnp.concatenate([da, db], axis=1), LANE)], axis=1)
    s = 3 * RWKV_WIDTH
    w_rwkv = jnp.concatenate([rp[:, :s],
                              _pad_cols(rp[:, s:s + RWKV_DECAY_RANK], LANE),
                              _pad_cols(rp[:, s + RWKV_DECAY_RANK:s + RWKV_DECAY_RANK + RWKV_A_RANK], LANE),
                              rp[:, s + RWKV_DECAY_RANK + RWKV_A_RANK:]], axis=1)
    return tuple(w.astype(BF16) for w in (w_gla, w_gdn, w_rwkv, gates))


def _pad_mu(mu):
    s = 3 * RWKV_WIDTH
    z = jnp.zeros((LANE - RWKV_DECAY_RANK,), F32)
    return jnp.concatenate([mu[:s], mu[s:s + RWKV_DECAY_RANK], z,
                            mu[s + RWKV_DECAY_RANK:s + RWKV_DECAY_RANK + RWKV_A_RANK], z,
                            mu[s + RWKV_DECAY_RANK + RWKV_A_RANK:]])[None, :]


def kernel(x, w_in, gla_gate_up, gla_gate_bias, gla_norm_g, gdn_conv, gdn_a_log, gdn_dt_bias, gdn_norm_g, rwkv_mu, rwkv_w0, rwkv_w2, rwkv_a0, rwkv_a2, rwkv_g2, rwkv_k_k, rwkv_k_a, rwkv_r_k, rwkv_ln_g, rwkv_ln_b, w_br_gla, w_br_gdn, w_br_rwkv, w_out, ln1_g, ln1_b, w_up, w_down, ln2_g, ln2_b):
    bsz, seq, d = x.shape
    xt = x.reshape(bsz * seq, d)
    row = lambda v: v[None, :]
    for l in range(DEPTH):
        w_gla, w_gdn, w_rwkv, w_gates = _split_w_in(w_in[l])
        p_gla = _project(xt, w_gla)
        p_gdn = _project(xt, w_gdn)
        p_rwkv = _project(xt, w_rwkv)
        p_gates = _project(xt, w_gates)

        o_gla = _gla(p_gla, _pad_rows(gla_gate_up[l], LANE).astype(BF16), row(gla_gate_bias[l]),
                     row(gla_norm_g[l]), bsz, seq)
        o_gdn = _gdn(p_gdn, gdn_conv[l], gdn_a_log[l], gdn_dt_bias[l], row(gdn_norm_g[l]), bsz, seq)
        o_rwkv = _rwkv(
            p_rwkv, _pad_mu(rwkv_mu[l]), row(rwkv_w0[l]), _pad_rows(rwkv_w2[l], LANE).astype(BF16),
            row(rwkv_a0[l]), _pad_rows(rwkv_a2[l], LANE).astype(BF16), rwkv_g2[l].astype(BF16),
            row(rwkv_k_k[l]), row(rwkv_k_a[l]), row(rwkv_r_k[l]), row(rwkv_ln_g[l]), row(rwkv_ln_b[l]),
            bsz, seq)

        xt = _merge(xt, p_gates, o_gla, o_gdn, o_rwkv, w_br_gla[l].astype(BF16), w_br_gdn[l].astype(BF16),
                    w_br_rwkv[l].astype(BF16), w_out[l].astype(BF16), row(ln1_g[l]), row(ln1_b[l]))
        xt = _mlp(xt, w_up[l].astype(BF16), w_down[l].astype(BF16), row(ln2_g[l]), row(ln2_b[l]))
    return xt.reshape(bsz, seq, d)
```

```python
import functools

import jax
import jax.numpy as jnp
from jax import lax
from jax.experimental import pallas as pl
from jax.experimental.pallas import tpu as pltpu

F32 = jnp.float32
BF16 = jnp.bfloat16

D_MODEL = 1024
DEPTH = 2
CHUNK = 64
GLA_HEADS = 4
GLA_DK = 512
GLA_DV = 1024
GLA_HK = 128
GLA_HV = 256
GLA_GATE_RANK = 16
GLA_GATE_TEMP = 16.0
GDN_HEADS = 4
GDN_HD = 128
GDN_WIDTH = 512
GDN_CONV = 4
RWKV_HD = 64
RWKV_WIDTH = 512
RWKV_HEADS = 8
RWKV_DECAY_RANK = 64
RWKV_A_RANK = 64
RWKV_GATE_RANK = 128
RWKV_GN_EPS = 64e-5
N_BRANCH = 3
D_FF = 4 * D_MODEL
DN_ALPHA = (2 * DEPTH) ** 0.25
LN_EPS = 1e-5
RMS_EPS = 1e-6
L2_EPS = 1e-6

LANE = 128
SUBLANE = 8
VMEM_LIMIT = 56 * 1024 * 1024
UNIT = 2 * CHUNK

_IN_WIDTHS = (GLA_DK, GLA_DK, GLA_DV, GLA_GATE_RANK, GLA_DV,
              3 * GDN_WIDTH, GDN_HEADS, GDN_HEADS, GDN_WIDTH,
              3 * RWKV_WIDTH + RWKV_DECAY_RANK + RWKV_A_RANK + RWKV_GATE_RANK,
              N_BRANCH * D_MODEL)
_IN_OFFS = [0]
for _w in _IN_WIDTHS:
    _IN_OFFS.append(_IN_OFFS[-1] + _w)


def _dot(a, b):
    return jnp.dot(a.astype(BF16), b.astype(BF16), preferred_element_type=F32)


def _dot_nt(a, b):
    return lax.dot_general(a.astype(BF16), b.astype(BF16), (((1,), (1,)), ((), ())),
                           preferred_element_type=F32)


def _dot_tn(a, b):
    return lax.dot_general(a.astype(BF16), b.astype(BF16), (((0,), (0,)), ((), ())),
                           preferred_element_type=F32)


def _split_bf16(x):
    hi = x.astype(BF16)
    return hi, (x - hi.astype(F32)).astype(BF16)


def _cumsum_rows(tri, x):
    hi, lo = _split_bf16(x)
    return (jnp.dot(tri, hi, preferred_element_type=F32)
            + jnp.dot(tri, lo, preferred_element_type=F32))


def _cumsum_lanes(x, tri):
    hi, lo = _split_bf16(x)
    dims = (((1,), (1,)), ((), ()))
    return (lax.dot_general(hi, tri, dims, preferred_element_type=F32)
            + lax.dot_general(lo, tri, dims, preferred_element_type=F32))


def _sigmoid(x):
    return 1.0 / (1.0 + jnp.exp(-x))


def _silu(x):
    return x * _sigmoid(x)


def _softplus(x):
    return jnp.maximum(x, 0.0) + jnp.log1p(jnp.exp(-jnp.abs(x)))


def _log_sigmoid(x):
    return -_softplus(-x)


def _chunk_masks(n):
    row = lax.broadcasted_iota(jnp.int32, (n, n), 0)
    col = lax.broadcasted_iota(jnp.int32, (n, n), 1)
    same = (row // CHUNK) == (col // CHUNK)
    return same & (row >= col), same & (row > col), row == col


def _per_chunk_rows(x, offset):
    w = x.shape[1]
    return jnp.concatenate(
        [jnp.broadcast_to(x[c * CHUNK + offset:c * CHUNK + offset + 1], (CHUNK, w))
         for c in range(UNIT // CHUNK)], axis=0)


def _unit_lower_inverse(a_list, eye, as_rhs=lambda p: p):
    n = eye.shape[0]
    ps = [-a for a in a_list]
    ts = [eye + p for p in ps]
    ps = [_dot(p, as_rhs(p)) for p in ps]
    for _ in range(CHUNK.bit_length() - 3):
        prods = [_dot(jnp.concatenate([t, p], axis=0), as_rhs(p)) for t, p in zip(ts, ps)]
        ts = [t + pr[0:n] for t, pr in zip(ts, prods)]
        ps = [pr[n:2 * n] for pr in prods]
    return [t + _dot(t, as_rhs(p)) for t, p in zip(ts, ps)]


def _layer_norm(y, g, b):
    mu = jnp.mean(y, -1, keepdims=True)
    d = y - mu
    var = jnp.mean(d * d, -1, keepdims=True)
    return d * lax.rsqrt(var + LN_EPS) * g + b


def _block_tril(n):
    idx = jnp.arange(n)
    same = (idx[:, None] // CHUNK) == (idx[None, :] // CHUNK)
    return (same & (idx[:, None] >= idx[None, :])).astype(BF16)


def _proj_kernel(x_ref, w_ref, o_ref):
    o_ref[...] = jnp.dot(x_ref[...].astype(BF16), w_ref[...],
                         preferred_element_type=F32).astype(o_ref.dtype)


def _project(x, w, out_dtype=F32, tm=1024):
    m, k = x.shape
    n = w.shape[1]
    return pl.pallas_call(
        _proj_kernel,
        grid=(m // tm,),
        in_specs=[pl.BlockSpec((tm, k), lambda i: (i, 0)),
                  pl.BlockSpec((k, n), lambda i: (0, 0), pipeline_mode=pl.Buffered(1))],
        out_specs=pl.BlockSpec((tm, n), lambda i: (i, 0)),
        out_shape=jax.ShapeDtypeStruct((m, n), out_dtype),
        compiler_params=pltpu.CompilerParams(dimension_semantics=("parallel",),
                                             vmem_limit_bytes=VMEM_LIMIT),
        name="in_proj",
    )(x, w)


def _gla_kernel(p_ref, gup_ref, gb_ref, ng_ref, tri_ref, o_ref, st_ref, *, tb):
    @pl.when(pl.program_id(1) == 0)
    def _():
        st_ref[...] = jnp.zeros_like(st_ref)

    dk, dv, hk, hv = GLA_DK, GLA_DV, GLA_HK, GLA_HV
    incl, _, _ = _chunk_masks(UNIT)
    gd = p_ref[:, 2 * dk + 2 * dv:2 * dk + 2 * dv + LANE]
    log_a = _log_sigmoid(_dot(gd, gup_ref[...]) + gb_ref[...]) * (1.0 / GLA_GATE_TEMP)
    b_all = _cumsum_rows(tri_ref[...], log_a)
    mid = CHUNK // 2
    nchunk = UNIT // CHUNK
    probs = [(h, u) for u in range(tb // UNIT) for h in range(GLA_HEADS)]

    pre = []
    for h, u in probs:
        rows = slice(u * UNIT, (u + 1) * UNIT)
        b = b_all[rows, h * hk:(h + 1) * hk]
        b_mid = _per_chunk_rows(b, mid)
        b_last = _per_chunk_rows(b, CHUNK - 1)
        qs = p_ref[rows, h * hk:(h + 1) * hk] * (hk ** -0.5) * jnp.exp(b - b_mid)
        ks = p_ref[rows, dk + h * hk:dk + (h + 1) * hk] * jnp.exp(b_mid - b)
        pre.append(dict(qs=qs, ks=ks, qe=qs * jnp.exp(b_mid), kd=ks * jnp.exp(b_last - b_mid),
                        e_last=jnp.exp(b_last),
                        v=p_ref[rows, 2 * dk + h * hv:2 * dk + (h + 1) * hv]))

    attns = [jnp.where(incl, _dot_nt(d["qs"], d["ks"]), 0.0) for d in pre]
    intras = [_dot(a, d["v"]) for a, d in zip(attns, pre)]
    kvs = [[_dot_tn(d["v"][c * CHUNK:(c + 1) * CHUNK], d["kd"][c * CHUNK:(c + 1) * CHUNK])
            for c in range(nchunk)] for d in pre]

    states = []
    for i, (h, u) in enumerate(probs):
        st = st_ref[h]
        per_chunk = []
        for c in range(nchunk):
            per_chunk.append(st)
            st = st * pre[i]["e_last"][c * CHUNK:c * CHUNK + 1] + kvs[i][c]
        st_ref[h] = st
        states.append(per_chunk)

    for i, (h, u) in enumerate(probs):
        for c in range(nchunk):
            rc = slice(c * CHUNK, (c + 1) * CHUNK)
            o = _dot_nt(pre[i]["qe"][rc], states[i][c]) + intras[i][rc]
            o = o * lax.rsqrt(jnp.mean(o * o, -1, keepdims=True) + RMS_EPS) * ng_ref[...]
            out_rows = slice(u * UNIT + c * CHUNK, u * UNIT + (c + 1) * CHUNK)
            og = p_ref[out_rows, 2 * dk + dv + h * hv:2 * dk + dv + (h + 1) * hv]
            o_ref[out_rows, h * hv:(h + 1) * hv] = (o * _silu(og)).astype(o_ref.dtype)


def _gla(proj, gup, gb, ng, bsz, seq, tb=256):
    nt = seq // tb
    width = proj.shape[1]
    full = lambda shape: pl.BlockSpec(shape, lambda b, t: (0,) * len(shape))
    return pl.pallas_call(
        functools.partial(_gla_kernel, tb=tb),
        grid=(bsz, nt),
        in_specs=[pl.BlockSpec((tb, width), lambda b, t: (b * nt + t, 0)),
                  full((LANE, GLA_DK)), full((1, GLA_DK)), full((1, GLA_HV)), full((tb, tb))],
        out_specs=pl.BlockSpec((tb, GLA_DV), lambda b, t: (b * nt + t, 0)),
        out_shape=jax.ShapeDtypeStruct((bsz * seq, GLA_DV), BF16),
        scratch_shapes=[pltpu.VMEM((GLA_HEADS, GLA_HV, GLA_HK), F32)],
        compiler_params=pltpu.CompilerParams(dimension_semantics=("parallel", "arbitrary"),
                                             vmem_limit_bytes=VMEM_LIMIT),
        name="gla",
    )(proj, gup, gb, ng, _block_tril(tb))


def _gdn_kernel(p_ref, cw_ref, alog_row_ref, dtb_row_ref, alog_col_ref, dtb_col_ref, ng_ref, tri_ref,
                o_ref, st_ref, buf_ref, *, tb):
    @pl.when(pl.program_id(1) == 0)
    def _():
        st_ref[...] = jnp.zeros_like(st_ref)
        buf_ref[0:SUBLANE, :] = jnp.zeros((SUBLANE, buf_ref.shape[1]), F32)

    wq = 3 * GDN_WIDTH
    hd = GDN_HD
    x = p_ref[:, 0:wq]
    buf_ref[SUBLANE:SUBLANE + tb, :] = x
    y = x * cw_ref[GDN_CONV - 1:GDN_CONV, :]
    for j in range(1, GDN_CONV):
        y = y + buf_ref[SUBLANE - j:SUBLANE - j + tb, :] * cw_ref[GDN_CONV - 1 - j:GDN_CONV - j, :]
    buf_ref[0:SUBLANE, :] = x[tb - SUBLANE:tb]
    qkv = _silu(y)

    ab = p_ref[:, wq + GDN_WIDTH:wq + GDN_WIDTH + LANE]
    g_cols = -jnp.exp(alog_row_ref[...]) * _softplus(ab + dtb_row_ref[...])
    g_rows = -jnp.exp(alog_col_ref[...]) * _softplus(ab.T[0:SUBLANE] + dtb_col_ref[...])
    gam_cols = _cumsum_rows(tri_ref[...], g_cols)
    gam_rows = _cumsum_lanes(g_rows, tri_ref[...])
    beta_cols = _sigmoid(ab)

    incl, strict, diag = _chunk_masks(UNIT)
    eye = diag.astype(F32)
    nchunk = UNIT // CHUNK
    probs = [(h, u) for u in range(tb // UNIT) for h in range(GDN_HEADS)]

    pre = []
    for h, u in probs:
        rows = slice(u * UNIT, (u + 1) * UNIT)
        q = qkv[rows, h * hd:(h + 1) * hd]
        k = qkv[rows, GDN_WIDTH + h * hd:GDN_WIDTH + (h + 1) * hd]
        v = qkv[rows, 2 * GDN_WIDTH + h * hd:2 * GDN_WIDTH + (h + 1) * hd]
        q = q * lax.rsqrt(jnp.sum(q * q, -1, keepdims=True) + L2_EPS) * (hd ** -0.5)
        k = k * lax.rsqrt(jnp.sum(k * k, -1, keepdims=True) + L2_EPS)
        gam = gam_cols[rows, h:h + 1]
        gam_r = gam_rows[h:h + 1, rows]
        beta = beta_cols[rows, GDN_HEADS + h:GDN_HEADS + h + 1]
        g_last = _per_chunk_rows(gam, CHUNK - 1)
        e_gam = jnp.exp(gam)
        kb = k * beta
        pre.append(dict(
            q=q, k=k, kb=kb, qe=q * e_gam,
            decay=jnp.where(incl, jnp.exp(jnp.minimum(gam - gam_r, 0.0)), 0.0),
            rhs=jnp.concatenate([v * beta, kb * e_gam], axis=1),
            kd=k * jnp.exp(g_last - gam), e_last=jnp.exp(g_last)))

    kqs = [_dot_nt(jnp.concatenate([d["kb"], d["q"]], axis=0), d["k"]) for d in pre]
    t_invs = _unit_lower_inverse(
        [jnp.where(strict, kq[0:UNIT] * d["decay"], 0.0) for kq, d in zip(kqs, pre)], eye)
    uws = [_dot(t, d["rhs"]) for t, d in zip(t_invs, pre)]
    ros = [_dot(kq[UNIT:2 * UNIT] * d["decay"], uw) for kq, d, uw in zip(kqs, pre, uws)]
    r_mats = [d["qe"] - ro[:, hd:2 * hd] for d, ro in zip(pre, ros)]
    qps = [[_dot_tn(d["kd"][c * CHUNK:(c + 1) * CHUNK],
                    jnp.concatenate([uw[c * CHUNK:(c + 1) * CHUNK, 0:hd],
                                     -uw[c * CHUNK:(c + 1) * CHUNK, hd:2 * hd]], axis=1))
            for c in range(nchunk)] for d, uw in zip(pre, uws)]

    outs = {}
    for u in range(tb // UNIT):
        for c in range(nchunk):
            rc = slice(c * CHUNK, (c + 1) * CHUNK)
            for h in range(GDN_HEADS):
                i = u * GDN_HEADS + h
                st = st_ref[h]
                outs[(i, c)] = _dot(r_mats[i][rc], st) + ros[i][rc, 0:hd]
                qp = qps[i][c]
                st_ref[h] = st * pre[i]["e_last"][c * CHUNK:c * CHUNK + 1] + _dot(qp[:, hd:2 * hd], st) + qp[:, 0:hd]

    for i, (h, u) in enumerate(probs):
        for c in range(nchunk):
            o = outs[(i, c)]
            o = o * lax.rsqrt(jnp.mean(o * o, -1, keepdims=True) + RMS_EPS) * ng_ref[...]
            out_rows = slice(u * UNIT + c * CHUNK, u * UNIT + (c + 1) * CHUNK)
            z = p_ref[out_rows, wq + h * hd:wq + (h + 1) * hd]
            o_ref[out_rows, h * hd:(h + 1) * hd] = (o * _silu(z)).astype(o_ref.dtype)


def _gdn(proj, cw, alog, dtb, ng, bsz, seq, tb=256):
    nt = seq // tb
    width = proj.shape[1]
    full = lambda shape: pl.BlockSpec(shape, lambda b, t: (0,) * len(shape))
    lane_row = lambda v: jnp.pad(v, (0, LANE - v.shape[0]))[None, :]
    sub_col = lambda v: jnp.pad(v, (0, SUBLANE - v.shape[0]))[:, None]
    return pl.pallas_call(
        functools.partial(_gdn_kernel, tb=tb),
        grid=(bsz, nt),
        in_specs=[pl.BlockSpec((tb, width), lambda b, t: (b * nt + t, 0)),
                  full((GDN_CONV, 3 * GDN_WIDTH)),
                  full((1, LANE)), full((1, LANE)), full((SUBLANE, 1)), full((SUBLANE, 1)),
                  full((1, GDN_HD)), full((tb, tb))],
        out_specs=pl.BlockSpec((tb, GDN_WIDTH), lambda b, t: (b * nt + t, 0)),
        out_shape=jax.ShapeDtypeStruct((bsz * seq, GDN_WIDTH), BF16),
        scratch_shapes=[pltpu.VMEM((GDN_HEADS, GDN_HD, GDN_HD), F32),
                        pltpu.VMEM((tb + SUBLANE, 3 * GDN_WIDTH), F32)],
        compiler_params=pltpu.CompilerParams(dimension_semantics=("parallel", "arbitrary"),
                                             vmem_limit_bytes=VMEM_LIMIT),
        name="gdn",
    )(proj, cw, lane_row(alog), lane_row(dtb), sub_col(alog), sub_col(dtb), ng, _block_tril(tb))


def _stack_heads(x, head0):
    return jnp.concatenate([jnp.where(head0, x, 0.0), jnp.where(head0, 0.0, x)], axis=0)


def _rwkv_kernel(p_ref, mu_ref, w0_ref, w2_ref, a0_ref, a2_ref, g2_ref, kk_ref, ka_ref,
                 rk_ref, lng_ref, lnb_ref, tri_ref, o_ref, st_ref, buf_ref, *, tb):
    @pl.when(pl.program_id(1) == 0)
    def _():
        st_ref[...] = jnp.zeros_like(st_ref)
        buf_ref[0:SUBLANE, :] = jnp.zeros((SUBLANE, buf_ref.shape[1]), F32)

    p = p_ref[...]
    buf_ref[SUBLANE:SUBLANE + tb, :] = p
    prev = buf_ref[SUBLANE - 1:SUBLANE - 1 + tb, :]
    buf_ref[0:SUBLANE, :] = p[tb - SUBLANE:tb]
    p = p + (prev - p) * mu_ref[...]
    wd = RWKV_WIDTH
    r_all = p[:, 0:wd]
    k_in = p[:, wd:2 * wd]
    v_all = p[:, 2 * wd:3 * wd]
    d_in = p[:, 3 * wd:3 * wd + LANE]
    a_in = p[:, 3 * wd + LANE:3 * wd + 2 * LANE]
    g_in = p[:, 3 * wd + 2 * LANE:3 * wd + 3 * LANE]
    lw_all = -jnp.exp(-_softplus(-(w0_ref[...] + _dot(jnp.tanh(d_in), w2_ref[...]))) - 0.5)
    a_all = _sigmoid(a0_ref[...] + _dot(a_in, a2_ref[...]))
    g_all = _dot(_sigmoid(g_in), g2_ref[...])
    kkraw_all = k_in * kk_ref[...]
    k_all = k_in * (1.0 + (a_all - 1.0) * ka_ref[...])
    cs_all = _cumsum_rows(tri_ref[...], lw_all)

    row = lax.broadcasted_iota(jnp.int32, (CHUNK, LANE), 0)
    lane = lax.broadcasted_iota(jnp.int32, (CHUNK, LANE), 1)
    head0 = lane < RWKV_HD
    pos = lane % RWKV_HD
    incl, strict = row >= pos, row > pos
    eye = (row == pos).astype(F32)
    brow = lax.broadcasted_iota(jnp.int32, (LANE, LANE), 0) < RWKV_HD
    bcol = lax.broadcasted_iota(jnp.int32, (LANE, LANE), 1) < RWKV_HD
    same_head = brow == bcol
    mid = CHUNK // 2
    inv_hd = 1.0 / RWKV_HD
    stack = lambda x: _stack_heads(x.astype(BF16), head0)
    fold = lambda x: jnp.where(head0, x[0:CHUNK], x[CHUNK:LANE])

    def head_sum(x):
        s0 = jnp.sum(jnp.where(head0, x, 0.0), -1, keepdims=True)
        s1 = jnp.sum(jnp.where(head0, 0.0, x), -1, keepdims=True)
        return jnp.where(head0, s0, s1)

    npair = RWKV_WIDTH // LANE
    nchunk = tb // CHUNK
    probs = [(pair, c) for c in range(nchunk) for pair in range(npair)]
    pslice = lambda pair: slice(pair * LANE, (pair + 1) * LANE)
    cslice = lambda c: slice(c * CHUNK, (c + 1) * CHUNK)

    pre = []
    for pair, c in probs:
        rs, ps = cslice(c), pslice(pair)
        r = r_all[rs, ps]
        k = k_all[rs, ps]
        v = v_all[rs, ps]
        lw = lw_all[rs, ps]
        cs = cs_all[rs, ps]
        c_mid = cs[mid:mid + 1]
        c_last = cs[CHUNK - 1:CHUNK]
        e_out = jnp.exp(c_mid - cs)
        e_mid = jnp.exp(c_mid)
        e_last_mid = jnp.exp(c_last - c_mid)
        kk = kkraw_all[rs, ps]
        kk = kk * lax.rsqrt(head_sum(kk * kk) + L2_EPS)
        r_s = r * jnp.exp(cs - c_mid)
        kk_s = kk * jnp.exp(cs - lw - c_mid)
        al_s = kk * a_all[rs, ps] * e_out
        k_s = k * e_out
        pre.append(dict(
            lhs=jnp.concatenate([kk_s, r_s], axis=0).astype(BF16),
            rhs=jnp.concatenate([stack(al_s), stack(k_s)], axis=0),
            v_st=stack(v), v=v.astype(BF16), kk2_st=stack(kk_s * e_mid),
            al2=(al_s * e_last_mid).astype(BF16), k2=(k_s * e_last_mid).astype(BF16),
            r2=r_s * e_mid, e_last=jnp.exp(c_last), rkr=r * k * rk_ref[:, ps]))

    grams = [_dot_nt(d["lhs"], d["rhs"]) for d in pre]
    t_invs = _unit_lower_inverse([jnp.where(strict, g[0:CHUNK, 0:LANE], 0.0) for g in grams], eye, stack)
    bvs = [_dot(jnp.where(strict, g[0:CHUNK, LANE:2 * LANE], 0.0), d["v_st"]) for g, d in zip(grams, pre)]
    tkws = [_dot(t, jnp.concatenate([d["kk2_st"], stack(bv)], axis=1))
            for t, d, bv in zip(t_invs, pre, bvs)]
    zs = [_dot_tn(tkw, d["al2"]) for tkw, d in zip(tkws, pre)]
    p_mats = [jnp.where(same_head, -z[0:LANE], 0.0).astype(BF16) for z in zs]
    q_mats = [fold(_dot_tn(d["v"], d["k2"])) - fold(z[LANE:2 * LANE]) for d, z in zip(pre, zs)]
    x2s = [_dot(jnp.where(incl, g[CHUNK:2 * CHUNK, 0:LANE], 0.0),
                jnp.concatenate([stack(tkw[:, 0:LANE]), stack(tkw[:, LANE:2 * LANE])], axis=1))
           for g, tkw in zip(grams, tkws)]
    r_mats = [d["r2"] - x2[:, 0:LANE] for d, x2 in zip(pre, x2s)]
    o_intras = [_dot(jnp.where(incl, g[CHUNK:2 * CHUNK, LANE:2 * LANE], 0.0), d["v_st"]) - x2[:, LANE:2 * LANE]
                for g, d, x2 in zip(grams, pre, x2s)]

    o_wide = []
    for i, (pair, c) in enumerate(probs):
        st = st_ref[pair]
        o_wide.append(_dot_nt(r_mats[i], stack(st)) + o_intras[i])
        st_ref[pair] = st * pre[i]["e_last"] + _dot(st, p_mats[i]) + q_mats[i]

    for i, (pair, c) in enumerate(probs):
        rs, ps = cslice(c), pslice(pair)
        o = o_wide[i]
        d = o - head_sum(o) * inv_hd
        y = d * lax.rsqrt(head_sum(d * d) * inv_hd + RWKV_GN_EPS)
        bonus = head_sum(pre[i]["rkr"]) * v_all[rs, ps]
        o_ref[rs, ps] = ((y * lng_ref[:, ps] + lnb_ref[:, ps] + bonus) * g_all[rs, ps]).astype(o_ref.dtype)


def _rwkv(proj, mu, w0, w2, a0, a2, g2, k_k, k_a, r_k, ln_g, ln_b, bsz, seq, tb=256):
    nt = seq // tb
    width = proj.shape[1]
    wd = RWKV_WIDTH
    full = lambda shape: pl.BlockSpec(shape, lambda b, t: (0,) * len(shape))
    return pl.pallas_call(
        functools.partial(_rwkv_kernel, tb=tb),
        grid=(bsz, nt),
        in_specs=[pl.BlockSpec((tb, width), lambda b, t: (b * nt + t, 0)),
                  full((1, width)), full((1, wd)), full((LANE, wd)), full((1, wd)), full((LANE, wd)),
                  full((LANE, wd)), full((1, wd)), full((1, wd)),
                  full((1, wd)), full((1, wd)), full((1, wd)), full((tb, tb))],
        out_specs=pl.BlockSpec((tb, wd), lambda b, t: (b * nt + t, 0)),
        out_shape=jax.ShapeDtypeStruct((bsz * seq, wd), BF16),
        scratch_shapes=[pltpu.VMEM((wd // LANE, CHUNK, LANE), F32),
                        pltpu.VMEM((tb + SUBLANE, width), F32)],
        compiler_params=pltpu.CompilerParams(dimension_semantics=("parallel", "arbitrary"),
                                             vmem_limit_bytes=VMEM_LIMIT),
        name="rwkv",
    )(proj, mu, w0, w2, a0, a2, g2, k_k, k_a, r_k, ln_g, ln_b, _block_tril(tb))


def _merge_kernel(x_ref, gt_ref, ogla_ref, ogdn_ref, orwkv_ref, wgla_ref, wgdn_ref, wrwkv_ref, wout_ref,
                  g_ref, b_ref, o_ref):
    d = D_MODEL
    gate = lambda i: _sigmoid(gt_ref[:, i * d:(i + 1) * d].astype(F32))
    merged = (gate(0) * _dot(ogla_ref[...], wgla_ref[...])
              + gate(1) * _dot(ogdn_ref[...], wgdn_ref[...])
              + gate(2) * _dot(orwkv_ref[...], wrwkv_ref[...]))
    mix = _dot(merged, wout_ref[...])
    o_ref[...] = _layer_norm(DN_ALPHA * x_ref[...] + mix, g_ref[...], b_ref[...])


def _merge(x, gates, ogla, ogdn, orwkv, wgla, wgdn, wrwkv, wout, g, b, tm=512):
    t = x.shape[0]
    d = D_MODEL
    tok = lambda w: pl.BlockSpec((tm, w), lambda i: (i, 0))
    full = lambda shape: pl.BlockSpec(shape, lambda i: (0, 0))
    return pl.pallas_call(
        _merge_kernel,
        grid=(t // tm,),
        in_specs=[tok(d), tok(N_BRANCH * d), tok(GLA_DV), tok(GDN_WIDTH), tok(RWKV_WIDTH),
                  full((GLA_DV, d)), full((GDN_WIDTH, d)), full((RWKV_WIDTH, d)), full((d, d)),
                  full((1, d)), full((1, d))],
        out_specs=tok(d),
        out_shape=jax.ShapeDtypeStruct((t, d), F32),
        compiler_params=pltpu.CompilerParams(dimension_semantics=("parallel",),
                                             vmem_limit_bytes=VMEM_LIMIT),
        name="merge",
    )(x, gates, ogla, ogdn, orwkv, wgla, wgdn, wrwkv, wout, g, b)


def _mlp_kernel(x_ref, wu_ref, wd_ref, g_ref, b_ref, o_ref, acc_ref):
    j = pl.program_id(1)
    x = x_ref[...]
    hid = jnp.maximum(_dot(x, wu_ref[...]), 0.0)
    part = _dot(hid * hid, wd_ref[...])

    @pl.when(j == 0)
    def _():
        acc_ref[...] = part

    @pl.when(j > 0)
    def _():
        acc_ref[...] += part

    @pl.when(j == pl.num_programs(1) - 1)
    def _():
        o_ref[...] = _layer_norm(DN_ALPHA * x + acc_ref[...], g_ref[...], b_ref[...])


def _mlp(x, wu, wd, g, b, tm=1024, tf=512):
    t = x.shape[0]
    d = D_MODEL
    return pl.pallas_call(
        _mlp_kernel,
        grid=(t // tm, D_FF // tf),
        in_specs=[pl.BlockSpec((tm, d), lambda i, j: (i, 0)),
                  pl.BlockSpec((d, tf), lambda i, j: (0, j)),
                  pl.BlockSpec((tf, d), lambda i, j: (j, 0)),
                  pl.BlockSpec((1, d), lambda i, j: (0, 0)),
                  pl.BlockSpec((1, d), lambda i, j: (0, 0))],
        out_specs=pl.BlockSpec((tm, d), lambda i, j: (i, 0)),
        out_shape=jax.ShapeDtypeStruct((t, d), F32),
        scratch_shapes=[pltpu.VMEM((tm, d), F32)],
        compiler_params=pltpu.CompilerParams(dimension_semantics=("parallel", "arbitrary"),
                                             vmem_limit_bytes=VMEM_LIMIT),
        name="mlp",
    )(x, wu, wd, g, b)


def _pad_cols(w, width):
    return jnp.pad(w, ((0, 0), (0, width - w.shape[1])))


def _pad_rows(w, height):
    return jnp.pad(w, ((0, height - w.shape[0]), (0, 0)))


def _split_w_in(w_in):
    o = _IN_OFFS
    col = lambda i: w_in[:, o[i]:o[i + 1]]
    gq, gk, gv, gga, ggr, dqkv, da, db, dz, rp, gates = (col(i) for i in range(11))
    w_gla = jnp.concatenate([gq, gk, gv, ggr, _pad_cols(gga, LANE)], axis=1)
    w_gdn = jnp.concatenate([dqkv, dz, _pad_cols(jnp.concatenate([da, db], axis=1), LANE)], axis=1)
    s = 3 * RWKV_WIDTH
    w_rwkv = jnp.concatenate([rp[:, :s],
                              _pad_cols(rp[:, s:s + RWKV_DECAY_RANK], LANE),
                              _pad_cols(rp[:, s + RWKV_DECAY_RANK:s + RWKV_DECAY_RANK + RWKV_A_RANK], LANE),
                              rp[:, s + RWKV_DECAY_RANK + RWKV_A_RANK:]], axis=1)
    return tuple(w.astype(BF16) for w in (w_gla, w_gdn, w_rwkv, gates))


def _pad_mu(mu):
    s = 3 * RWKV_WIDTH
    z = jnp.zeros((LANE - RWKV_DECAY_RANK,), F32)
    return jnp.concatenate([mu[:s], mu[s:s + RWKV_DECAY_RANK], z,
                            mu[s + RWKV_DECAY_RANK:s + RWKV_DECAY_RANK + RWKV_A_RANK], z,
                            mu[s + RWKV_DECAY_RANK + RWKV_A_RANK:]])[None, :]


def kernel(x, w_in, gla_gate_up, gla_gate_bias, gla_norm_g, gdn_conv, gdn_a_log, gdn_dt_bias, gdn_norm_g, rwkv_mu, rwkv_w0, rwkv_w2, rwkv_a0, rwkv_a2, rwkv_g2, rwkv_k_k, rwkv_k_a, rwkv_r_k, rwkv_ln_g, rwkv_ln_b, w_br_gla, w_br_gdn, w_br_rwkv, w_out, ln1_g, ln1_b, w_up, w_down, ln2_g, ln2_b):
    bsz, seq, d = x.shape
    xt = x.reshape(bsz * seq, d)
    row = lambda v: v[None, :]
    for l in range(DEPTH):
        w_gla, w_gdn, w_rwkv, w_gates = _split_w_in(w_in[l])
        p_gla = _project(xt, w_gla)
        p_gdn = _project(xt, w_gdn)
        p_rwkv = _project(xt, w_rwkv)
        p_gates = _project(xt, w_gates, out_dtype=BF16)

        o_gla = _gla(p_gla, _pad_rows(gla_gate_up[l], LANE).astype(BF16), row(gla_gate_bias[l]),
                     row(gla_norm_g[l]), bsz, seq)
        o_gdn = _gdn(p_gdn, gdn_conv[l], gdn_a_log[l], gdn_dt_bias[l], row(gdn_norm_g[l]), bsz, seq)
        o_rwkv = _rwkv(
            p_rwkv, _pad_mu(rwkv_mu[l]), row(rwkv_w0[l]), _pad_rows(rwkv_w2[l], LANE).astype(BF16),
            row(rwkv_a0[l]), _pad_rows(rwkv_a2[l], LANE).astype(BF16), rwkv_g2[l].astype(BF16),
            row(rwkv_k_k[l]), row(rwkv_k_a[l]), row(rwkv_r_k[l]), row(rwkv_ln_g[l]), row(rwkv_ln_b[l]),
            bsz, seq)

        xt = _merge(xt, p_gates, o_gla, o_gdn, o_rwkv, w_br_gla[l].astype(BF16), w_br_gdn[l].astype(BF16),
                    w_br_rwkv[l].astype(BF16), w_out[l].astype(BF16), row(ln1_g[l]), row(ln1_b[l]))
        xt = _mlp(xt, w_up[l], w_down[l], row(ln2_g[l]), row(ln2_b[l]))
    return xt.reshape(bsz, seq, d)
```

```python
import functools

import jax
import jax.numpy as jnp
from jax import lax
from jax.experimental import pallas as pl
from jax.experimental.pallas import tpu as pltpu

F32 = jnp.float32
BF16 = jnp.bfloat16

D_MODEL = 1024
DEPTH = 2
CHUNK = 64
GLA_HEADS = 4
GLA_DK = 512
GLA_DV = 1024
GLA_HK = 128
GLA_HV = 256
GLA_GATE_RANK = 16
GLA_GATE_TEMP = 16.0
GDN_HEADS = 4
GDN_HD = 128
GDN_WIDTH = 512
GDN_CONV = 4
RWKV_HD = 64
RWKV_WIDTH = 512
RWKV_HEADS = 8
RWKV_DECAY_RANK = 64
RWKV_A_RANK = 64
RWKV_GATE_RANK = 128
RWKV_GN_EPS = 64e-5
N_BRANCH = 3
D_FF = 4 * D_MODEL
DN_ALPHA = (2 * DEPTH) ** 0.25
LN_EPS = 1e-5
RMS_EPS = 1e-6
L2_EPS = 1e-6

LANE = 128
SUBLANE = 8
VMEM_LIMIT = 56 * 1024 * 1024
UNIT = 2 * CHUNK

_IN_WIDTHS = (GLA_DK, GLA_DK, GLA_DV, GLA_GATE_RANK, GLA_DV,
              3 * GDN_WIDTH, GDN_HEADS, GDN_HEADS, GDN_WIDTH,
              3 * RWKV_WIDTH + RWKV_DECAY_RANK + RWKV_A_RANK + RWKV_GATE_RANK,
              N_BRANCH * D_MODEL)
_IN_OFFS = [0]
for _w in _IN_WIDTHS:
    _IN_OFFS.append(_IN_OFFS[-1] + _w)


def _dot(a, b):
    return jnp.dot(a.astype(BF16), b.astype(BF16), preferred_element_type=F32)


def _dot_nt(a, b):
    return lax.dot_general(a.astype(BF16), b.astype(BF16), (((1,), (1,)), ((), ())),
                           preferred_element_type=F32)


def _dot_tn(a, b):
    return lax.dot_general(a.astype(BF16), b.astype(BF16), (((0,), (0,)), ((), ())),
                           preferred_element_type=F32)


def _split_bf16(x):
    hi = x.astype(BF16)
    return hi, (x - hi.astype(F32)).astype(BF16)


def _cumsum_rows(tri, x):
    hi, lo = _split_bf16(x)
    return (jnp.dot(tri, hi, preferred_element_type=F32)
            + jnp.dot(tri, lo, preferred_element_type=F32))


def _cumsum_lanes(x, tri):
    hi, lo = _split_bf16(x)
    dims = (((1,), (1,)), ((), ()))
    return (lax.dot_general(hi, tri, dims, preferred_element_type=F32)
            + lax.dot_general(lo, tri, dims, preferred_element_type=F32))


def _sigmoid(x):
    return 1.0 / (1.0 + jnp.exp(-x))


def _silu(x):
    return x * _sigmoid(x)


def _softplus(x):
    return jnp.maximum(x, 0.0) + jnp.log1p(jnp.exp(-jnp.abs(x)))


def _log_sigmoid(x):
    return -_softplus(-x)


def _chunk_masks(n):
    row = lax.broadcasted_iota(jnp.int32, (n, n), 0)
    col = lax.broadcasted_iota(jnp.int32, (n, n), 1)
    same = (row // CHUNK) == (col // CHUNK)
    return same & (row >= col), same & (row > col), row == col


def _per_chunk_rows(x, offset):
    w = x.shape[1]
    return jnp.concatenate(
        [jnp.broadcast_to(x[c * CHUNK + offset:c * CHUNK + offset + 1], (CHUNK, w))
         for c in range(UNIT // CHUNK)], axis=0)


def _unit_lower_inverse(a_list, eye, as_rhs=lambda p: p):
    n = eye.shape[0]
    ps = [-a for a in a_list]
    ts = [eye + p for p in ps]
    ps = [_dot(p, as_rhs(p)) for p in ps]
    for _ in range(CHUNK.bit_length() - 3):
        prods = [_dot(jnp.concatenate([t, p], axis=0), as_rhs(p)) for t, p in zip(ts, ps)]
        ts = [t + pr[0:n] for t, pr in zip(ts, prods)]
        ps = [pr[n:2 * n] for pr in prods]
    return [t + _dot(t, as_rhs(p)) for t, p in zip(ts, ps)]


def _layer_norm(y, g, b):
    mu = jnp.mean(y, -1, keepdims=True)
    d = y - mu
    var = jnp.mean(d * d, -1, keepdims=True)
    return d * lax.rsqrt(var + LN_EPS) * g + b


def _block_tril(n):
    idx = jnp.arange(n)
    same = (idx[:, None] // CHUNK) == (idx[None, :] // CHUNK)
    return (same & (idx[:, None] >= idx[None, :])).astype(BF16)


def _proj_kernel(x_ref, w_ref, o_ref):
    o_ref[...] = jnp.dot(x_ref[...].astype(BF16), w_ref[...],
                         preferred_element_type=F32).astype(o_ref.dtype)


def _project(x, w, out_dtype=F32, tm=1024):
    m, k = x.shape
    n = w.shape[1]
    return pl.pallas_call(
        _proj_kernel,
        grid=(m // tm,),
        in_specs=[pl.BlockSpec((tm, k), lambda i: (i, 0)),
                  pl.BlockSpec((k, n), lambda i: (0, 0), pipeline_mode=pl.Buffered(1))],
        out_specs=pl.BlockSpec((tm, n), lambda i: (i, 0)),
        out_shape=jax.ShapeDtypeStruct((m, n), out_dtype),
        compiler_params=pltpu.CompilerParams(dimension_semantics=("parallel",),
                                             vmem_limit_bytes=VMEM_LIMIT),
        name="in_proj",
    )(x, w)


class _NextBlockProjection:
    def __init__(self, xn_ref, w_ref, pbuf_ref, chunk=2 * LANE):
        self.xb = xn_ref[...].astype(BF16)
        self.w_ref = w_ref
        self.pbuf_ref = pbuf_ref
        width = w_ref.shape[1]
        self.bounds = [(lo, min(lo + chunk, width)) for lo in range(0, width, chunk)]

    def emit(self, count=1):
        for _ in range(min(count, len(self.bounds))):
            lo, hi = self.bounds.pop(0)
            self.pbuf_ref[:, lo:hi] = jnp.dot(self.xb, self.w_ref[:, lo:hi], preferred_element_type=F32)

    def flush(self):
        self.emit(len(self.bounds))


def _first_block_projection(x0_ref, w_ref, pbuf_ref):
    pbuf_ref[...] = jnp.dot(x0_ref[...].astype(BF16), w_ref[...], preferred_element_type=F32)


def _mixer_call(kernel_fn, x, w, params, out_width, scratch, bsz, seq, tb, name):
    nt = seq // tb
    d = x.shape[1]
    width = w.shape[1]
    full = lambda a: pl.BlockSpec(a.shape, lambda b, t: (0,) * a.ndim)
    return pl.pallas_call(
        functools.partial(kernel_fn, tb=tb),
        grid=(bsz, nt),
        in_specs=[pl.BlockSpec((tb, d), lambda b, t: (b * nt, 0)),
                  pl.BlockSpec((tb, d), lambda b, t: (b * nt + jnp.minimum(t + 1, nt - 1), 0)),
                  pl.BlockSpec((d, width), lambda b, t: (0, 0), pipeline_mode=pl.Buffered(1))]
                 + [full(a) for a in params],
        out_specs=pl.BlockSpec((tb, out_width), lambda b, t: (b * nt + t, 0)),
        out_shape=jax.ShapeDtypeStruct((bsz * seq, out_width), BF16),
        scratch_shapes=[pltpu.VMEM((tb, width), F32)] + scratch,
        compiler_params=pltpu.CompilerParams(dimension_semantics=("parallel", "arbitrary"),
                                             vmem_limit_bytes=VMEM_LIMIT),
        name=name,
    )(x, x, w, *params)


def _gla_kernel(x0_ref, xn_ref, w_ref, gup_ref, gb_ref, ng_ref, tri_ref, o_ref, pbuf_ref, p_ref, st_ref, *, tb):
    @pl.when(pl.program_id(1) == 0)
    def _():
        st_ref[...] = jnp.zeros_like(st_ref)
        _first_block_projection(x0_ref, w_ref, pbuf_ref)

    p_ref[...] = pbuf_ref[...]
    nxt = _NextBlockProjection(xn_ref, w_ref, pbuf_ref)
    nxt.emit(3)
    dk, dv, hk, hv = GLA_DK, GLA_DV, GLA_HK, GLA_HV
    incl, _, _ = _chunk_masks(UNIT)
    gd = p_ref[:, 2 * dk + 2 * dv:2 * dk + 2 * dv + LANE]
    log_a = _log_sigmoid(_dot(gd, gup_ref[...]) + gb_ref[...]) * (1.0 / GLA_GATE_TEMP)
    b_all = _cumsum_rows(tri_ref[...], log_a)
    mid = CHUNK // 2
    nchunk = UNIT // CHUNK
    probs = [(h, u) for u in range(tb // UNIT) for h in range(GLA_HEADS)]

    pre = []
    for h, u in probs:
        nxt.emit(1)
        rows = slice(u * UNIT, (u + 1) * UNIT)
        b = b_all[rows, h * hk:(h + 1) * hk]
        b_mid = _per_chunk_rows(b, mid)
        b_last = _per_chunk_rows(b, CHUNK - 1)
        qs = p_ref[rows, h * hk:(h + 1) * hk] * (hk ** -0.5) * jnp.exp(b - b_mid)
        ks = p_ref[rows, dk + h * hk:dk + (h + 1) * hk] * jnp.exp(b_mid - b)
        pre.append(dict(qs=qs, ks=ks, qe=qs * jnp.exp(b_mid), kd=ks * jnp.exp(b_last - b_mid),
                        e_last=jnp.exp(b_last),
                        v=p_ref[rows, 2 * dk + h * hv:2 * dk + (h + 1) * hv],
                        og=p_ref[rows, 2 * dk + dv + h * hv:2 * dk + dv + (h + 1) * hv]))

    attns = [jnp.where(incl, _dot_nt(d["qs"], d["ks"]), 0.0) for d in pre]
    intras = [_dot(a, d["v"]) for a, d in zip(attns, pre)]
    kvs = [[_dot_tn(d["v"][c * CHUNK:(c + 1) * CHUNK], d["kd"][c * CHUNK:(c + 1) * CHUNK])
            for c in range(nchunk)] for d in pre]

    states = []
    for i, (h, u) in enumerate(probs):
        st = st_ref[h]
        per_chunk = []
        for c in range(nchunk):
            per_chunk.append(st)
            st = st * pre[i]["e_last"][c * CHUNK:c * CHUNK + 1] + kvs[i][c]
        st_ref[h] = st
        states.append(per_chunk)

    for i, (h, u) in enumerate(probs):
        for c in range(nchunk):
            nxt.emit(1)
            rc = slice(c * CHUNK, (c + 1) * CHUNK)
            o = _dot_nt(pre[i]["qe"][rc], states[i][c]) + intras[i][rc]
            o = o * lax.rsqrt(jnp.mean(o * o, -1, keepdims=True) + RMS_EPS) * ng_ref[...]
            out_rows = slice(u * UNIT + c * CHUNK, u * UNIT + (c + 1) * CHUNK)
            o_ref[out_rows, h * hv:(h + 1) * hv] = (o * _silu(pre[i]["og"][rc])).astype(o_ref.dtype)
    nxt.flush()


def _gla(x, w, gup, gb, ng, bsz, seq, tb=256):
    return _mixer_call(_gla_kernel, x, w, [gup, gb, ng, _block_tril(tb)], GLA_DV,
                       [pltpu.VMEM((tb, w.shape[1]), F32), pltpu.VMEM((GLA_HEADS, GLA_HV, GLA_HK), F32)],
                       bsz, seq, tb, "gla")


def _gdn_kernel(x0_ref, xn_ref, w_ref, cw_ref, alog_row_ref, dtb_row_ref, alog_col_ref, dtb_col_ref, ng_ref,
                tri_ref, o_ref, pbuf_ref, st_ref, buf_ref, *, tb):
    @pl.when(pl.program_id(1) == 0)
    def _():
        st_ref[...] = jnp.zeros_like(st_ref)
        buf_ref[0:SUBLANE, :] = jnp.zeros((SUBLANE, buf_ref.shape[1]), F32)
        _first_block_projection(x0_ref, w_ref, pbuf_ref)

    wq = 3 * GDN_WIDTH
    hd = GDN_HD
    x = pbuf_ref[:, 0:wq]
    z_all = pbuf_ref[:, wq:wq + GDN_WIDTH]
    ab = pbuf_ref[:, wq + GDN_WIDTH:wq + GDN_WIDTH + LANE]
    nxt = _NextBlockProjection(xn_ref, w_ref, pbuf_ref)

    buf_ref[SUBLANE:SUBLANE + tb, :] = x
    groups = []
    for lo in range(0, wq, 2 * LANE):
        nxt.emit(1)
        cols = slice(lo, lo + 2 * LANE)
        y = x[:, cols] * cw_ref[GDN_CONV - 1:GDN_CONV, cols]
        for j in range(1, GDN_CONV):
            y = y + buf_ref[SUBLANE - j:SUBLANE - j + tb, cols] * cw_ref[GDN_CONV - 1 - j:GDN_CONV - j, cols]
        groups.append(_silu(y))
    buf_ref[0:SUBLANE, :] = x[tb - SUBLANE:tb]
    qkv = jnp.concatenate(groups, axis=1)

    g_cols = -jnp.exp(alog_row_ref[...]) * _softplus(ab + dtb_row_ref[...])
    g_rows = -jnp.exp(alog_col_ref[...]) * _softplus(ab.T[0:SUBLANE] + dtb_col_ref[...])
    gam_cols = _cumsum_rows(tri_ref[...], g_cols)
    gam_rows = _cumsum_lanes(g_rows, tri_ref[...])
    beta_cols = _sigmoid(ab)

    incl, strict, diag = _chunk_masks(UNIT)
    eye = diag.astype(F32)
    nchunk = UNIT // CHUNK
    probs = [(h, u) for u in range(tb // UNIT) for h in range(GDN_HEADS)]

    pre = []
    for h, u in probs:
        nxt.emit(1)
        rows = slice(u * UNIT, (u + 1) * UNIT)
        q = qkv[rows, h * hd:(h + 1) * hd]
        k = qkv[rows, GDN_WIDTH + h * hd:GDN_WIDTH + (h + 1) * hd]
        v = qkv[rows, 2 * GDN_WIDTH + h * hd:2 * GDN_WIDTH + (h + 1) * hd]
        q = q * lax.rsqrt(jnp.sum(q * q, -1, keepdims=True) + L2_EPS) * (hd ** -0.5)
        k = k * lax.rsqrt(jnp.sum(k * k, -1, keepdims=True) + L2_EPS)
        gam = gam_cols[rows, h:h + 1]
        gam_r = gam_rows[h:h + 1, rows]
        beta = beta_cols[rows, GDN_HEADS + h:GDN_HEADS + h + 1]
        g_last = _per_chunk_rows(gam, CHUNK - 1)
        e_gam = jnp.exp(gam)
        kb = k * beta
        pre.append(dict(
            q=q, k=k, kb=kb, qe=q * e_gam,
            decay=jnp.where(incl, jnp.exp(jnp.minimum(gam - gam_r, 0.0)), 0.0),
            rhs=jnp.concatenate([v * beta, kb * e_gam], axis=1),
            kd=k * jnp.exp(g_last - gam), e_last=jnp.exp(g_last)))

    kqs = [_dot_nt(jnp.concatenate([d["kb"], d["q"]], axis=0), d["k"]) for d in pre]
    t_invs = _unit_lower_inverse(
        [jnp.where(strict, kq[0:UNIT] * d["decay"], 0.0) for kq, d in zip(kqs, pre)], eye)
    uws = [_dot(t, d["rhs"]) for t, d in zip(t_invs, pre)]
    ros = [_dot(kq[UNIT:2 * UNIT] * d["decay"], uw) for kq, d, uw in zip(kqs, pre, uws)]
    r_mats = [d["qe"] - ro[:, hd:2 * hd] for d, ro in zip(pre, ros)]
    qps = [[_dot_tn(d["kd"][c * CHUNK:(c + 1) * CHUNK],
                    jnp.concatenate([uw[c * CHUNK:(c + 1) * CHUNK, 0:hd],
                                     -uw[c * CHUNK:(c + 1) * CHUNK, hd:2 * hd]], axis=1))
            for c in range(nchunk)] for d, uw in zip(pre, uws)]

    outs = {}
    for u in range(tb // UNIT):
        for c in range(nchunk):
            rc = slice(c * CHUNK, (c + 1) * CHUNK)
            for h in range(GDN_HEADS):
                i = u * GDN_HEADS + h
                st = st_ref[h]
                outs[(i, c)] = _dot(r_mats[i][rc], st) + ros[i][rc, 0:hd]
                qp = qps[i][c]
                st_ref[h] = st * pre[i]["e_last"][c * CHUNK:c * CHUNK + 1] + _dot(qp[:, hd:2 * hd], st) + qp[:, 0:hd]

    for i, (h, u) in enumerate(probs):
        for c in range(nchunk):
            o = outs[(i, c)]
            o = o * lax.rsqrt(jnp.mean(o * o, -1, keepdims=True) + RMS_EPS) * ng_ref[...]
            out_rows = slice(u * UNIT + c * CHUNK, u * UNIT + (c + 1) * CHUNK)
            z = z_all[out_rows, h * hd:(h + 1) * hd]
            o_ref[out_rows, h * hd:(h + 1) * hd] = (o * _silu(z)).astype(o_ref.dtype)
    nxt.flush()


def _gdn(x, w, cw, alog, dtb, ng, bsz, seq, tb=256):
    lane_row = lambda v: jnp.pad(v, (0, LANE - v.shape[0]))[None, :]
    sub_col = lambda v: jnp.pad(v, (0, SUBLANE - v.shape[0]))[:, None]
    params = [cw, lane_row(alog), lane_row(dtb), sub_col(alog), sub_col(dtb), ng, _block_tril(tb)]
    scratch = [pltpu.VMEM((GDN_HEADS, GDN_HD, GDN_HD), F32),
               pltpu.VMEM((tb + SUBLANE, 3 * GDN_WIDTH), F32)]
    return _mixer_call(_gdn_kernel, x, w, params, GDN_WIDTH, scratch, bsz, seq, tb, "gdn")


def _stack_heads(x, head0):
    return jnp.concatenate([jnp.where(head0, x, 0.0), jnp.where(head0, 0.0, x)], axis=0)


def _rwkv_kernel(x0_ref, xn_ref, w_ref, mu_ref, w0_ref, w2_ref, a0_ref, a2_ref, g2_ref, kk_ref, ka_ref,
                 rk_ref, lng_ref, lnb_ref, tri_ref, o_ref, pbuf_ref, st_ref, buf_ref, *, tb):
    @pl.when(pl.program_id(1) == 0)
    def _():
        st_ref[...] = jnp.zeros_like(st_ref)
        buf_ref[0:SUBLANE, :] = jnp.zeros((SUBLANE, buf_ref.shape[1]), F32)
        _first_block_projection(x0_ref, w_ref, pbuf_ref)

    buf_ref[SUBLANE:SUBLANE + tb, :] = pbuf_ref[...]
    nxt = _NextBlockProjection(xn_ref, w_ref, pbuf_ref)
    nxt.emit(2)
    p = buf_ref[SUBLANE:SUBLANE + tb, :]
    prev = buf_ref[SUBLANE - 1:SUBLANE - 1 + tb, :]
    buf_ref[0:SUBLANE, :] = p[tb - SUBLANE:tb]
    p = p + (prev - p) * mu_ref[...]
    wd = RWKV_WIDTH
    r_all = p[:, 0:wd]
    k_in = p[:, wd:2 * wd]
    v_all = p[:, 2 * wd:3 * wd]
    d_in = p[:, 3 * wd:3 * wd + LANE]
    a_in = p[:, 3 * wd + LANE:3 * wd + 2 * LANE]
    g_in = p[:, 3 * wd + 2 * LANE:3 * wd + 3 * LANE]
    lw_all = -jnp.exp(-_softplus(-(w0_ref[...] + _dot(jnp.tanh(d_in), w2_ref[...]))) - 0.5)
    a_all = _sigmoid(a0_ref[...] + _dot(a_in, a2_ref[...]))
    g_all = _dot(_sigmoid(g_in), g2_ref[...])
    kkraw_all = k_in * kk_ref[...]
    k_all = k_in * (1.0 + (a_all - 1.0) * ka_ref[...])
    cs_all = _cumsum_rows(tri_ref[...], lw_all)

    row = lax.broadcasted_iota(jnp.int32, (CHUNK, LANE), 0)
    lane = lax.broadcasted_iota(jnp.int32, (CHUNK, LANE), 1)
    head0 = lane < RWKV_HD
    pos = lane % RWKV_HD
    incl, strict = row >= pos, row > pos
    eye = (row == pos).astype(F32)
    brow = lax.broadcasted_iota(jnp.int32, (LANE, LANE), 0) < RWKV_HD
    bcol = lax.broadcasted_iota(jnp.int32, (LANE, LANE), 1) < RWKV_HD
    same_head = brow == bcol
    mid = CHUNK // 2
    inv_hd = 1.0 / RWKV_HD
    stack = lambda x: _stack_heads(x.astype(BF16), head0)
    fold = lambda x: jnp.where(head0, x[0:CHUNK], x[CHUNK:LANE])

    def head_sum(x):
        s0 = jnp.sum(jnp.where(head0, x, 0.0), -1, keepdims=True)
        s1 = jnp.sum(jnp.where(head0, 0.0, x), -1, keepdims=True)
        return jnp.where(head0, s0, s1)

    npair = RWKV_WIDTH // LANE
    nchunk = tb // CHUNK
    probs = [(pair, c) for c in range(nchunk) for pair in range(npair)]
    pslice = lambda pair: slice(pair * LANE, (pair + 1) * LANE)
    cslice = lambda c: slice(c * CHUNK, (c + 1) * CHUNK)

    pre = []
    for n, (pair, c) in enumerate(probs):
        nxt.emit(n % 2)
        rs, ps = cslice(c), pslice(pair)
        r = r_all[rs, ps]
        k = k_all[rs, ps]
        v = v_all[rs, ps]
        lw = lw_all[rs, ps]
        cs = cs_all[rs, ps]
        c_mid = cs[mid:mid + 1]
        c_last = cs[CHUNK - 1:CHUNK]
        e_out = jnp.exp(c_mid - cs)
        e_mid = jnp.exp(c_mid)
        e_last_mid = jnp.exp(c_last - c_mid)
        kk = kkraw_all[rs, ps]
        kk = kk * lax.rsqrt(head_sum(kk * kk) + L2_EPS)
        r_s = r * jnp.exp(cs - c_mid)
        kk_s = kk * jnp.exp(cs - lw - c_mid)
        al_s = kk * a_all[rs, ps] * e_out
        k_s = k * e_out
        pre.append(dict(
            lhs=jnp.concatenate([kk_s, r_s], axis=0).astype(BF16),
            rhs=jnp.concatenate([stack(al_s), stack(k_s)], axis=0),
            v_st=stack(v), v=v.astype(BF16), kk2_st=stack(kk_s * e_mid),
            al2=(al_s * e_last_mid).astype(BF16), k2=(k_s * e_last_mid).astype(BF16),
            r2=r_s * e_mid, e_last=jnp.exp(c_last), rkr=r * k * rk_ref[:, ps]))

    grams = [_dot_nt(d["lhs"], d["rhs"]) for d in pre]
    t_invs = _unit_lower_inverse([jnp.where(strict, g[0:CHUNK, 0:LANE], 0.0) for g in grams], eye, stack)
    bvs = [_dot(jnp.where(strict, g[0:CHUNK, LANE:2 * LANE], 0.0), d["v_st"]) for g, d in zip(grams, pre)]
    tkws = [_dot(t, jnp.concatenate([d["kk2_st"], stack(bv)], axis=1))
            for t, d, bv in zip(t_invs, pre, bvs)]
    zs = [_dot_tn(tkw, d["al2"]) for tkw, d in zip(tkws, pre)]
    p_mats = [jnp.where(same_head, -z[0:LANE], 0.0).astype(BF16) for z in zs]
    q_mats = [fold(_dot_tn(d["v"], d["k2"])) - fold(z[LANE:2 * LANE]) for d, z in zip(pre, zs)]
    x2s = [_dot(jnp.where(incl, g[CHUNK:2 * CHUNK, 0:LANE], 0.0),
                jnp.concatenate([stack(tkw[:, 0:LANE]), stack(tkw[:, LANE:2 * LANE])], axis=1))
           for g, tkw in zip(grams, tkws)]
    r_mats = [d["r2"] - x2[:, 0:LANE] for d, x2 in zip(pre, x2s)]
    o_intras = [_dot(jnp.where(incl, g[CHUNK:2 * CHUNK, LANE:2 * LANE], 0.0), d["v_st"]) - x2[:, LANE:2 * LANE]
                for g, d, x2 in zip(grams, pre, x2s)]

    o_wide = []
    for i, (pair, c) in enumerate(probs):
        st = st_ref[pair]
        o_wide.append(_dot_nt(r_mats[i], stack(st)) + o_intras[i])
        st_ref[pair] = st * pre[i]["e_last"] + _dot(st, p_mats[i]) + q_mats[i]

    for i, (pair, c) in enumerate(probs):
        rs, ps = cslice(c), pslice(pair)
        o = o_wide[i]
        d = o - head_sum(o) * inv_hd
        y = d * lax.rsqrt(head_sum(d * d) * inv_hd + RWKV_GN_EPS)
        bonus = head_sum(pre[i]["rkr"]) * v_all[rs, ps]
        o_ref[rs, ps] = ((y * lng_ref[:, ps] + lnb_ref[:, ps] + bonus) * g_all[rs, ps]).astype(o_ref.dtype)
    nxt.flush()


def _rwkv(x, w, mu, w0, w2, a0, a2, g2, k_k, k_a, r_k, ln_g, ln_b, bsz, seq, tb=256):
    params = [mu, w0, w2, a0, a2, g2, k_k, k_a, r_k, ln_g, ln_b, _block_tril(tb)]
    scratch = [pltpu.VMEM((RWKV_WIDTH // LANE, CHUNK, LANE), F32),
               pltpu.VMEM((tb + SUBLANE, w.shape[1]), F32)]
    return _mixer_call(_rwkv_kernel, x, w, params, RWKV_WIDTH, scratch, bsz, seq, tb, "rwkv")


def _merge_kernel(x_ref, gt_ref, ogla_ref, ogdn_ref, orwkv_ref, wgla_ref, wgdn_ref, wrwkv_ref, wout_ref,
                  g_ref, b_ref, o_ref):
    d = D_MODEL
    gate = lambda i: _sigmoid(gt_ref[:, i * d:(i + 1) * d].astype(F32))
    merged = (gate(0) * _dot(ogla_ref[...], wgla_ref[...])
              + gate(1) * _dot(ogdn_ref[...], wgdn_ref[...])
              + gate(2) * _dot(orwkv_ref[...], wrwkv_ref[...]))
    mix = _dot(merged, wout_ref[...])
    o_ref[...] = _layer_norm(DN_ALPHA * x_ref[...] + mix, g_ref[...], b_ref[...])


def _merge(x, gates, ogla, ogdn, orwkv, wgla, wgdn, wrwkv, wout, g, b, tm=512):
    t = x.shape[0]
    d = D_MODEL
    tok = lambda w: pl.BlockSpec((tm, w), lambda i: (i, 0))
    full = lambda shape: pl.BlockSpec(shape, lambda i: (0, 0))
    return pl.pallas_call(
        _merge_kernel,
        grid=(t // tm,),
        in_specs=[tok(d), tok(N_BRANCH * d), tok(GLA_DV), tok(GDN_WIDTH), tok(RWKV_WIDTH),
                  full((GLA_DV, d)), full((GDN_WIDTH, d)), full((RWKV_WIDTH, d)), full((d, d)),
                  full((1, d)), full((1, d))],
        out_specs=tok(d),
        out_shape=jax.ShapeDtypeStruct((t, d), F32),
        compiler_params=pltpu.CompilerParams(dimension_semantics=("parallel",),
                                             vmem_limit_bytes=VMEM_LIMIT),
        name="merge",
    )(x, gates, ogla, ogdn, orwkv, wgla, wgdn, wrwkv, wout, g, b)


def _mlp_kernel(x_ref, wu_ref, wd_ref, g_ref, b_ref, o_ref, *, tf):
    x = x_ref[...]
    xb = x.astype(BF16)
    acc = None
    for j in range(D_FF // tf):
        hid = jnp.maximum(jnp.dot(xb, wu_ref[:, j * tf:(j + 1) * tf], preferred_element_type=F32), 0.0)
        part = jnp.dot((hid * hid).astype(BF16), wd_ref[j * tf:(j + 1) * tf, :], preferred_element_type=F32)
        acc = part if acc is None else acc + part
    o_ref[...] = _layer_norm(DN_ALPHA * x + acc, g_ref[...], b_ref[...])


def _mlp(x, wu, wd, g, b, layer, tm=512, tf=1024):
    t = x.shape[0]
    d = D_MODEL
    resident = lambda shape: pl.BlockSpec((None,) + shape, lambda i: (layer, 0, 0),
                                          pipeline_mode=pl.Buffered(1))
    return pl.pallas_call(
        functools.partial(_mlp_kernel, tf=tf),
        grid=(t // tm,),
        in_specs=[pl.BlockSpec((tm, d), lambda i: (i, 0)),
                  resident((d, D_FF)), resident((D_FF, d)),
                  pl.BlockSpec((1, d), lambda i: (0, 0)),
                  pl.BlockSpec((1, d), lambda i: (0, 0))],
        out_specs=pl.BlockSpec((tm, d), lambda i: (i, 0)),
        out_shape=jax.ShapeDtypeStruct((t, d), F32),
        compiler_params=pltpu.CompilerParams(dimension_semantics=("parallel",),
                                             vmem_limit_bytes=VMEM_LIMIT),
        name="mlp",
    )(x, wu, wd, g, b)


def _pad_cols(w, width):
    return jnp.pad(w, ((0, 0), (0, width - w.shape[1])))


def _pad_rows(w, height):
    return jnp.pad(w, ((0, height - w.shape[0]), (0, 0)))


def _split_w_in(w_in):
    o = _IN_OFFS
    col = lambda i: w_in[:, o[i]:o[i + 1]]
    gq, gk, gv, gga, ggr, dqkv, da, db, dz, rp, gates = (col(i) for i in range(11))
    w_gla = jnp.concatenate([gq, gk, gv, ggr, _pad_cols(gga, LANE)], axis=1)
    w_gdn = jnp.concatenate([dqkv, dz, _pad_cols(jnp.concatenate([da, db], axis=1), LANE)], axis=1)
    s = 3 * RWKV_WIDTH
    w_rwkv = jnp.concatenate([rp[:, :s],
                              _pad_cols(rp[:, s:s + RWKV_DECAY_RANK], LANE),
                              _pad_cols(rp[:, s + RWKV_DECAY_RANK:s + RWKV_DECAY_RANK + RWKV_A_RANK], LANE),
                              rp[:, s + RWKV_DECAY_RANK + RWKV_A_RANK:]], axis=1)
    return tuple(w.astype(BF16) for w in (w_gla, w_gdn, w_rwkv, gates))


def _pad_mu(mu):
    s = 3 * RWKV_WIDTH
    z = jnp.zeros((LANE - RWKV_DECAY_RANK,), F32)
    return jnp.concatenate([mu[:s], mu[s:s + RWKV_DECAY_RANK], z,
                            mu[s + RWKV_DECAY_RANK:s + RWKV_DECAY_RANK + RWKV_A_RANK], z,
                            mu[s + RWKV_DECAY_RANK + RWKV_A_RANK:]])[None, :]


def kernel(x, w_in, gla_gate_up, gla_gate_bias, gla_norm_g, gdn_conv, gdn_a_log, gdn_dt_bias, gdn_norm_g, rwkv_mu, rwkv_w0, rwkv_w2, rwkv_a0, rwkv_a2, rwkv_g2, rwkv_k_k, rwkv_k_a, rwkv_r_k, rwkv_ln_g, rwkv_ln_b, w_br_gla, w_br_gdn, w_br_rwkv, w_out, ln1_g, ln1_b, w_up, w_down, ln2_g, ln2_b):
    bsz, seq, d = x.shape
    xt = x.reshape(bsz * seq, d)
    row = lambda v: v[None, :]
    w_up_b = w_up.astype(BF16)
    w_down_b = w_down.astype(BF16)
    for l in range(DEPTH):
        w_gla, w_gdn, w_rwkv, w_gates = _split_w_in(w_in[l])
        p_gates = _project(xt, w_gates, out_dtype=BF16)

        o_gla = _gla(xt, w_gla, _pad_rows(gla_gate_up[l], LANE).astype(BF16), row(gla_gate_bias[l]),
                     row(gla_norm_g[l]), bsz, seq)
        o_gdn = _gdn(xt, w_gdn, gdn_conv[l], gdn_a_log[l], gdn_dt_bias[l], row(gdn_norm_g[l]), bsz, seq)
        o_rwkv = _rwkv(
            xt, w_rwkv, _pad_mu(rwkv_mu[l]), row(rwkv_w0[l]), _pad_rows(rwkv_w2[l], LANE).astype(BF16),
            row(rwkv_a0[l]), _pad_rows(rwkv_a2[l], LANE).astype(BF16), rwkv_g2[l].astype(BF16),
            row(rwkv_k_k[l]), row(rwkv_k_a[l]), row(rwkv_r_k[l]), row(rwkv_ln_g[l]), row(rwkv_ln_b[l]),
            bsz, seq)

        xt = _merge(xt, p_gates, o_gla, o_gdn, o_rwkv, w_br_gla[l].astype(BF16), w_br_gdn[l].astype(BF16),
                    w_br_rwkv[l].astype(BF16), w_out[l].astype(BF16), row(ln1_g[l]), row(ln1_b[l]))
        xt = _mlp(xt, w_up_b, w_down_b, row(ln2_g[l]), row(ln2_b[l]), l)
    return xt.reshape(bsz, seq, d)
```

```python
import functools

import jax
import jax.numpy as jnp
from jax import lax
from jax.experimental import pallas as pl
from jax.experimental.pallas import tpu as pltpu

F32 = jnp.float32
BF16 = jnp.bfloat16

D_MODEL = 1024
DEPTH = 2
CHUNK = 64
GLA_HEADS = 4
GLA_DK = 512
GLA_DV = 1024
GLA_HK = 128
GLA_HV = 256
GLA_GATE_RANK = 16
GLA_GATE_TEMP = 16.0
GDN_HEADS = 4
GDN_HD = 128
GDN_WIDTH = 512
GDN_CONV = 4
RWKV_HD = 64
RWKV_WIDTH = 512
RWKV_HEADS = 8
RWKV_DECAY_RANK = 64
RWKV_A_RANK = 64
RWKV_GATE_RANK = 128
RWKV_GN_EPS = 64e-5
N_BRANCH = 3
D_FF = 4 * D_MODEL
DN_ALPHA = (2 * DEPTH) ** 0.25
LN_EPS = 1e-5
RMS_EPS = 1e-6
L2_EPS = 1e-6

LANE = 128
SUBLANE = 8
VMEM_LIMIT = 56 * 1024 * 1024
UNIT = 2 * CHUNK

_IN_WIDTHS = (GLA_DK, GLA_DK, GLA_DV, GLA_GATE_RANK, GLA_DV,
              3 * GDN_WIDTH, GDN_HEADS, GDN_HEADS, GDN_WIDTH,
              3 * RWKV_WIDTH + RWKV_DECAY_RANK + RWKV_A_RANK + RWKV_GATE_RANK,
              N_BRANCH * D_MODEL)
_IN_OFFS = [0]
for _w in _IN_WIDTHS:
    _IN_OFFS.append(_IN_OFFS[-1] + _w)


def _dot(a, b):
    return jnp.dot(a.astype(BF16), b.astype(BF16), preferred_element_type=F32)


def _dot_nt(a, b):
    return lax.dot_general(a.astype(BF16), b.astype(BF16), (((1,), (1,)), ((), ())),
                           preferred_element_type=F32)


def _dot_tn(a, b):
    return lax.dot_general(a.astype(BF16), b.astype(BF16), (((0,), (0,)), ((), ())),
                           preferred_element_type=F32)


def _split_bf16(x):
    hi = x.astype(BF16)
    return hi, (x - hi.astype(F32)).astype(BF16)


def _cumsum_rows(tri, x):
    hi, lo = _split_bf16(x)
    return (jnp.dot(tri, hi, preferred_element_type=F32)
            + jnp.dot(tri, lo, preferred_element_type=F32))


def _cumsum_lanes(x, tri):
    hi, lo = _split_bf16(x)
    dims = (((1,), (1,)), ((), ()))
    return (lax.dot_general(hi, tri, dims, preferred_element_type=F32)
            + lax.dot_general(lo, tri, dims, preferred_element_type=F32))


def _sigmoid(x):
    return 1.0 / (1.0 + jnp.exp(-x))


def _silu(x):
    return x * _sigmoid(x)


def _softplus(x):
    return jnp.maximum(x, 0.0) + jnp.log1p(jnp.exp(-jnp.abs(x)))


def _log_sigmoid(x):
    return -_softplus(-x)


def _chunk_masks(n):
    row = lax.broadcasted_iota(jnp.int32, (n, n), 0)
    col = lax.broadcasted_iota(jnp.int32, (n, n), 1)
    same = (row // CHUNK) == (col // CHUNK)
    return same & (row >= col), same & (row > col), row == col


def _per_chunk_rows(x, offset):
    w = x.shape[1]
    return jnp.concatenate(
        [jnp.broadcast_to(x[c * CHUNK + offset:c * CHUNK + offset + 1], (CHUNK, w))
         for c in range(UNIT // CHUNK)], axis=0)


def _unit_lower_inverse(out, a_list, eye, as_rhs=lambda p: p):
    n = eye.shape[0]
    ps = [-a for a in a_list]
    ts = [eye + p for p in ps]
    squares = []
    yield from _staged(squares, lambda p: _dot(p, as_rhs(p)), [ps], 1.0)
    ps = squares
    for _ in range(CHUNK.bit_length() - 3):
        prods = []
        yield from _staged(prods, lambda t, p: _dot(jnp.concatenate([t, p], axis=0), as_rhs(p)), [ts, ps], 2.0)
        ts = [t + pr[0:n] for t, pr in zip(ts, prods)]
        ps = [pr[n:2 * n] for pr in prods]
    yield from _staged(out, lambda t, p: t + _dot(t, as_rhs(p)), [ts, ps], 1.0)


def _layer_norm(y, g, b):
    mu = jnp.mean(y, -1, keepdims=True)
    d = y - mu
    var = jnp.mean(d * d, -1, keepdims=True)
    return d * lax.rsqrt(var + LN_EPS) * g + b


def _block_tril(n):
    idx = jnp.arange(n)
    same = (idx[:, None] // CHUNK) == (idx[None, :] // CHUNK)
    return (same & (idx[:, None] >= idx[None, :])).astype(BF16)


def _proj_kernel(x_ref, w_ref, o_ref):
    o_ref[...] = jnp.dot(x_ref[...].astype(BF16), w_ref[...],
                         preferred_element_type=F32).astype(o_ref.dtype)


def _project(x, w, out_dtype=F32, tm=1024):
    m, k = x.shape
    n = w.shape[1]
    return pl.pallas_call(
        _proj_kernel,
        grid=(m // tm,),
        in_specs=[pl.BlockSpec((tm, k), lambda i: (i, 0)),
                  pl.BlockSpec((k, n), lambda i: (0, 0), pipeline_mode=pl.Buffered(1))],
        out_specs=pl.BlockSpec((tm, n), lambda i: (i, 0)),
        out_shape=jax.ShapeDtypeStruct((m, n), out_dtype),
        compiler_params=pltpu.CompilerParams(dimension_semantics=("parallel",),
                                             vmem_limit_bytes=VMEM_LIMIT),
        name="in_proj",
    )(x, w)


_PHASE_END = "phase-end"
_ELEMENTWISE_UNIT = 10.0


def _run_staggered(streams):
    spent = [0.0] * len(streams)
    finished = [False] * len(streams)
    slot = 0
    while not all(finished):
        running = [j for j in range(len(streams)) if j <= slot and not finished[j]]
        while running:
            j = min(running, key=lambda i: spent[i])
            try:
                cost = next(streams[j])
            except StopIteration:
                finished[j] = True
                running.remove(j)
                continue
            if cost == _PHASE_END:
                running.remove(j)
            else:
                spent[j] += cost
        top = max(spent)
        spent = [top] * len(streams)
        slot += 1


def _staged(out, fn, arg_lists, cost):
    for args in zip(*arg_lists):
        out.append(fn(*args))
        yield cost


class _NextBlockProjection:
    def __init__(self, xn_ref, w_ref, pbuf_ref, chunk=2 * LANE):
        self.xb = xn_ref[...].astype(BF16)
        self.w_ref = w_ref
        self.pbuf_ref = pbuf_ref
        width = w_ref.shape[1]
        self.bounds = [(lo, min(lo + chunk, width)) for lo in range(0, width, chunk)]

    def emit(self, count=1):
        for _ in range(min(count, len(self.bounds))):
            lo, hi = self.bounds.pop(0)
            self.pbuf_ref[:, lo:hi] = jnp.dot(self.xb, self.w_ref[:, lo:hi], preferred_element_type=F32)

    def spread(self, units_left):
        self.emit(-(-len(self.bounds) // max(units_left, 1)))

    def flush(self):
        self.emit(len(self.bounds))


def _mixer_call(kernel_fn, x, w, params, out_width, scratch, tb, name):
    bsz, seq, d = x.shape
    nt = seq // tb
    width = w.shape[1]
    full = lambda a: pl.BlockSpec(a.shape, lambda t: (0,) * a.ndim)
    return pl.pallas_call(
        functools.partial(kernel_fn, tb=tb),
        grid=(nt,),
        in_specs=[pl.BlockSpec((bsz, tb, d), lambda t: (0, 0, 0)),
                  pl.BlockSpec((bsz, tb, d), lambda t: (0, jnp.minimum(t + 1, nt - 1), 0)),
                  pl.BlockSpec((d, width), lambda t: (0, 0), pipeline_mode=pl.Buffered(1))]
                 + [full(a) for a in params],
        out_specs=pl.BlockSpec((bsz, tb, out_width), lambda t: (0, t, 0)),
        out_shape=jax.ShapeDtypeStruct((bsz, seq, out_width), BF16),
        scratch_shapes=[pltpu.VMEM((bsz, tb, width), F32)] + scratch(bsz),
        compiler_params=pltpu.CompilerParams(dimension_semantics=("arbitrary",),
                                             vmem_limit_bytes=VMEM_LIMIT),
        name=name,
    )(x, x, w, *params)


def _mixer_prologue(x0_ref, w_ref, pbuf_ref, zero_refs):
    @pl.when(pl.program_id(0) == 0)
    def _():
        for ref in zero_refs:
            ref[...] = jnp.zeros_like(ref)
        for b in range(x0_ref.shape[0]):
            pbuf_ref[b] = jnp.dot(x0_ref[b].astype(BF16), w_ref[...], preferred_element_type=F32)


def _gla_stream(xn_ref, w_ref, gup_ref, gb_ref, ng_ref, tri_ref, o_ref, pbuf_ref, p_ref, st_ref, tb, first):
    dk, dv, hk, hv = GLA_DK, GLA_DV, GLA_HK, GLA_HV
    p_ref[...] = pbuf_ref[...]
    nxt = _NextBlockProjection(xn_ref, w_ref, pbuf_ref)
    incl, _, _ = _chunk_masks(UNIT)
    gd = p_ref[:, 2 * dk + 2 * dv:2 * dk + 2 * dv + LANE]
    log_a = _log_sigmoid(_dot(gd, gup_ref[...]) + gb_ref[...]) * (1.0 / GLA_GATE_TEMP)
    b_all = _cumsum_rows(tri_ref[...], log_a)
    mid = CHUNK // 2
    nchunk = UNIT // CHUNK
    probs = [(h, u) for u in range(tb // UNIT) for h in range(GLA_HEADS)]
    yield _ELEMENTWISE_UNIT

    pre = []
    for n, (h, u) in enumerate(probs):
        if first:
            nxt.spread(len(probs) - n)
        rows = slice(u * UNIT, (u + 1) * UNIT)
        b = b_all[rows, h * hk:(h + 1) * hk]
        b_mid = _per_chunk_rows(b, mid)
        b_last = _per_chunk_rows(b, CHUNK - 1)
        qs = p_ref[rows, h * hk:(h + 1) * hk] * (hk ** -0.5) * jnp.exp(b - b_mid)
        ks = p_ref[rows, dk + h * hk:dk + (h + 1) * hk] * jnp.exp(b_mid - b)
        pre.append(dict(qs=qs, ks=ks, qe=qs * jnp.exp(b_mid), kd=ks * jnp.exp(b_last - b_mid),
                        e_last=jnp.exp(b_last),
                        v=p_ref[rows, 2 * dk + h * hv:2 * dk + (h + 1) * hv],
                        og=p_ref[rows, 2 * dk + dv + h * hv:2 * dk + dv + (h + 1) * hv]))
        yield _ELEMENTWISE_UNIT
    yield _PHASE_END

    attns, intras, kvs = [], [], []
    yield from _staged(attns, lambda d: jnp.where(incl, _dot_nt(d["qs"], d["ks"]), 0.0), [pre], 1.0)
    yield from _staged(intras, lambda a, d: _dot(a, d["v"]), [attns, pre], 2.0)
    yield from _staged(
        kvs, lambda d: [_dot_tn(d["v"][c * CHUNK:(c + 1) * CHUNK], d["kd"][c * CHUNK:(c + 1) * CHUNK])
                        for c in range(nchunk)], [pre], 2.0)
    yield _PHASE_END

    states = []
    for i, (h, u) in enumerate(probs):
        st = st_ref[h]
        per_chunk = []
        for c in range(nchunk):
            per_chunk.append(st)
            st = st * pre[i]["e_last"][c * CHUNK:c * CHUNK + 1] + kvs[i][c]
        st_ref[h] = st
        states.append(per_chunk)
        yield 0.3 * _ELEMENTWISE_UNIT
    units = len(probs) * nchunk
    for i, (h, u) in enumerate(probs):
        for c in range(nchunk):
            if not first:
                nxt.spread(units - (i * nchunk + c))
            rc = slice(c * CHUNK, (c + 1) * CHUNK)
            o = _dot_nt(pre[i]["qe"][rc], states[i][c]) + intras[i][rc]
            o = o * lax.rsqrt(jnp.mean(o * o, -1, keepdims=True) + RMS_EPS) * ng_ref[...]
            out_rows = slice(u * UNIT + c * CHUNK, u * UNIT + (c + 1) * CHUNK)
            o_ref[out_rows, h * hv:(h + 1) * hv] = (o * _silu(pre[i]["og"][rc])).astype(o_ref.dtype)
            yield 0.5 * _ELEMENTWISE_UNIT
    nxt.flush()


def _gla_kernel(x0_ref, xn_ref, w_ref, gup_ref, gb_ref, ng_ref, tri_ref, o_ref, pbuf_ref, p_ref, st_ref, *, tb):
    _mixer_prologue(x0_ref, w_ref, pbuf_ref, [st_ref])
    _run_staggered([_gla_stream(xn_ref.at[b], w_ref, gup_ref, gb_ref, ng_ref, tri_ref, o_ref.at[b],
                                pbuf_ref.at[b], p_ref.at[b], st_ref.at[b], tb, first=(b == 0))
                    for b in range(o_ref.shape[0])])


def _gla(x, w, gup, gb, ng, tb=256):
    scratch = lambda bsz: [pltpu.VMEM((bsz, tb, w.shape[1]), F32),
                           pltpu.VMEM((bsz, GLA_HEADS, GLA_HV, GLA_HK), F32)]
    return _mixer_call(_gla_kernel, x, w, [gup, gb, ng, _block_tril(tb)], GLA_DV, scratch, tb, "gla")


def _gdn_stream(xn_ref, w_ref, cw_ref, alog_row_ref, dtb_row_ref, alog_col_ref, dtb_col_ref, ng_ref, tri_ref,
                o_ref, pbuf_ref, st_ref, buf_ref, tb, first):
    wq = 3 * GDN_WIDTH
    hd = GDN_HD
    x = pbuf_ref[:, 0:wq]
    z_all = pbuf_ref[:, wq:wq + GDN_WIDTH]
    ab = pbuf_ref[:, wq + GDN_WIDTH:wq + GDN_WIDTH + LANE]
    nxt = _NextBlockProjection(xn_ref, w_ref, pbuf_ref)
    probs = [(h, u) for u in range(tb // UNIT) for h in range(GDN_HEADS)]
    n_groups = wq // (2 * LANE)
    early_units = n_groups + len(probs)

    buf_ref[SUBLANE:SUBLANE + tb, :] = x
    groups = []
    for g in range(n_groups):
        if first:
            nxt.spread(early_units - g)
        cols = slice(g * 2 * LANE, (g + 1) * 2 * LANE)
        y = x[:, cols] * cw_ref[GDN_CONV - 1:GDN_CONV, cols]
        for j in range(1, GDN_CONV):
            y = y + buf_ref[SUBLANE - j:SUBLANE - j + tb, cols] * cw_ref[GDN_CONV - 1 - j:GDN_CONV - j, cols]
        groups.append(_silu(y))
        yield _ELEMENTWISE_UNIT
    buf_ref[0:SUBLANE, :] = x[tb - SUBLANE:tb]
    qkv = jnp.concatenate(groups, axis=1)

    g_cols = -jnp.exp(alog_row_ref[...]) * _softplus(ab + dtb_row_ref[...])
    g_rows = -jnp.exp(alog_col_ref[...]) * _softplus(ab.T[0:SUBLANE] + dtb_col_ref[...])
    gam_cols = _cumsum_rows(tri_ref[...], g_cols)
    gam_rows = _cumsum_lanes(g_rows, tri_ref[...])
    beta_cols = _sigmoid(ab)
    yield 0.5 * _ELEMENTWISE_UNIT

    incl, strict, diag = _chunk_masks(UNIT)
    eye = diag.astype(F32)
    nchunk = UNIT // CHUNK

    pre = []
    for n, (h, u) in enumerate(probs):
        if first:
            nxt.spread(len(probs) - n)
        rows = slice(u * UNIT, (u + 1) * UNIT)
        q = qkv[rows, h * hd:(h + 1) * hd]
        k = qkv[rows, GDN_WIDTH + h * hd:GDN_WIDTH + (h + 1) * hd]
        v = qkv[rows, 2 * GDN_WIDTH + h * hd:2 * GDN_WIDTH + (h + 1) * hd]
        q = q * lax.rsqrt(jnp.sum(q * q, -1, keepdims=True) + L2_EPS) * (hd ** -0.5)
        k = k * lax.rsqrt(jnp.sum(k * k, -1, keepdims=True) + L2_EPS)
        gam = gam_cols[rows, h:h + 1]
        gam_r = gam_rows[h:h + 1, rows]
        beta = beta_cols[rows, GDN_HEADS + h:GDN_HEADS + h + 1]
        g_last = _per_chunk_rows(gam, CHUNK - 1)
        e_gam = jnp.exp(gam)
        kb = k * beta
        pre.append(dict(
            q=q, k=k, kb=kb, qe=q * e_gam,
            decay=jnp.where(incl, jnp.exp(jnp.minimum(gam - gam_r, 0.0)), 0.0),
            rhs=jnp.concatenate([v * beta, kb * e_gam], axis=1),
            kd=k * jnp.exp(g_last - gam), e_last=jnp.exp(g_last)))
        yield _ELEMENTWISE_UNIT
    yield _PHASE_END

    kqs, a_mats, t_invs, uws, ros, qps = [], [], [], [], [], []
    yield from _staged(kqs, lambda d: _dot_nt(jnp.concatenate([d["kb"], d["q"]], axis=0), d["k"]), [pre], 2.0)
    a_mats = [jnp.where(strict, kq[0:UNIT] * d["decay"], 0.0) for kq, d in zip(kqs, pre)]
    yield from _unit_lower_inverse(t_invs, a_mats, eye)
    yield from _staged(uws, lambda t, d: _dot(t, d["rhs"]), [t_invs, pre], 2.0)
    yield from _staged(ros, lambda kq, d, uw: _dot(kq[UNIT:2 * UNIT] * d["decay"], uw), [kqs, pre, uws], 2.0)
    r_mats = [d["qe"] - ro[:, hd:2 * hd] for d, ro in zip(pre, ros)]
    yield from _staged(
        qps, lambda d, uw: [_dot_tn(d["kd"][c * CHUNK:(c + 1) * CHUNK],
                                    jnp.concatenate([uw[c * CHUNK:(c + 1) * CHUNK, 0:hd],
                                                     -uw[c * CHUNK:(c + 1) * CHUNK, hd:2 * hd]], axis=1))
                            for c in range(nchunk)], [pre, uws], 4.0)
    yield _PHASE_END

    outs = {}
    for u in range(tb // UNIT):
        for c in range(nchunk):
            rc = slice(c * CHUNK, (c + 1) * CHUNK)
            for h in range(GDN_HEADS):
                i = u * GDN_HEADS + h
                st = st_ref[h]
                outs[(i, c)] = _dot(r_mats[i][rc], st) + ros[i][rc, 0:hd]
                qp = qps[i][c]
                st_ref[h] = st * pre[i]["e_last"][c * CHUNK:c * CHUNK + 1] + _dot(qp[:, hd:2 * hd], st) + qp[:, 0:hd]
                yield 2.0

    units = len(probs) * nchunk
    for i, (h, u) in enumerate(probs):
        for c in range(nchunk):
            if not first:
                nxt.spread(units - (i * nchunk + c))
            o = outs[(i, c)]
            o = o * lax.rsqrt(jnp.mean(o * o, -1, keepdims=True) + RMS_EPS) * ng_ref[...]
            out_rows = slice(u * UNIT + c * CHUNK, u * UNIT + (c + 1) * CHUNK)
            z = z_all[out_rows, h * hd:(h + 1) * hd]
            o_ref[out_rows, h * hd:(h + 1) * hd] = (o * _silu(z)).astype(o_ref.dtype)
            yield 0.3 * _ELEMENTWISE_UNIT
    nxt.flush()


def _gdn_kernel(x0_ref, xn_ref, w_ref, cw_ref, alog_row_ref, dtb_row_ref, alog_col_ref, dtb_col_ref, ng_ref,
                tri_ref, o_ref, pbuf_ref, st_ref, buf_ref, *, tb):
    _mixer_prologue(x0_ref, w_ref, pbuf_ref, [st_ref, buf_ref])
    _run_staggered([_gdn_stream(xn_ref.at[b], w_ref, cw_ref, alog_row_ref, dtb_row_ref, alog_col_ref, dtb_col_ref,
                                ng_ref, tri_ref, o_ref.at[b], pbuf_ref.at[b], st_ref.at[b], buf_ref.at[b],
                                tb, first=(b == 0))
                    for b in range(o_ref.shape[0])])


def _gdn(x, w, cw, alog, dtb, ng, tb=256):
    lane_row = lambda v: jnp.pad(v, (0, LANE - v.shape[0]))[None, :]
    sub_col = lambda v: jnp.pad(v, (0, SUBLANE - v.shape[0]))[:, None]
    params = [cw, lane_row(alog), lane_row(dtb), sub_col(alog), sub_col(dtb), ng, _block_tril(tb)]
    scratch = lambda bsz: [pltpu.VMEM((bsz, GDN_HEADS, GDN_HD, GDN_HD), F32),
                           pltpu.VMEM((bsz, tb + SUBLANE, 3 * GDN_WIDTH), F32)]
    return _mixer_call(_gdn_kernel, x, w, params, GDN_WIDTH, scratch, tb, "gdn")


def _stack_heads(x, head0):
    return jnp.concatenate([jnp.where(head0, x, 0.0), jnp.where(head0, 0.0, x)], axis=0)


def _rwkv_stream(xn_ref, w_ref, mu_ref, w0_ref, w2_ref, a0_ref, a2_ref, g2_ref, kk_ref, ka_ref,
                 rk_ref, lng_ref, lnb_ref, tri_ref, o_ref, pbuf_ref, st_ref, buf_ref, tb, first):
    buf_ref[SUBLANE:SUBLANE + tb, :] = pbuf_ref[...]
    nxt = _NextBlockProjection(xn_ref, w_ref, pbuf_ref)
    npair = RWKV_WIDTH // LANE
    nchunk = tb // CHUNK
    probs = [(pair, c) for c in range(nchunk) for pair in range(npair)]
    if first:
        nxt.spread(len(probs) // 4)
    p = buf_ref[SUBLANE:SUBLANE + tb, :]
    prev = buf_ref[SUBLANE - 1:SUBLANE - 1 + tb, :]
    buf_ref[0:SUBLANE, :] = p[tb - SUBLANE:tb]
    p = p + (prev - p) * mu_ref[...]
    wd = RWKV_WIDTH
    r_all = p[:, 0:wd]
    k_in = p[:, wd:2 * wd]
    v_all = p[:, 2 * wd:3 * wd]
    d_in = p[:, 3 * wd:3 * wd + LANE]
    a_in = p[:, 3 * wd + LANE:3 * wd + 2 * LANE]
    g_in = p[:, 3 * wd + 2 * LANE:3 * wd + 3 * LANE]
    lw_all = -jnp.exp(-_softplus(-(w0_ref[...] + _dot(jnp.tanh(d_in), w2_ref[...]))) - 0.5)
    a_all = _sigmoid(a0_ref[...] + _dot(a_in, a2_ref[...]))
    g_all = _dot(_sigmoid(g_in), g2_ref[...])
    kkraw_all = k_in * kk_ref[...]
    k_all = k_in * (1.0 + (a_all - 1.0) * ka_ref[...])
    cs_all = _cumsum_rows(tri_ref[...], lw_all)
    yield 4 * _ELEMENTWISE_UNIT

    row = lax.broadcasted_iota(jnp.int32, (CHUNK, LANE), 0)
    lane = lax.broadcasted_iota(jnp.int32, (CHUNK, LANE), 1)
    head0 = lane < RWKV_HD
    pos = lane % RWKV_HD
    incl, strict = row >= pos, row > pos
    eye = (row == pos).astype(F32)
    brow = lax.broadcasted_iota(jnp.int32, (LANE, LANE), 0) < RWKV_HD
    bcol = lax.broadcasted_iota(jnp.int32, (LANE, LANE), 1) < RWKV_HD
    same_head = brow == bcol
    mid = CHUNK // 2
    inv_hd = 1.0 / RWKV_HD
    stack = lambda x: _stack_heads(x.astype(BF16), head0)
    fold = lambda x: jnp.where(head0, x[0:CHUNK], x[CHUNK:LANE])

    def head_sum(x):
        s0 = jnp.sum(jnp.where(head0, x, 0.0), -1, keepdims=True)
        s1 = jnp.sum(jnp.where(head0, 0.0, x), -1, keepdims=True)
        return jnp.where(head0, s0, s1)

    pslice = lambda pair: slice(pair * LANE, (pair + 1) * LANE)
    cslice = lambda c: slice(c * CHUNK, (c + 1) * CHUNK)

    pre = []
    for n, (pair, c) in enumerate(probs):
        if first:
            nxt.spread(len(probs) - n)
        rs, ps = cslice(c), pslice(pair)
        r = r_all[rs, ps]
        k = k_all[rs, ps]
        v = v_all[rs, ps]
        lw = lw_all[rs, ps]
        cs = cs_all[rs, ps]
        c_mid = cs[mid:mid + 1]
        c_last = cs[CHUNK - 1:CHUNK]
        e_out = jnp.exp(c_mid - cs)
        e_mid = jnp.exp(c_mid)
        e_last_mid = jnp.exp(c_last - c_mid)
        kk = kkraw_all[rs, ps]
        kk = kk * lax.rsqrt(head_sum(kk * kk) + L2_EPS)
        r_s = r * jnp.exp(cs - c_mid)
        kk_s = kk * jnp.exp(cs - lw - c_mid)
        al_s = kk * a_all[rs, ps] * e_out
        k_s = k * e_out
        pre.append(dict(
            lhs=jnp.concatenate([kk_s, r_s], axis=0).astype(BF16),
            rhs=jnp.concatenate([stack(al_s), stack(k_s)], axis=0),
            v_st=stack(v), v=v.astype(BF16), kk2_st=stack(kk_s * e_mid),
            al2=(al_s * e_last_mid).astype(BF16), k2=(k_s * e_last_mid).astype(BF16),
            r2=r_s * e_mid, e_last=jnp.exp(c_last), rkr=r * k * rk_ref[:, ps]))
        yield _ELEMENTWISE_UNIT
    yield _PHASE_END

    grams, t_invs, bvs, tkws, zs, qfulls, x2s, akvs = [], [], [], [], [], [], [], []
    yield from _staged(grams, lambda d: _dot_nt(d["lhs"], d["rhs"]), [pre], 2.0)
    a_mats = [jnp.where(strict, g[0:CHUNK, 0:LANE], 0.0) for g in grams]
    yield from _unit_lower_inverse(t_invs, a_mats, eye, stack)
    yield from _staged(bvs, lambda g, d: _dot(jnp.where(strict, g[0:CHUNK, LANE:2 * LANE], 0.0), d["v_st"]),
                       [grams, pre], 1.0)
    yield from _staged(tkws, lambda t, d, bv: _dot(t, jnp.concatenate([d["kk2_st"], stack(bv)], axis=1)),
                       [t_invs, pre, bvs], 1.5)
    yield from _staged(zs, lambda tkw, d: _dot_tn(tkw, d["al2"]), [tkws, pre], 2.0)
    p_mats = [jnp.where(same_head, -z[0:LANE], 0.0).astype(BF16) for z in zs]
    yield from _staged(qfulls, lambda d: _dot_tn(d["v"], d["k2"]), [pre], 1.0)
    q_mats = [fold(qf) - fold(z[LANE:2 * LANE]) for qf, z in zip(qfulls, zs)]
    yield from _staged(
        x2s, lambda g, tkw: _dot(jnp.where(incl, g[CHUNK:2 * CHUNK, 0:LANE], 0.0),
                                 jnp.concatenate([stack(tkw[:, 0:LANE]), stack(tkw[:, LANE:2 * LANE])], axis=1)),
        [grams, tkws], 1.5)
    r_mats = [d["r2"] - x2[:, 0:LANE] for d, x2 in zip(pre, x2s)]
    yield from _staged(akvs, lambda g, d: _dot(jnp.where(incl, g[CHUNK:2 * CHUNK, LANE:2 * LANE], 0.0), d["v_st"]),
                       [grams, pre], 1.0)
    o_intras = [akv - x2[:, LANE:2 * LANE] for akv, x2 in zip(akvs, x2s)]
    yield _PHASE_END

    o_wide = []
    for i, (pair, c) in enumerate(probs):
        st = st_ref[pair]
        o_wide.append(_dot_nt(r_mats[i], stack(st)) + o_intras[i])
        st_ref[pair] = st * pre[i]["e_last"] + _dot(st, p_mats[i]) + q_mats[i]
        yield 2.0

    for i, (pair, c) in enumerate(probs):
        if not first:
            nxt.spread(len(probs) - i)
        rs, ps = cslice(c), pslice(pair)
        o = o_wide[i]
        d = o - head_sum(o) * inv_hd
        y = d * lax.rsqrt(head_sum(d * d) * inv_hd + RWKV_GN_EPS)
        bonus = head_sum(pre[i]["rkr"]) * v_all[rs, ps]
        o_ref[rs, ps] = ((y * lng_ref[:, ps] + lnb_ref[:, ps] + bonus) * g_all[rs, ps]).astype(o_ref.dtype)
        yield 0.4 * _ELEMENTWISE_UNIT
    nxt.flush()


def _rwkv_kernel(x0_ref, xn_ref, w_ref, mu_ref, w0_ref, w2_ref, a0_ref, a2_ref, g2_ref, kk_ref, ka_ref,
                 rk_ref, lng_ref, lnb_ref, tri_ref, o_ref, pbuf_ref, st_ref, buf_ref, *, tb):
    _mixer_prologue(x0_ref, w_ref, pbuf_ref, [st_ref, buf_ref])
    _run_staggered([_rwkv_stream(xn_ref.at[b], w_ref, mu_ref, w0_ref, w2_ref, a0_ref, a2_ref, g2_ref, kk_ref,
                                 ka_ref, rk_ref, lng_ref, lnb_ref, tri_ref, o_ref.at[b], pbuf_ref.at[b],
                                 st_ref.at[b], buf_ref.at[b], tb, first=(b == 0))
                    for b in range(o_ref.shape[0])])


def _rwkv(x, w, mu, w0, w2, a0, a2, g2, k_k, k_a, r_k, ln_g, ln_b, tb=256):
    params = [mu, w0, w2, a0, a2, g2, k_k, k_a, r_k, ln_g, ln_b, _block_tril(tb)]
    scratch = lambda bsz: [pltpu.VMEM((bsz, RWKV_WIDTH // LANE, CHUNK, LANE), F32),
                           pltpu.VMEM((bsz, tb + SUBLANE, w.shape[1]), F32)]
    return _mixer_call(_rwkv_kernel, x, w, params, RWKV_WIDTH, scratch, tb, "rwkv")


def _merge_kernel(x_ref, gt_ref, ogla_ref, ogdn_ref, orwkv_ref, wgla_ref, wgdn_ref, wrwkv_ref, wout_ref,
                  g_ref, b_ref, o_ref):
    d = D_MODEL
    gate = lambda i: _sigmoid(gt_ref[:, i * d:(i + 1) * d].astype(F32))
    merged = (gate(0) * _dot(ogla_ref[...], wgla_ref[...])
              + gate(1) * _dot(ogdn_ref[...], wgdn_ref[...])
              + gate(2) * _dot(orwkv_ref[...], wrwkv_ref[...]))
    mix = _dot(merged, wout_ref[...])
    o_ref[...] = _layer_norm(DN_ALPHA * x_ref[...] + mix, g_ref[...], b_ref[...])


def _merge(x, gates, ogla, ogdn, orwkv, wgla, wgdn, wrwkv, wout, g, b, tm=512):
    t = x.shape[0]
    d = D_MODEL
    tok = lambda w: pl.BlockSpec((tm, w), lambda i: (i, 0))
    full = lambda shape: pl.BlockSpec(shape, lambda i: (0, 0))
    return pl.pallas_call(
        _merge_kernel,
        grid=(t // tm,),
        in_specs=[tok(d), tok(N_BRANCH * d), tok(GLA_DV), tok(GDN_WIDTH), tok(RWKV_WIDTH),
                  full((GLA_DV, d)), full((GDN_WIDTH, d)), full((RWKV_WIDTH, d)), full((d, d)),
                  full((1, d)), full((1, d))],
        out_specs=tok(d),
        out_shape=jax.ShapeDtypeStruct((t, d), F32),
        compiler_params=pltpu.CompilerParams(dimension_semantics=("parallel",),
                                             vmem_limit_bytes=VMEM_LIMIT),
        name="merge",
    )(x, gates, ogla, ogdn, orwkv, wgla, wgdn, wrwkv, wout, g, b)


def _mlp_kernel(x_ref, wu_ref, wd_ref, g_ref, b_ref, o_ref, *, tf):
    x = x_ref[...]
    xb = x.astype(BF16)
    acc = None
    for j in range(D_FF // tf):
        hid = jnp.maximum(jnp.dot(xb, wu_ref[:, j * tf:(j + 1) * tf], preferred_element_type=F32), 0.0)
        part = jnp.dot((hid * hid).astype(BF16), wd_ref[j * tf:(j + 1) * tf, :], preferred_element_type=F32)
        acc = part if acc is None else acc + part
    o_ref[...] = _layer_norm(DN_ALPHA * x + acc, g_ref[...], b_ref[...])


def _mlp(x, wu, wd, g, b, layer, tm=512, tf=1024):
    t = x.shape[0]
    d = D_MODEL
    resident = lambda shape: pl.BlockSpec((None,) + shape, lambda i: (layer, 0, 0),
                                          pipeline_mode=pl.Buffered(1))
    return pl.pallas_call(
        functools.partial(_mlp_kernel, tf=tf),
        grid=(t // tm,),
        in_specs=[pl.BlockSpec((tm, d), lambda i: (i, 0)),
                  resident((d, D_FF)), resident((D_FF, d)),
                  pl.BlockSpec((1, d), lambda i: (0, 0)),
                  pl.BlockSpec((1, d), lambda i: (0, 0))],
        out_specs=pl.BlockSpec((tm, d), lambda i: (i, 0)),
        out_shape=jax.ShapeDtypeStruct((t, d), F32),
        compiler_params=pltpu.CompilerParams(dimension_semantics=("parallel",),
                                             vmem_limit_bytes=VMEM_LIMIT),
        name="mlp",
    )(x, wu, wd, g, b)


def _pad_cols(w, width):
    return jnp.pad(w, ((0, 0), (0, width - w.shape[1])))


def _pad_rows(w, height):
    return jnp.pad(w, ((0, height - w.shape[0]), (0, 0)))


def _split_w_in(w_in):
    o = _IN_OFFS
    col = lambda i: w_in[:, o[i]:o[i + 1]]
    gq, gk, gv, gga, ggr, dqkv, da, db, dz, rp, gates = (col(i) for i in range(11))
    w_gla = jnp.concatenate([gq, gk, gv, ggr, _pad_cols(gga, LANE)], axis=1)
    w_gdn = jnp.concatenate([dqkv, dz, _pad_cols(jnp.concatenate([da, db], axis=1), LANE)], axis=1)
    s = 3 * RWKV_WIDTH
    w_rwkv = jnp.concatenate([rp[:, :s],
                              _pad_cols(rp[:, s:s + RWKV_DECAY_RANK], LANE),
                              _pad_cols(rp[:, s + RWKV_DECAY_RANK:s + RWKV_DECAY_RANK + RWKV_A_RANK], LANE),
                              rp[:, s + RWKV_DECAY_RANK + RWKV_A_RANK:]], axis=1)
    return tuple(w.astype(BF16) for w in (w_gla, w_gdn, w_rwkv, gates))


def _pad_mu(mu):
    s = 3 * RWKV_WIDTH
    z = jnp.zeros((LANE - RWKV_DECAY_RANK,), F32)
    return jnp.concatenate([mu[:s], mu[s:s + RWKV_DECAY_RANK], z,
                            mu[s + RWKV_DECAY_RANK:s + RWKV_DECAY_RANK + RWKV_A_RANK], z,
                            mu[s + RWKV_DECAY_RANK + RWKV_A_RANK:]])[None, :]


def kernel(x, w_in, gla_gate_up, gla_gate_bias, gla_norm_g, gdn_conv, gdn_a_log, gdn_dt_bias, gdn_norm_g, rwkv_mu, rwkv_w0, rwkv_w2, rwkv_a0, rwkv_a2, rwkv_g2, rwkv_k_k, rwkv_k_a, rwkv_r_k, rwkv_ln_g, rwkv_ln_b, w_br_gla, w_br_gdn, w_br_rwkv, w_out, ln1_g, ln1_b, w_up, w_down, ln2_g, ln2_b):
    bsz, seq, d = x.shape
    xt = x.reshape(bsz * seq, d)
    as_rows = lambda o: o.reshape(bsz * seq, o.shape[-1])
    row = lambda v: v[None, :]
    w_up_b = w_up.astype(BF16)
    w_down_b = w_down.astype(BF16)
    for l in range(DEPTH):
        w_gla, w_gdn, w_rwkv, w_gates = _split_w_in(w_in[l])
        p_gates = _project(xt, w_gates, out_dtype=BF16)

        xs = xt.reshape(bsz, seq, d)
        o_gla = _gla(xs, w_gla, _pad_rows(gla_gate_up[l], LANE).astype(BF16), row(gla_gate_bias[l]),
                     row(gla_norm_g[l]))
        o_gdn = _gdn(xs, w_gdn, gdn_conv[l], gdn_a_log[l], gdn_dt_bias[l], row(gdn_norm_g[l]))
        o_rwkv = _rwkv(
            xs, w_rwkv, _pad_mu(rwkv_mu[l]), row(rwkv_w0[l]), _pad_rows(rwkv_w2[l], LANE).astype(BF16),
            row(rwkv_a0[l]), _pad_rows(rwkv_a2[l], LANE).astype(BF16), rwkv_g2[l].astype(BF16),
            row(rwkv_k_k[l]), row(rwkv_k_a[l]), row(rwkv_r_k[l]), row(rwkv_ln_g[l]), row(rwkv_ln_b[l]))

        xt = _merge(xt, p_gates, as_rows(o_gla), as_rows(o_gdn), as_rows(o_rwkv), w_br_gla[l].astype(BF16), w_br_gdn[l].astype(BF16),
                    w_br_rwkv[l].astype(BF16), w_out[l].astype(BF16), row(ln1_g[l]), row(ln1_b[l]))
        xt = _mlp(xt, w_up_b, w_down_b, row(ln2_g[l]), row(ln2_b[l]), l)
    return xt.reshape(bsz, seq, d)
```

```python
import functools

import jax
import jax.numpy as jnp
from jax import lax
from jax.experimental import pallas as pl
from jax.experimental.pallas import tpu as pltpu

F32 = jnp.float32
BF16 = jnp.bfloat16

D_MODEL = 1024
DEPTH = 2
CHUNK = 64
GLA_HEADS = 4
GLA_DK = 512
GLA_DV = 1024
GLA_HK = 128
GLA_HV = 256
GLA_GATE_RANK = 16
GLA_GATE_TEMP = 16.0
GDN_HEADS = 4
GDN_HD = 128
GDN_WIDTH = 512
GDN_CONV = 4
RWKV_HD = 64
RWKV_WIDTH = 512
RWKV_HEADS = 8
RWKV_DECAY_RANK = 64
RWKV_A_RANK = 64
RWKV_GATE_RANK = 128
RWKV_GN_EPS = 64e-5
N_BRANCH = 3
D_FF = 4 * D_MODEL
DN_ALPHA = (2 * DEPTH) ** 0.25
LN_EPS = 1e-5
RMS_EPS = 1e-6
L2_EPS = 1e-6

LANE = 128
SUBLANE = 8
VMEM_LIMIT = 56 * 1024 * 1024
UNIT = 2 * CHUNK

_IN_WIDTHS = (GLA_DK, GLA_DK, GLA_DV, GLA_GATE_RANK, GLA_DV,
              3 * GDN_WIDTH, GDN_HEADS, GDN_HEADS, GDN_WIDTH,
              3 * RWKV_WIDTH + RWKV_DECAY_RANK + RWKV_A_RANK + RWKV_GATE_RANK,
              N_BRANCH * D_MODEL)
_IN_OFFS = [0]
for _w in _IN_WIDTHS:
    _IN_OFFS.append(_IN_OFFS[-1] + _w)


def _dot(a, b):
    return jnp.dot(a.astype(BF16), b.astype(BF16), preferred_element_type=F32)


def _dot_nt(a, b):
    return lax.dot_general(a.astype(BF16), b.astype(BF16), (((1,), (1,)), ((), ())),
                           preferred_element_type=F32)


def _dot_tn(a, b):
    return lax.dot_general(a.astype(BF16), b.astype(BF16), (((0,), (0,)), ((), ())),
                           preferred_element_type=F32)


def _split_bf16(x):
    hi = x.astype(BF16)
    return hi, (x - hi.astype(F32)).astype(BF16)


def _cumsum_rows(tri, x):
    hi, lo = _split_bf16(x)
    return (jnp.dot(tri, hi, preferred_element_type=F32)
            + jnp.dot(tri, lo, preferred_element_type=F32))


def _cumsum_lanes(x, tri):
    hi, lo = _split_bf16(x)
    dims = (((1,), (1,)), ((), ()))
    return (lax.dot_general(hi, tri, dims, preferred_element_type=F32)
            + lax.dot_general(lo, tri, dims, preferred_element_type=F32))


def _sigmoid(x):
    return 1.0 / (1.0 + jnp.exp(-x))


def _silu(x):
    return x * _sigmoid(x)


def _softplus(x):
    return jnp.maximum(x, 0.0) + jnp.log1p(jnp.exp(-jnp.abs(x)))


def _log_sigmoid(x):
    return -_softplus(-x)


def _chunk_masks(n):
    row = lax.broadcasted_iota(jnp.int32, (n, n), 0)
    col = lax.broadcasted_iota(jnp.int32, (n, n), 1)
    same = (row // CHUNK) == (col // CHUNK)
    return same & (row >= col), same & (row > col), row == col


def _per_chunk_rows(x, offset):
    w = x.shape[1]
    return jnp.concatenate(
        [jnp.broadcast_to(x[c * CHUNK + offset:c * CHUNK + offset + 1], (CHUNK, w))
         for c in range(UNIT // CHUNK)], axis=0)


def _unit_lower_inverse(out, a_list, eye, as_rhs=lambda p: p):
    n = eye.shape[0]
    ps = [-a for a in a_list]
    ts = [eye + p for p in ps]
    squares = []
    yield from _staged(squares, lambda p: _dot(p, as_rhs(p)), [ps], 1.0)
    ps = squares
    for _ in range(CHUNK.bit_length() - 3):
        prods = []
        yield from _staged(prods, lambda t, p: _dot(jnp.concatenate([t, p], axis=0), as_rhs(p)), [ts, ps], 2.0)
        ts = [t + pr[0:n] for t, pr in zip(ts, prods)]
        ps = [pr[n:2 * n] for pr in prods]
    yield from _staged(out, lambda t, p: t + _dot(t, as_rhs(p)), [ts, ps], 1.0)


def _layer_norm(y, g, b):
    mu = jnp.mean(y, -1, keepdims=True)
    d = y - mu
    var = jnp.mean(d * d, -1, keepdims=True)
    return d * lax.rsqrt(var + LN_EPS) * g + b


def _block_tril(n):
    idx = jnp.arange(n)
    same = (idx[:, None] // CHUNK) == (idx[None, :] // CHUNK)
    return (same & (idx[:, None] >= idx[None, :])).astype(BF16)


_PHASE_END = "phase-end"
_ELEMENTWISE_UNIT = 10.0


def _run_staggered(streams):
    spent = [0.0] * len(streams)
    finished = [False] * len(streams)
    slot = 0
    while not all(finished):
        running = [j for j in range(len(streams)) if j <= slot and not finished[j]]
        while running:
            j = min(running, key=lambda i: spent[i])
            try:
                cost = next(streams[j])
            except StopIteration:
                finished[j] = True
                running.remove(j)
                continue
            if cost == _PHASE_END:
                running.remove(j)
            else:
                spent[j] += cost
        top = max(spent)
        spent = [top] * len(streams)
        slot += 1


def _staged(out, fn, arg_lists, cost):
    for args in zip(*arg_lists):
        out.append(fn(*args))
        yield cost


class _NextBlockProjection:
    def __init__(self, xn_ref, w_ref, pbuf_ref, chunk=2 * LANE):
        self.xb = xn_ref[...].astype(BF16)
        self.w_ref = w_ref
        self.pbuf_ref = pbuf_ref
        width = w_ref.shape[1]
        self.bounds = [(lo, min(lo + chunk, width)) for lo in range(0, width, chunk)]

    def emit(self, count=1):
        for _ in range(min(count, len(self.bounds))):
            lo, hi = self.bounds.pop(0)
            self.pbuf_ref[:, lo:hi] = jnp.dot(self.xb, self.w_ref[:, lo:hi], preferred_element_type=F32)

    def spread(self, units_left):
        self.emit(-(-len(self.bounds) // max(units_left, 1)))

    def flush(self):
        self.emit(len(self.bounds))


def _mixer_call(kernel_fn, x, w, params, out_width, scratch, tb, name):
    bsz, seq, d = x.shape
    nt = seq // tb
    width = w.shape[1]
    full = lambda a: pl.BlockSpec(a.shape, lambda t: (0,) * a.ndim)
    return pl.pallas_call(
        functools.partial(kernel_fn, tb=tb),
        grid=(nt,),
        in_specs=[pl.BlockSpec((bsz, tb, d), lambda t: (0, 0, 0)),
                  pl.BlockSpec((bsz, tb, d), lambda t: (0, jnp.minimum(t + 1, nt - 1), 0)),
                  pl.BlockSpec((d, width), lambda t: (0, 0), pipeline_mode=pl.Buffered(1))]
                 + [full(a) for a in params],
        out_specs=pl.BlockSpec((bsz, tb, out_width), lambda t: (0, t, 0)),
        out_shape=jax.ShapeDtypeStruct((bsz, seq, out_width), BF16),
        scratch_shapes=[pltpu.VMEM((bsz, tb, width), F32)] + scratch(bsz),
        compiler_params=pltpu.CompilerParams(dimension_semantics=("arbitrary",),
                                             vmem_limit_bytes=VMEM_LIMIT),
        name=name,
    )(x, x, w, *params)


def _mixer_prologue(x0_ref, w_ref, pbuf_ref, zero_refs):
    @pl.when(pl.program_id(0) == 0)
    def _():
        for ref in zero_refs:
            ref[...] = jnp.zeros_like(ref)
        for b in range(x0_ref.shape[0]):
            pbuf_ref[b] = jnp.dot(x0_ref[b].astype(BF16), w_ref[...], preferred_element_type=F32)


def _gla_stream(xn_ref, w_ref, gup_ref, gb_ref, ng_ref, tri_ref, o_ref, pbuf_ref, p_ref, st_ref, tb, first):
    dk, dv, hk, hv = GLA_DK, GLA_DV, GLA_HK, GLA_HV
    p_ref[...] = pbuf_ref[...]
    nxt = _NextBlockProjection(xn_ref, w_ref, pbuf_ref)
    incl, _, _ = _chunk_masks(UNIT)
    gd = p_ref[:, 2 * dk + 2 * dv:2 * dk + 2 * dv + LANE]
    log_a = _log_sigmoid(_dot(gd, gup_ref[...]) + gb_ref[...]) * (1.0 / GLA_GATE_TEMP)
    b_all = _cumsum_rows(tri_ref[...], log_a)
    mid = CHUNK // 2
    nchunk = UNIT // CHUNK
    probs = [(h, u) for u in range(tb // UNIT) for h in range(GLA_HEADS)]
    yield _ELEMENTWISE_UNIT

    pre = []
    for n, (h, u) in enumerate(probs):
        if first:
            nxt.spread(len(probs) - n)
        rows = slice(u * UNIT, (u + 1) * UNIT)
        b = b_all[rows, h * hk:(h + 1) * hk]
        b_mid = _per_chunk_rows(b, mid)
        b_last = _per_chunk_rows(b, CHUNK - 1)
        qs = p_ref[rows, h * hk:(h + 1) * hk] * (hk ** -0.5) * jnp.exp(b - b_mid)
        ks = p_ref[rows, dk + h * hk:dk + (h + 1) * hk] * jnp.exp(b_mid - b)
        pre.append(dict(qs=qs, ks=ks, qe=qs * jnp.exp(b_mid), kd=ks * jnp.exp(b_last - b_mid),
                        e_last=jnp.exp(b_last),
                        v=p_ref[rows, 2 * dk + h * hv:2 * dk + (h + 1) * hv],
                        og=p_ref[rows, 2 * dk + dv + h * hv:2 * dk + dv + (h + 1) * hv]))
        yield _ELEMENTWISE_UNIT
    yield _PHASE_END

    attns, intras, kvs = [], [], []
    yield from _staged(attns, lambda d: jnp.where(incl, _dot_nt(d["qs"], d["ks"]), 0.0), [pre], 1.0)
    yield from _staged(intras, lambda a, d: _dot(a, d["v"]), [attns, pre], 2.0)
    yield from _staged(
        kvs, lambda d: [_dot_tn(d["v"][c * CHUNK:(c + 1) * CHUNK], d["kd"][c * CHUNK:(c + 1) * CHUNK])
                        for c in range(nchunk)], [pre], 2.0)
    yield _PHASE_END

    states = []
    for i, (h, u) in enumerate(probs):
        st = st_ref[h]
        per_chunk = []
        for c in range(nchunk):
            per_chunk.append(st)
            st = st * pre[i]["e_last"][c * CHUNK:c * CHUNK + 1] + kvs[i][c]
        st_ref[h] = st
        states.append(per_chunk)
        yield 0.3 * _ELEMENTWISE_UNIT
    units = len(probs) * nchunk
    for i, (h, u) in enumerate(probs):
        for c in range(nchunk):
            if not first:
                nxt.spread(units - (i * nchunk + c))
            rc = slice(c * CHUNK, (c + 1) * CHUNK)
            o = _dot_nt(pre[i]["qe"][rc], states[i][c]) + intras[i][rc]
            o = o * lax.rsqrt(jnp.mean(o * o, -1, keepdims=True) + RMS_EPS) * ng_ref[...]
            out_rows = slice(u * UNIT + c * CHUNK, u * UNIT + (c + 1) * CHUNK)
            o_ref[out_rows, h * hv:(h + 1) * hv] = (o * _silu(pre[i]["og"][rc])).astype(o_ref.dtype)
            yield 0.5 * _ELEMENTWISE_UNIT
    nxt.flush()


def _gla_kernel(x0_ref, xn_ref, w_ref, gup_ref, gb_ref, ng_ref, tri_ref, o_ref, pbuf_ref, p_ref, st_ref, *, tb):
    _mixer_prologue(x0_ref, w_ref, pbuf_ref, [st_ref])
    _run_staggered([_gla_stream(xn_ref.at[b], w_ref, gup_ref, gb_ref, ng_ref, tri_ref, o_ref.at[b],
                                pbuf_ref.at[b], p_ref.at[b], st_ref.at[b], tb, first=(b == 0))
                    for b in range(o_ref.shape[0])])


def _gla(x, w, gup, gb, ng, tb=256):
    scratch = lambda bsz: [pltpu.VMEM((bsz, tb, w.shape[1]), F32),
                           pltpu.VMEM((bsz, GLA_HEADS, GLA_HV, GLA_HK), F32)]
    return _mixer_call(_gla_kernel, x, w, [gup, gb, ng, _block_tril(tb)], GLA_DV, scratch, tb, "gla")


def _gdn_stream(xn_ref, w_ref, cw_ref, alog_row_ref, dtb_row_ref, alog_col_ref, dtb_col_ref, ng_ref, tri_ref,
                o_ref, pbuf_ref, st_ref, buf_ref, tb, first):
    wq = 3 * GDN_WIDTH
    hd = GDN_HD
    x = pbuf_ref[:, 0:wq]
    z_all = pbuf_ref[:, wq:wq + GDN_WIDTH]
    ab = pbuf_ref[:, wq + GDN_WIDTH:wq + GDN_WIDTH + LANE]
    nxt = _NextBlockProjection(xn_ref, w_ref, pbuf_ref)
    probs = [(h, u) for u in range(tb // UNIT) for h in range(GDN_HEADS)]
    n_groups = wq // (2 * LANE)
    early_units = n_groups + len(probs)

    buf_ref[SUBLANE:SUBLANE + tb, :] = x
    groups = []
    for g in range(n_groups):
        if first:
            nxt.spread(early_units - g)
        cols = slice(g * 2 * LANE, (g + 1) * 2 * LANE)
        y = x[:, cols] * cw_ref[GDN_CONV - 1:GDN_CONV, cols]
        for j in range(1, GDN_CONV):
            y = y + buf_ref[SUBLANE - j:SUBLANE - j + tb, cols] * cw_ref[GDN_CONV - 1 - j:GDN_CONV - j, cols]
        groups.append(_silu(y))
        yield _ELEMENTWISE_UNIT
    buf_ref[0:SUBLANE, :] = x[tb - SUBLANE:tb]
    qkv = jnp.concatenate(groups, axis=1)

    g_cols = -jnp.exp(alog_row_ref[...]) * _softplus(ab + dtb_row_ref[...])
    g_rows = -jnp.exp(alog_col_ref[...]) * _softplus(ab.T[0:SUBLANE] + dtb_col_ref[...])
    gam_cols = _cumsum_rows(tri_ref[...], g_cols)
    gam_rows = _cumsum_lanes(g_rows, tri_ref[...])
    beta_cols = _sigmoid(ab)
    yield 0.5 * _ELEMENTWISE_UNIT

    incl, strict, diag = _chunk_masks(UNIT)
    eye = diag.astype(F32)
    nchunk = UNIT // CHUNK

    pre = []
    for n, (h, u) in enumerate(probs):
        if first:
            nxt.spread(len(probs) - n)
        rows = slice(u * UNIT, (u + 1) * UNIT)
        q = qkv[rows, h * hd:(h + 1) * hd]
        k = qkv[rows, GDN_WIDTH + h * hd:GDN_WIDTH + (h + 1) * hd]
        v = qkv[rows, 2 * GDN_WIDTH + h * hd:2 * GDN_WIDTH + (h + 1) * hd]
        q = q * lax.rsqrt(jnp.sum(q * q, -1, keepdims=True) + L2_EPS) * (hd ** -0.5)
        k = k * lax.rsqrt(jnp.sum(k * k, -1, keepdims=True) + L2_EPS)
        gam = gam_cols[rows, h:h + 1]
        gam_r = gam_rows[h:h + 1, rows]
        beta = beta_cols[rows, GDN_HEADS + h:GDN_HEADS + h + 1]
        g_last = _per_chunk_rows(gam, CHUNK - 1)
        e_gam = jnp.exp(gam)
        kb = k * beta
        pre.append(dict(
            q=q, k=k, kb=kb, qe=q * e_gam,
            decay=jnp.where(incl, jnp.exp(jnp.minimum(gam - gam_r, 0.0)), 0.0),
            rhs=jnp.concatenate([v * beta, kb * e_gam], axis=1),
            kd=k * jnp.exp(g_last - gam), e_last=jnp.exp(g_last)))
        yield _ELEMENTWISE_UNIT
    yield _PHASE_END

    kqs, a_mats, t_invs, uws, ros, qps = [], [], [], [], [], []
    yield from _staged(kqs, lambda d: _dot_nt(jnp.concatenate([d["kb"], d["q"]], axis=0), d["k"]), [pre], 2.0)
    a_mats = [jnp.where(strict, kq[0:UNIT] * d["decay"], 0.0) for kq, d in zip(kqs, pre)]
    yield from _unit_lower_inverse(t_invs, a_mats, eye)
    yield from _staged(uws, lambda t, d: _dot(t, d["rhs"]), [t_invs, pre], 2.0)
    yield from _staged(ros, lambda kq, d, uw: _dot(kq[UNIT:2 * UNIT] * d["decay"], uw), [kqs, pre, uws], 2.0)
    r_mats = [d["qe"] - ro[:, hd:2 * hd] for d, ro in zip(pre, ros)]
    yield from _staged(
        qps, lambda d, uw: [_dot_tn(d["kd"][c * CHUNK:(c + 1) * CHUNK],
                                    jnp.concatenate([uw[c * CHUNK:(c + 1) * CHUNK, 0:hd],
                                                     -uw[c * CHUNK:(c + 1) * CHUNK, hd:2 * hd]], axis=1))
                            for c in range(nchunk)], [pre, uws], 4.0)
    yield _PHASE_END

    outs = {}
    for u in range(tb // UNIT):
        for c in range(nchunk):
            rc = slice(c * CHUNK, (c + 1) * CHUNK)
            for h in range(GDN_HEADS):
                i = u * GDN_HEADS + h
                st = st_ref[h]
                outs[(i, c)] = _dot(r_mats[i][rc], st) + ros[i][rc, 0:hd]
                qp = qps[i][c]
                st_ref[h] = st * pre[i]["e_last"][c * CHUNK:c * CHUNK + 1] + _dot(qp[:, hd:2 * hd], st) + qp[:, 0:hd]
                yield 2.0

    units = len(probs) * nchunk
    for i, (h, u) in enumerate(probs):
        for c in range(nchunk):
            if not first:
                nxt.spread(units - (i * nchunk + c))
            o = outs[(i, c)]
            o = o * lax.rsqrt(jnp.mean(o * o, -1, keepdims=True) + RMS_EPS) * ng_ref[...]
            out_rows = slice(u * UNIT + c * CHUNK, u * UNIT + (c + 1) * CHUNK)
            z = z_all[out_rows, h * hd:(h + 1) * hd]
            o_ref[out_rows, h * hd:(h + 1) * hd] = (o * _silu(z)).astype(o_ref.dtype)
            yield 0.3 * _ELEMENTWISE_UNIT
    nxt.flush()


def _gdn_kernel(x0_ref, xn_ref, w_ref, cw_ref, alog_row_ref, dtb_row_ref, alog_col_ref, dtb_col_ref, ng_ref,
                tri_ref, o_ref, pbuf_ref, st_ref, buf_ref, *, tb):
    _mixer_prologue(x0_ref, w_ref, pbuf_ref, [st_ref, buf_ref])
    _run_staggered([_gdn_stream(xn_ref.at[b], w_ref, cw_ref, alog_row_ref, dtb_row_ref, alog_col_ref, dtb_col_ref,
                                ng_ref, tri_ref, o_ref.at[b], pbuf_ref.at[b], st_ref.at[b], buf_ref.at[b],
                                tb, first=(b == 0))
                    for b in range(o_ref.shape[0])])


def _gdn(x, w, cw, alog, dtb, ng, tb=256):
    lane_row = lambda v: jnp.pad(v, (0, LANE - v.shape[0]))[None, :]
    sub_col = lambda v: jnp.pad(v, (0, SUBLANE - v.shape[0]))[:, None]
    params = [cw, lane_row(alog), lane_row(dtb), sub_col(alog), sub_col(dtb), ng, _block_tril(tb)]
    scratch = lambda bsz: [pltpu.VMEM((bsz, GDN_HEADS, GDN_HD, GDN_HD), F32),
                           pltpu.VMEM((bsz, tb + SUBLANE, 3 * GDN_WIDTH), F32)]
    return _mixer_call(_gdn_kernel, x, w, params, GDN_WIDTH, scratch, tb, "gdn")


def _stack_heads(x, head0):
    return jnp.concatenate([jnp.where(head0, x, 0.0), jnp.where(head0, 0.0, x)], axis=0)


def _rwkv_stream(xn_ref, w_ref, mu_ref, w0_ref, w2_ref, a0_ref, a2_ref, g2_ref, kk_ref, ka_ref,
                 rk_ref, lng_ref, lnb_ref, tri_ref, o_ref, pbuf_ref, st_ref, buf_ref, tb, first):
    buf_ref[SUBLANE:SUBLANE + tb, :] = pbuf_ref[...]
    nxt = _NextBlockProjection(xn_ref, w_ref, pbuf_ref)
    npair = RWKV_WIDTH // LANE
    nchunk = tb // CHUNK
    probs = [(pair, c) for c in range(nchunk) for pair in range(npair)]
    if first:
        nxt.spread(len(probs) // 4)
    p = buf_ref[SUBLANE:SUBLANE + tb, :]
    prev = buf_ref[SUBLANE - 1:SUBLANE - 1 + tb, :]
    buf_ref[0:SUBLANE, :] = p[tb - SUBLANE:tb]
    p = p + (prev - p) * mu_ref[...]
    wd = RWKV_WIDTH
    r_all = p[:, 0:wd]
    k_in = p[:, wd:2 * wd]
    v_all = p[:, 2 * wd:3 * wd]
    d_in = p[:, 3 * wd:3 * wd + LANE]
    a_in = p[:, 3 * wd + LANE:3 * wd + 2 * LANE]
    g_in = p[:, 3 * wd + 2 * LANE:3 * wd + 3 * LANE]
    lw_all = -jnp.exp(-_softplus(-(w0_ref[...] + _dot(jnp.tanh(d_in), w2_ref[...]))) - 0.5)
    a_all = _sigmoid(a0_ref[...] + _dot(a_in, a2_ref[...]))
    g_all = _dot(_sigmoid(g_in), g2_ref[...])
    kkraw_all = k_in * kk_ref[...]
    k_all = k_in * (1.0 + (a_all - 1.0) * ka_ref[...])
    cs_all = _cumsum_rows(tri_ref[...], lw_all)
    yield 4 * _ELEMENTWISE_UNIT

    row = lax.broadcasted_iota(jnp.int32, (CHUNK, LANE), 0)
    lane = lax.broadcasted_iota(jnp.int32, (CHUNK, LANE), 1)
    head0 = lane < RWKV_HD
    pos = lane % RWKV_HD
    incl, strict = row >= pos, row > pos
    eye = (row == pos).astype(F32)
    brow = lax.broadcasted_iota(jnp.int32, (LANE, LANE), 0) < RWKV_HD
    bcol = lax.broadcasted_iota(jnp.int32, (LANE, LANE), 1) < RWKV_HD
    same_head = brow == bcol
    mid = CHUNK // 2
    inv_hd = 1.0 / RWKV_HD
    stack = lambda x: _stack_heads(x.astype(BF16), head0)
    fold = lambda x: jnp.where(head0, x[0:CHUNK], x[CHUNK:LANE])

    def head_sum(x):
        s0 = jnp.sum(jnp.where(head0, x, 0.0), -1, keepdims=True)
        s1 = jnp.sum(jnp.where(head0, 0.0, x), -1, keepdims=True)
        return jnp.where(head0, s0, s1)

    pslice = lambda pair: slice(pair * LANE, (pair + 1) * LANE)
    cslice = lambda c: slice(c * CHUNK, (c + 1) * CHUNK)

    pre = []
    for n, (pair, c) in enumerate(probs):
        if first:
            nxt.spread(len(probs) - n)
        rs, ps = cslice(c), pslice(pair)
        r = r_all[rs, ps]
        k = k_all[rs, ps]
        v = v_all[rs, ps]
        lw = lw_all[rs, ps]
        cs = cs_all[rs, ps]
        c_mid = cs[mid:mid + 1]
        c_last = cs[CHUNK - 1:CHUNK]
        e_out = jnp.exp(c_mid - cs)
        e_mid = jnp.exp(c_mid)
        e_last_mid = jnp.exp(c_last - c_mid)
        kk = kkraw_all[rs, ps]
        kk = kk * lax.rsqrt(head_sum(kk * kk) + L2_EPS)
        r_s = r * jnp.exp(cs - c_mid)
        kk_s = kk * jnp.exp(cs - lw - c_mid)
        al_s = kk * a_all[rs, ps] * e_out
        k_s = k * e_out
        pre.append(dict(
            lhs=jnp.concatenate([kk_s, r_s], axis=0).astype(BF16),
            rhs=jnp.concatenate([stack(al_s), stack(k_s)], axis=0),
            v_st=stack(v), v=v.astype(BF16), kk2_st=stack(kk_s * e_mid),
            al2=(al_s * e_last_mid).astype(BF16), k2=(k_s * e_last_mid).astype(BF16),
            r2=r_s * e_mid, e_last=jnp.exp(c_last), rkr=r * k * rk_ref[:, ps]))
        yield _ELEMENTWISE_UNIT
    yield _PHASE_END

    grams, t_invs, bvs, tkws, zs, qfulls, x2s, akvs = [], [], [], [], [], [], [], []
    yield from _staged(grams, lambda d: _dot_nt(d["lhs"], d["rhs"]), [pre], 2.0)
    a_mats = [jnp.where(strict, g[0:CHUNK, 0:LANE], 0.0) for g in grams]
    yield from _unit_lower_inverse(t_invs, a_mats, eye, stack)
    yield from _staged(bvs, lambda g, d: _dot(jnp.where(strict, g[0:CHUNK, LANE:2 * LANE], 0.0), d["v_st"]),
                       [grams, pre], 1.0)
    yield from _staged(tkws, lambda t, d, bv: _dot(t, jnp.concatenate([d["kk2_st"], stack(bv)], axis=1)),
                       [t_invs, pre, bvs], 1.5)
    yield from _staged(zs, lambda tkw, d: _dot_tn(tkw, d["al2"]), [tkws, pre], 2.0)
    p_mats = [jnp.where(same_head, -z[0:LANE], 0.0).astype(BF16) for z in zs]
    yield from _staged(qfulls, lambda d: _dot_tn(d["v"], d["k2"]), [pre], 1.0)
    q_mats = [fold(qf) - fold(z[LANE:2 * LANE]) for qf, z in zip(qfulls, zs)]
    yield from _staged(
        x2s, lambda g, tkw: _dot(jnp.where(incl, g[CHUNK:2 * CHUNK, 0:LANE], 0.0),
                                 jnp.concatenate([stack(tkw[:, 0:LANE]), stack(tkw[:, LANE:2 * LANE])], axis=1)),
        [grams, tkws], 1.5)
    r_mats = [d["r2"] - x2[:, 0:LANE] for d, x2 in zip(pre, x2s)]
    yield from _staged(akvs, lambda g, d: _dot(jnp.where(incl, g[CHUNK:2 * CHUNK, LANE:2 * LANE], 0.0), d["v_st"]),
                       [grams, pre], 1.0)
    o_intras = [akv - x2[:, LANE:2 * LANE] for akv, x2 in zip(akvs, x2s)]
    yield _PHASE_END

    o_wide = []
    for i, (pair, c) in enumerate(probs):
        st = st_ref[pair]
        o_wide.append(_dot_nt(r_mats[i], stack(st)) + o_intras[i])
        st_ref[pair] = st * pre[i]["e_last"] + _dot(st, p_mats[i]) + q_mats[i]
        yield 2.0

    for i, (pair, c) in enumerate(probs):
        if not first:
            nxt.spread(len(probs) - i)
        rs, ps = cslice(c), pslice(pair)
        o = o_wide[i]
        d = o - head_sum(o) * inv_hd
        y = d * lax.rsqrt(head_sum(d * d) * inv_hd + RWKV_GN_EPS)
        bonus = head_sum(pre[i]["rkr"]) * v_all[rs, ps]
        o_ref[rs, ps] = ((y * lng_ref[:, ps] + lnb_ref[:, ps] + bonus) * g_all[rs, ps]).astype(o_ref.dtype)
        yield 0.4 * _ELEMENTWISE_UNIT
    nxt.flush()


def _rwkv_kernel(x0_ref, xn_ref, w_ref, mu_ref, w0_ref, w2_ref, a0_ref, a2_ref, g2_ref, kk_ref, ka_ref,
                 rk_ref, lng_ref, lnb_ref, tri_ref, o_ref, pbuf_ref, st_ref, buf_ref, *, tb):
    _mixer_prologue(x0_ref, w_ref, pbuf_ref, [st_ref, buf_ref])
    _run_staggered([_rwkv_stream(xn_ref.at[b], w_ref, mu_ref, w0_ref, w2_ref, a0_ref, a2_ref, g2_ref, kk_ref,
                                 ka_ref, rk_ref, lng_ref, lnb_ref, tri_ref, o_ref.at[b], pbuf_ref.at[b],
                                 st_ref.at[b], buf_ref.at[b], tb, first=(b == 0))
                    for b in range(o_ref.shape[0])])


def _rwkv(x, w, mu, w0, w2, a0, a2, g2, k_k, k_a, r_k, ln_g, ln_b, tb=256):
    params = [mu, w0, w2, a0, a2, g2, k_k, k_a, r_k, ln_g, ln_b, _block_tril(tb)]
    scratch = lambda bsz: [pltpu.VMEM((bsz, RWKV_WIDTH // LANE, CHUNK, LANE), F32),
                           pltpu.VMEM((bsz, tb + SUBLANE, w.shape[1]), F32)]
    return _mixer_call(_rwkv_kernel, x, w, params, RWKV_WIDTH, scratch, tb, "rwkv")


def _merge_kernel(x_ref, wgate_ref, ogla_ref, ogdn_ref, orwkv_ref, wgla_ref, wgdn_ref, wrwkv_ref, wout_ref,
                  g_ref, b_ref, o_ref):
    d = D_MODEL
    x = x_ref[...]
    xb = x.astype(BF16)
    gate = lambda i: _sigmoid(jnp.dot(xb, wgate_ref[:, i * d:(i + 1) * d], preferred_element_type=F32))
    merged = (gate(0) * _dot(ogla_ref[...], wgla_ref[...])
              + gate(1) * _dot(ogdn_ref[...], wgdn_ref[...])
              + gate(2) * _dot(orwkv_ref[...], wrwkv_ref[...]))
    mix = _dot(merged, wout_ref[...])
    o_ref[...] = _layer_norm(DN_ALPHA * x + mix, g_ref[...], b_ref[...])


def _merge(x, wgate, ogla, ogdn, orwkv, wgla, wgdn, wrwkv, wout, g, b, layer, tm=512):
    t = x.shape[0]
    d = D_MODEL
    tok = lambda w: pl.BlockSpec((tm, w), lambda i: (i, 0))
    full = lambda shape: pl.BlockSpec(shape, lambda i: (0, 0))
    resident = lambda rows: pl.BlockSpec((None, rows, d), lambda i: (layer, 0, 0),
                                         pipeline_mode=pl.Buffered(1))
    return pl.pallas_call(
        _merge_kernel,
        grid=(t // tm,),
        in_specs=[tok(d),
                  pl.BlockSpec((d, N_BRANCH * d), lambda i: (0, 0), pipeline_mode=pl.Buffered(1)),
                  tok(GLA_DV), tok(GDN_WIDTH), tok(RWKV_WIDTH),
                  resident(GLA_DV), resident(GDN_WIDTH), resident(RWKV_WIDTH), resident(d),
                  full((1, d)), full((1, d))],
        out_specs=tok(d),
        out_shape=jax.ShapeDtypeStruct((t, d), F32),
        compiler_params=pltpu.CompilerParams(dimension_semantics=("parallel",),
                                             vmem_limit_bytes=VMEM_LIMIT),
        name="merge",
    )(x, wgate, ogla, ogdn, orwkv, wgla, wgdn, wrwkv, wout, g, b)


def _mlp_kernel(x_ref, wu_ref, wd_ref, g_ref, b_ref, o_ref, *, tf):
    x = x_ref[...]
    xb = x.astype(BF16)
    acc = None
    for j in range(D_FF // tf):
        hid = jnp.maximum(jnp.dot(xb, wu_ref[:, j * tf:(j + 1) * tf], preferred_element_type=F32), 0.0)
        part = jnp.dot((hid * hid).astype(BF16), wd_ref[j * tf:(j + 1) * tf, :], preferred_element_type=F32)
        acc = part if acc is None else acc + part
    o_ref[...] = _layer_norm(DN_ALPHA * x + acc, g_ref[...], b_ref[...])


def _mlp(x, wu, wd, g, b, layer, tm=512, tf=1024):
    t = x.shape[0]
    d = D_MODEL
    resident = lambda shape: pl.BlockSpec((None,) + shape, lambda i: (layer, 0, 0),
                                          pipeline_mode=pl.Buffered(1))
    return pl.pallas_call(
        functools.partial(_mlp_kernel, tf=tf),
        grid=(t // tm,),
        in_specs=[pl.BlockSpec((tm, d), lambda i: (i, 0)),
                  resident((d, D_FF)), resident((D_FF, d)),
                  pl.BlockSpec((1, d), lambda i: (0, 0)),
                  pl.BlockSpec((1, d), lambda i: (0, 0))],
        out_specs=pl.BlockSpec((tm, d), lambda i: (i, 0)),
        out_shape=jax.ShapeDtypeStruct((t, d), F32),
        compiler_params=pltpu.CompilerParams(dimension_semantics=("parallel",),
                                             vmem_limit_bytes=VMEM_LIMIT),
        name="mlp",
    )(x, wu, wd, g, b)


def _pad_rows(w, height):
    return jnp.pad(w, ((0, height - w.shape[0]), (0, 0)))


def _relayout_plans():
    o = _IN_OFFS
    s = 3 * RWKV_WIDTH
    gla = (2 * GLA_DK + 2 * GLA_DV + LANE,
           [(0, o[0], 2 * GLA_DK + GLA_DV), (2 * GLA_DK + GLA_DV, o[4], GLA_DV),
            (2 * GLA_DK + 2 * GLA_DV, o[3], GLA_GATE_RANK)])
    gdn = (4 * GDN_WIDTH + LANE,
           [(0, o[5], 3 * GDN_WIDTH), (3 * GDN_WIDTH, o[8], GDN_WIDTH), (4 * GDN_WIDTH, o[6], 2 * GDN_HEADS)])
    rwkv = (s + 3 * LANE,
            [(0, o[9], s), (s, o[9] + s, RWKV_DECAY_RANK), (s + LANE, o[9] + s + RWKV_DECAY_RANK, RWKV_A_RANK),
             (s + 2 * LANE, o[9] + s + RWKV_DECAY_RANK + RWKV_A_RANK, RWKV_GATE_RANK)])
    gates = (N_BRANCH * D_MODEL, [(0, o[10], N_BRANCH * D_MODEL)])
    return gla, gdn, rwkv, gates


def _relayout_kernel(w_ref, *out_refs):
    for out_ref, (width, pieces) in zip(out_refs, _relayout_plans()):
        covered = 0
        for dst, src, n in pieces:
            if dst > covered:
                out_ref[:, covered:dst] = jnp.zeros((out_ref.shape[0], dst - covered), out_ref.dtype)
            lo = src // LANE * LANE
            hi = min(-(-(src + n) // LANE) * LANE, w_ref.shape[1])
            x = w_ref[:, lo:hi]
            out_ref[:, dst:dst + n] = x[:, src - lo:src - lo + n].astype(out_ref.dtype)
            covered = dst + n
        if width > covered:
            out_ref[:, covered:width] = jnp.zeros((out_ref.shape[0], width - covered), out_ref.dtype)


def _split_w_in(w_in, layer, tk=128):
    _, k, n_in = w_in.shape
    widths = [width for width, _ in _relayout_plans()]
    return pl.pallas_call(
        _relayout_kernel,
        grid=(k // tk,),
        in_specs=[pl.BlockSpec((None, tk, n_in), lambda i: (layer, i, 0))],
        out_specs=[pl.BlockSpec((tk, width), lambda i: (i, 0)) for width in widths],
        out_shape=[jax.ShapeDtypeStruct((k, width), BF16) for width in widths],
        compiler_params=pltpu.CompilerParams(dimension_semantics=("parallel",),
                                             vmem_limit_bytes=VMEM_LIMIT),
        name="w_in_relayout",
    )(w_in)


def _pad_mu(mu):
    s = 3 * RWKV_WIDTH
    z = jnp.zeros((LANE - RWKV_DECAY_RANK,), F32)
    return jnp.concatenate([mu[:s], mu[s:s + RWKV_DECAY_RANK], z,
                            mu[s + RWKV_DECAY_RANK:s + RWKV_DECAY_RANK + RWKV_A_RANK], z,
                            mu[s + RWKV_DECAY_RANK + RWKV_A_RANK:]])[None, :]


def kernel(x, w_in, gla_gate_up, gla_gate_bias, gla_norm_g, gdn_conv, gdn_a_log, gdn_dt_bias, gdn_norm_g, rwkv_mu, rwkv_w0, rwkv_w2, rwkv_a0, rwkv_a2, rwkv_g2, rwkv_k_k, rwkv_k_a, rwkv_r_k, rwkv_ln_g, rwkv_ln_b, w_br_gla, w_br_gdn, w_br_rwkv, w_out, ln1_g, ln1_b, w_up, w_down, ln2_g, ln2_b):
    bsz, seq, d = x.shape
    xt = x.reshape(bsz * seq, d)
    as_rows = lambda o: o.reshape(bsz * seq, o.shape[-1])
    row = lambda v: v[None, :]
    w_up_b = w_up.astype(BF16)
    w_down_b = w_down.astype(BF16)
    w_br_b = [w.astype(BF16) for w in (w_br_gla, w_br_gdn, w_br_rwkv)]
    w_out_b = w_out.astype(BF16)
    for l in range(DEPTH):
        w_gla, w_gdn, w_rwkv, w_gates = _split_w_in(w_in, l)
        xs = xt.reshape(bsz, seq, d)
        o_gla = _gla(xs, w_gla, _pad_rows(gla_gate_up[l], LANE).astype(BF16), row(gla_gate_bias[l]),
                     row(gla_norm_g[l]))
        o_gdn = _gdn(xs, w_gdn, gdn_conv[l], gdn_a_log[l], gdn_dt_bias[l], row(gdn_norm_g[l]))
        o_rwkv = _rwkv(
            xs, w_rwkv, _pad_mu(rwkv_mu[l]), row(rwkv_w0[l]), _pad_rows(rwkv_w2[l], LANE).astype(BF16),
            row(rwkv_a0[l]), _pad_rows(rwkv_a2[l], LANE).astype(BF16), rwkv_g2[l].astype(BF16),
            row(rwkv_k_k[l]), row(rwkv_k_a[l]), row(rwkv_r_k[l]), row(rwkv_ln_g[l]), row(rwkv_ln_b[l]))

        xt = _merge(xt, w_gates, as_rows(o_gla), as_rows(o_gdn), as_rows(o_rwkv), w_br_b[0], w_br_b[1], w_br_b[2],
                    w_out_b, row(ln1_g[l]), row(ln1_b[l]), l)
        xt = _mlp(xt, w_up_b, w_down_b, row(ln2_g[l]), row(ln2_b[l]), l)
    return xt.reshape(bsz, seq, d)
```

```python
import functools

import jax
import jax.numpy as jnp
from jax import lax
from jax.experimental import pallas as pl
from jax.experimental.pallas import tpu as pltpu

F32 = jnp.float32
BF16 = jnp.bfloat16

D_MODEL = 1024
DEPTH = 2
CHUNK = 64
GLA_HEADS = 4
GLA_DK = 512
GLA_DV = 1024
GLA_HK = 128
GLA_HV = 256
GLA_GATE_RANK = 16
GLA_GATE_TEMP = 16.0
GDN_HEADS = 4
GDN_HD = 128
GDN_WIDTH = 512
GDN_CONV = 4
RWKV_HD = 64
RWKV_WIDTH = 512
RWKV_HEADS = 8
RWKV_DECAY_RANK = 64
RWKV_A_RANK = 64
RWKV_GATE_RANK = 128
RWKV_GN_EPS = 64e-5
N_BRANCH = 3
D_FF = 4 * D_MODEL
DN_ALPHA = (2 * DEPTH) ** 0.25
LN_EPS = 1e-5
RMS_EPS = 1e-6
L2_EPS = 1e-6

LANE = 128
SUBLANE = 8
VMEM_LIMIT = 56 * 1024 * 1024
UNIT = 2 * CHUNK

_IN_WIDTHS = (GLA_DK, GLA_DK, GLA_DV, GLA_GATE_RANK, GLA_DV,
              3 * GDN_WIDTH, GDN_HEADS, GDN_HEADS, GDN_WIDTH,
              3 * RWKV_WIDTH + RWKV_DECAY_RANK + RWKV_A_RANK + RWKV_GATE_RANK,
              N_BRANCH * D_MODEL)
_IN_OFFS = [0]
for _w in _IN_WIDTHS:
    _IN_OFFS.append(_IN_OFFS[-1] + _w)


def _dot(a, b):
    return jnp.dot(a.astype(BF16), b.astype(BF16), preferred_element_type=F32)


def _dot_nt(a, b):
    return lax.dot_general(a.astype(BF16), b.astype(BF16), (((1,), (1,)), ((), ())),
                           preferred_element_type=F32)


def _dot_tn(a, b):
    return lax.dot_general(a.astype(BF16), b.astype(BF16), (((0,), (0,)), ((), ())),
                           preferred_element_type=F32)


def _split_bf16(x):
    hi = x.astype(BF16)
    return hi, (x - hi.astype(F32)).astype(BF16)


def _cumsum_rows(tri, x):
    hi, lo = _split_bf16(x)
    return (jnp.dot(tri, hi, preferred_element_type=F32)
            + jnp.dot(tri, lo, preferred_element_type=F32))


def _cumsum_lanes(x, tri):
    hi, lo = _split_bf16(x)
    dims = (((1,), (1,)), ((), ()))
    return (lax.dot_general(hi, tri, dims, preferred_element_type=F32)
            + lax.dot_general(lo, tri, dims, preferred_element_type=F32))


def _sigmoid(x):
    return 1.0 / (1.0 + jnp.exp(-x))


def _silu(x):
    return x * _sigmoid(x)


def _softplus(x):
    return jnp.maximum(x, 0.0) + jnp.log1p(jnp.exp(-jnp.abs(x)))


def _log_sigmoid(x):
    return -_softplus(-x)


def _chunk_masks(n):
    row = lax.broadcasted_iota(jnp.int32, (n, n), 0)
    col = lax.broadcasted_iota(jnp.int32, (n, n), 1)
    same = (row // CHUNK) == (col // CHUNK)
    return same & (row >= col), same & (row > col), row == col


def _per_chunk_rows(x, offset):
    w = x.shape[1]
    return jnp.concatenate(
        [jnp.broadcast_to(x[c * CHUNK + offset:c * CHUNK + offset + 1], (CHUNK, w))
         for c in range(UNIT // CHUNK)], axis=0)


def _unit_lower_inverse(out, a_list, eye, as_rhs=lambda p: p):
    n = eye.shape[0]
    ps = [-a for a in a_list]
    ts = [eye + p for p in ps]
    squares = []
    yield from _staged(squares, lambda p: _dot(p, as_rhs(p)), [ps], 1.0)
    ps = squares
    for _ in range(CHUNK.bit_length() - 3):
        prods = []
        yield from _staged(prods, lambda t, p: _dot(jnp.concatenate([t, p], axis=0), as_rhs(p)), [ts, ps], 2.0)
        ts = [t + pr[0:n] for t, pr in zip(ts, prods)]
        ps = [pr[n:2 * n] for pr in prods]
    yield from _staged(out, lambda t, p: t + _dot(t, as_rhs(p)), [ts, ps], 1.0)


def _layer_norm(y, g, b):
    mu = jnp.mean(y, -1, keepdims=True)
    d = y - mu
    var = jnp.mean(d * d, -1, keepdims=True)
    return d * lax.rsqrt(var + LN_EPS) * g + b


def _block_tril(n):
    idx = jnp.arange(n)
    same = (idx[:, None] // CHUNK) == (idx[None, :] // CHUNK)
    return (same & (idx[:, None] >= idx[None, :])).astype(BF16)


_PHASE_END = "phase-end"
_ELEMENTWISE_UNIT = 10.0


def _run_staggered(streams):
    spent = [0.0] * len(streams)
    finished = [False] * len(streams)
    slot = 0
    while not all(finished):
        running = [j for j in range(len(streams)) if j <= slot and not finished[j]]
        while running:
            j = min(running, key=lambda i: spent[i])
            try:
                cost = next(streams[j])
            except StopIteration:
                finished[j] = True
                running.remove(j)
                continue
            if cost == _PHASE_END:
                running.remove(j)
            else:
                spent[j] += cost
        top = max(spent)
        spent = [top] * len(streams)
        slot += 1


def _staged(out, fn, arg_lists, cost):
    for args in zip(*arg_lists):
        out.append(fn(*args))
        yield cost


class _NextBlockProjection:
    def __init__(self, xn_ref, w_ref, pbuf_ref, chunk=2 * LANE):
        self.xb = xn_ref[...].astype(BF16)
        self.w_ref = w_ref
        self.pbuf_ref = pbuf_ref
        width = w_ref.shape[1]
        self.bounds = [(lo, min(lo + chunk, width)) for lo in range(0, width, chunk)]

    def emit(self, count=1):
        for _ in range(min(count, len(self.bounds))):
            lo, hi = self.bounds.pop(0)
            self.pbuf_ref[:, lo:hi] = jnp.dot(self.xb, self.w_ref[:, lo:hi], preferred_element_type=F32)

    def spread(self, units_left):
        self.emit(-(-len(self.bounds) // max(units_left, 1)))

    def flush(self):
        self.emit(len(self.bounds))


def _mixer_call(kernel_fn, x, w, params, out_width, scratch, tb, blocks, name):
    bsz, seq, d = x.shape
    rows = blocks * tb
    nt = seq // rows
    width = w.shape[1]
    full = lambda a: pl.BlockSpec(a.shape, lambda t: (0,) * a.ndim)
    return pl.pallas_call(
        functools.partial(kernel_fn, tb=tb),
        grid=(nt,),
        in_specs=[pl.BlockSpec((bsz, rows, d), lambda t: (0, 0, 0), pipeline_mode=pl.Buffered(1)),
                  pl.BlockSpec((bsz, rows, d), lambda t: (0, jnp.minimum(t + 1, nt - 1), 0)),
                  pl.BlockSpec((d, width), lambda t: (0, 0), pipeline_mode=pl.Buffered(1))]
                 + [full(a) for a in params],
        out_specs=pl.BlockSpec((bsz, rows, out_width), lambda t: (0, t, 0)),
        out_shape=jax.ShapeDtypeStruct((bsz, seq, out_width), BF16),
        scratch_shapes=[pltpu.VMEM((bsz, rows, width), F32)] + scratch(bsz),
        compiler_params=pltpu.CompilerParams(dimension_semantics=("arbitrary",),
                                             vmem_limit_bytes=VMEM_LIMIT),
        name=name,
    )(x, x, w, *params)


def _stream_blocks(o_ref, tb):
    bsz, rows = o_ref.shape[0], o_ref.shape[1]
    return [(blk * bsz + b, b, pl.ds(blk * tb, tb)) for blk in range(rows // tb) for b in range(bsz)]


def _mixer_prologue(x0_ref, w_ref, pbuf_ref, zero_refs):
    @pl.when(pl.program_id(0) == 0)
    def _():
        for ref in zero_refs:
            ref[...] = jnp.zeros_like(ref)
        for b in range(x0_ref.shape[0]):
            pbuf_ref[b] = jnp.dot(x0_ref[b].astype(BF16), w_ref[...], preferred_element_type=F32)


def _gla_stream(xn_ref, w_ref, gup_ref, gb_ref, ng_ref, tri_ref, o_ref, pbuf_ref, st_ref, tb):
    dk, dv, hk, hv = GLA_DK, GLA_DV, GLA_HK, GLA_HV
    p_ref = pbuf_ref
    nxt = _NextBlockProjection(xn_ref, w_ref, pbuf_ref)
    incl, _, _ = _chunk_masks(UNIT)
    gd = p_ref[:, 2 * dk + 2 * dv:2 * dk + 2 * dv + LANE]
    log_a = _log_sigmoid(_dot(gd, gup_ref[...]) + gb_ref[...]) * (1.0 / GLA_GATE_TEMP)
    b_all = _cumsum_rows(tri_ref[...], log_a)
    mid = CHUNK // 2
    nchunk = UNIT // CHUNK
    probs = [(h, u) for u in range(tb // UNIT) for h in range(GLA_HEADS)]
    yield _ELEMENTWISE_UNIT

    pre = []
    for h, u in probs:
        rows = slice(u * UNIT, (u + 1) * UNIT)
        b = b_all[rows, h * hk:(h + 1) * hk]
        b_mid = _per_chunk_rows(b, mid)
        b_last = _per_chunk_rows(b, CHUNK - 1)
        qs = p_ref[rows, h * hk:(h + 1) * hk] * (hk ** -0.5) * jnp.exp(b - b_mid)
        ks = p_ref[rows, dk + h * hk:dk + (h + 1) * hk] * jnp.exp(b_mid - b)
        pre.append(dict(qs=qs, ks=ks, qe=qs * jnp.exp(b_mid), kd=ks * jnp.exp(b_last - b_mid),
                        e_last=jnp.exp(b_last),
                        v=p_ref[rows, 2 * dk + h * hv:2 * dk + (h + 1) * hv],
                        og=p_ref[rows, 2 * dk + dv + h * hv:2 * dk + dv + (h + 1) * hv]))
        yield _ELEMENTWISE_UNIT
    yield _PHASE_END

    attns, intras, kvs = [], [], []
    yield from _staged(attns, lambda d: jnp.where(incl, _dot_nt(d["qs"], d["ks"]), 0.0), [pre], 1.0)
    yield from _staged(intras, lambda a, d: _dot(a, d["v"]), [attns, pre], 2.0)
    yield from _staged(
        kvs, lambda d: [_dot_tn(d["v"][c * CHUNK:(c + 1) * CHUNK], d["kd"][c * CHUNK:(c + 1) * CHUNK])
                        for c in range(nchunk)], [pre], 2.0)
    yield _PHASE_END

    states = []
    for i, (h, u) in enumerate(probs):
        st = st_ref[h]
        per_chunk = []
        for c in range(nchunk):
            per_chunk.append(st)
            st = st * pre[i]["e_last"][c * CHUNK:c * CHUNK + 1] + kvs[i][c]
        st_ref[h] = st
        states.append(per_chunk)
        yield 0.3 * _ELEMENTWISE_UNIT
    units = len(probs) * nchunk
    for i, (h, u) in enumerate(probs):
        for c in range(nchunk):
            nxt.spread(units - (i * nchunk + c))
            rc = slice(c * CHUNK, (c + 1) * CHUNK)
            o = _dot_nt(pre[i]["qe"][rc], states[i][c]) + intras[i][rc]
            o = o * lax.rsqrt(jnp.mean(o * o, -1, keepdims=True) + RMS_EPS) * ng_ref[...]
            out_rows = slice(u * UNIT + c * CHUNK, u * UNIT + (c + 1) * CHUNK)
            o_ref[out_rows, h * hv:(h + 1) * hv] = (o * _silu(pre[i]["og"][rc])).astype(o_ref.dtype)
            yield 0.5 * _ELEMENTWISE_UNIT
    nxt.flush()


def _gla_kernel(x0_ref, xn_ref, w_ref, gup_ref, gb_ref, ng_ref, tri_ref, o_ref, pbuf_ref, st_ref, *, tb):
    _mixer_prologue(x0_ref, w_ref, pbuf_ref, [st_ref])
    _run_staggered([_gla_stream(xn_ref.at[b, rows], w_ref, gup_ref, gb_ref, ng_ref, tri_ref, o_ref.at[b, rows],
                                pbuf_ref.at[b, rows], st_ref.at[b], tb)
                    for _, b, rows in _stream_blocks(o_ref, tb)])


def _gla(x, w, gup, gb, ng, tb=256):
    scratch = lambda bsz: [pltpu.VMEM((bsz, GLA_HEADS, GLA_HV, GLA_HK), F32)]
    return _mixer_call(_gla_kernel, x, w, [gup, gb, ng, _block_tril(tb)], GLA_DV, scratch, tb, 2, "gla")


def _gdn_stream(xn_ref, w_ref, cw_ref, alog_row_ref, dtb_row_ref, alog_col_ref, dtb_col_ref, ng_ref, tri_ref,
                o_ref, pbuf_ref, st_ref, buf_ref, tb, first):
    wq = 3 * GDN_WIDTH
    hd = GDN_HD
    x = pbuf_ref[:, 0:wq]
    z_all = pbuf_ref[:, wq:wq + GDN_WIDTH]
    ab = pbuf_ref[:, wq + GDN_WIDTH:wq + GDN_WIDTH + LANE]
    nxt = _NextBlockProjection(xn_ref, w_ref, pbuf_ref)
    probs = [(h, u) for u in range(tb // UNIT) for h in range(GDN_HEADS)]
    n_groups = wq // (2 * LANE)
    early_units = n_groups + len(probs)

    buf_ref[SUBLANE:SUBLANE + tb, :] = x
    groups = []
    for g in range(n_groups):
        if first:
            nxt.spread(early_units - g)
        cols = slice(g * 2 * LANE, (g + 1) * 2 * LANE)
        y = x[:, cols] * cw_ref[GDN_CONV - 1:GDN_CONV, cols]
        for j in range(1, GDN_CONV):
            y = y + buf_ref[SUBLANE - j:SUBLANE - j + tb, cols] * cw_ref[GDN_CONV - 1 - j:GDN_CONV - j, cols]
        groups.append(_silu(y))
        yield _ELEMENTWISE_UNIT
    buf_ref[0:SUBLANE, :] = x[tb - SUBLANE:tb]
    qkv = jnp.concatenate(groups, axis=1)

    g_cols = -jnp.exp(alog_row_ref[...]) * _softplus(ab + dtb_row_ref[...])
    g_rows = -jnp.exp(alog_col_ref[...]) * _softplus(ab.T[0:SUBLANE] + dtb_col_ref[...])
    gam_cols = _cumsum_rows(tri_ref[...], g_cols)
    gam_rows = _cumsum_lanes(g_rows, tri_ref[...])
    beta_cols = _sigmoid(ab)
    yield 0.5 * _ELEMENTWISE_UNIT

    incl, strict, diag = _chunk_masks(UNIT)
    eye = diag.astype(F32)
    nchunk = UNIT // CHUNK

    pre = []
    for n, (h, u) in enumerate(probs):
        if first:
            nxt.spread(len(probs) - n)
        rows = slice(u * UNIT, (u + 1) * UNIT)
        q = qkv[rows, h * hd:(h + 1) * hd]
        k = qkv[rows, GDN_WIDTH + h * hd:GDN_WIDTH + (h + 1) * hd]
        v = qkv[rows, 2 * GDN_WIDTH + h * hd:2 * GDN_WIDTH + (h + 1) * hd]
        q = q * lax.rsqrt(jnp.sum(q * q, -1, keepdims=True) + L2_EPS) * (hd ** -0.5)
        k = k * lax.rsqrt(jnp.sum(k * k, -1, keepdims=True) + L2_EPS)
        gam = gam_cols[rows, h:h + 1]
        gam_r = gam_rows[h:h + 1, rows]
        beta = beta_cols[rows, GDN_HEADS + h:GDN_HEADS + h + 1]
        g_last = _per_chunk_rows(gam, CHUNK - 1)
        e_gam = jnp.exp(gam)
        kb = k * beta
        pre.append(dict(
            q=q, k=k, kb=kb, qe=q * e_gam,
            decay=jnp.where(incl, jnp.exp(jnp.minimum(gam - gam_r, 0.0)), 0.0),
            rhs=jnp.concatenate([v * beta, kb * e_gam], axis=1),
            kd=k * jnp.exp(g_last - gam), e_last=jnp.exp(g_last)))
        yield _ELEMENTWISE_UNIT
    yield _PHASE_END

    kqs, a_mats, t_invs, uws, ros, qps = [], [], [], [], [], []
    yield from _staged(kqs, lambda d: _dot_nt(jnp.concatenate([d["kb"], d["q"]], axis=0), d["k"]), [pre], 2.0)
    a_mats = [jnp.where(strict, kq[0:UNIT] * d["decay"], 0.0) for kq, d in zip(kqs, pre)]
    yield from _unit_lower_inverse(t_invs, a_mats, eye)
    yield from _staged(uws, lambda t, d: _dot(t, d["rhs"]), [t_invs, pre], 2.0)
    yield from _staged(ros, lambda kq, d, uw: _dot(kq[UNIT:2 * UNIT] * d["decay"], uw), [kqs, pre, uws], 2.0)
    r_mats = [d["qe"] - ro[:, hd:2 * hd] for d, ro in zip(pre, ros)]
    yield from _staged(
        qps, lambda d, uw: [_dot_tn(d["kd"][c * CHUNK:(c + 1) * CHUNK],
                                    jnp.concatenate([uw[c * CHUNK:(c + 1) * CHUNK, 0:hd],
                                                     -uw[c * CHUNK:(c + 1) * CHUNK, hd:2 * hd]], axis=1))
                            for c in range(nchunk)], [pre, uws], 4.0)
    yield _PHASE_END

    outs = {}
    for u in range(tb // UNIT):
        for c in range(nchunk):
            rc = slice(c * CHUNK, (c + 1) * CHUNK)
            for h in range(GDN_HEADS):
                i = u * GDN_HEADS + h
                st = st_ref[h]
                outs[(i, c)] = _dot(r_mats[i][rc], st) + ros[i][rc, 0:hd]
                qp = qps[i][c]
                st_ref[h] = st * pre[i]["e_last"][c * CHUNK:c * CHUNK + 1] + _dot(qp[:, hd:2 * hd], st) + qp[:, 0:hd]
                yield 2.0

    units = len(probs) * nchunk
    for i, (h, u) in enumerate(probs):
        for c in range(nchunk):
            if not first:
                nxt.spread(units - (i * nchunk + c))
            o = outs[(i, c)]
            o = o * lax.rsqrt(jnp.mean(o * o, -1, keepdims=True) + RMS_EPS) * ng_ref[...]
            out_rows = slice(u * UNIT + c * CHUNK, u * UNIT + (c + 1) * CHUNK)
            z = z_all[out_rows, h * hd:(h + 1) * hd]
            o_ref[out_rows, h * hd:(h + 1) * hd] = (o * _silu(z)).astype(o_ref.dtype)
            yield 0.3 * _ELEMENTWISE_UNIT
    nxt.flush()


def _gdn_kernel(x0_ref, xn_ref, w_ref, cw_ref, alog_row_ref, dtb_row_ref, alog_col_ref, dtb_col_ref, ng_ref,
                tri_ref, o_ref, pbuf_ref, st_ref, buf_ref, *, tb):
    _mixer_prologue(x0_ref, w_ref, pbuf_ref, [st_ref, buf_ref])
    _run_staggered([_gdn_stream(xn_ref.at[b, rows], w_ref, cw_ref, alog_row_ref, dtb_row_ref, alog_col_ref,
                                dtb_col_ref, ng_ref, tri_ref, o_ref.at[b, rows], pbuf_ref.at[b, rows],
                                st_ref.at[b], buf_ref.at[b], tb, first=(j == 0))
                    for j, b, rows in _stream_blocks(o_ref, tb)])


def _gdn(x, w, cw, alog, dtb, ng, tb=256):
    lane_row = lambda v: jnp.pad(v, (0, LANE - v.shape[0]))[None, :]
    sub_col = lambda v: jnp.pad(v, (0, SUBLANE - v.shape[0]))[:, None]
    params = [cw, lane_row(alog), lane_row(dtb), sub_col(alog), sub_col(dtb), ng, _block_tril(tb)]
    scratch = lambda bsz: [pltpu.VMEM((bsz, GDN_HEADS, GDN_HD, GDN_HD), F32),
                           pltpu.VMEM((bsz, tb + SUBLANE, 3 * GDN_WIDTH), F32)]
    return _mixer_call(_gdn_kernel, x, w, params, GDN_WIDTH, scratch, tb, 2, "gdn")


def _stack_heads(x, head0):
    return jnp.concatenate([jnp.where(head0, x, 0.0), jnp.where(head0, 0.0, x)], axis=0)


def _rwkv_stream(xn_ref, w_ref, mu_ref, w0_ref, w2_ref, a0_ref, a2_ref, g2_ref, kk_ref, ka_ref,
                 rk_ref, lng_ref, lnb_ref, tri_ref, o_ref, pbuf_ref, st_ref, buf_ref, tb, first):
    buf_ref[SUBLANE:SUBLANE + tb, :] = pbuf_ref[...]
    nxt = _NextBlockProjection(xn_ref, w_ref, pbuf_ref)
    npair = RWKV_WIDTH // LANE
    nchunk = tb // CHUNK
    probs = [(pair, c) for c in range(nchunk) for pair in range(npair)]
    if first:
        nxt.spread(len(probs) // 4)
    p = buf_ref[SUBLANE:SUBLANE + tb, :]
    prev = buf_ref[SUBLANE - 1:SUBLANE - 1 + tb, :]
    buf_ref[0:SUBLANE, :] = p[tb - SUBLANE:tb]
    p = p + (prev - p) * mu_ref[...]
    wd = RWKV_WIDTH
    r_all = p[:, 0:wd]
    k_in = p[:, wd:2 * wd]
    v_all = p[:, 2 * wd:3 * wd]
    d_in = p[:, 3 * wd:3 * wd + LANE]
    a_in = p[:, 3 * wd + LANE:3 * wd + 2 * LANE]
    g_in = p[:, 3 * wd + 2 * LANE:3 * wd + 3 * LANE]
    lw_all = -jnp.exp(-_softplus(-(w0_ref[...] + _dot(jnp.tanh(d_in), w2_ref[...]))) - 0.5)
    a_all = _sigmoid(a0_ref[...] + _dot(a_in, a2_ref[...]))
    g_all = _dot(_sigmoid(g_in), g2_ref[...])
    kkraw_all = k_in * kk_ref[...]
    k_all = k_in * (1.0 + (a_all - 1.0) * ka_ref[...])
    cs_all = _cumsum_rows(tri_ref[...], lw_all)
    yield 4 * _ELEMENTWISE_UNIT

    row = lax.broadcasted_iota(jnp.int32, (CHUNK, LANE), 0)
    lane = lax.broadcasted_iota(jnp.int32, (CHUNK, LANE), 1)
    head0 = lane < RWKV_HD
    pos = lane % RWKV_HD
    incl, strict = row >= pos, row > pos
    eye = (row == pos).astype(F32)
    brow = lax.broadcasted_iota(jnp.int32, (LANE, LANE), 0) < RWKV_HD
    bcol = lax.broadcasted_iota(jnp.int32, (LANE, LANE), 1) < RWKV_HD
    same_head = brow == bcol
    mid = CHUNK // 2
    inv_hd = 1.0 / RWKV_HD
    stack = lambda x: _stack_heads(x.astype(BF16), head0)
    fold = lambda x: jnp.where(head0, x[0:CHUNK], x[CHUNK:LANE])

    def head_sum(x):
        s0 = jnp.sum(jnp.where(head0, x, 0.0), -1, keepdims=True)
        s1 = jnp.sum(jnp.where(head0, 0.0, x), -1, keepdims=True)
        return jnp.where(head0, s0, s1)

    pslice = lambda pair: slice(pair * LANE, (pair + 1) * LANE)
    cslice = lambda c: slice(c * CHUNK, (c + 1) * CHUNK)

    pre = []
    for n, (pair, c) in enumerate(probs):
        if first:
            nxt.spread(len(probs) - n)
        rs, ps = cslice(c), pslice(pair)
        r = r_all[rs, ps]
        k = k_all[rs, ps]
        v = v_all[rs, ps]
        lw = lw_all[rs, ps]
        cs = cs_all[rs, ps]
        c_mid = cs[mid:mid + 1]
        c_last = cs[CHUNK - 1:CHUNK]
        e_out = jnp.exp(c_mid - cs)
        e_mid = jnp.exp(c_mid)
        e_last_mid = jnp.exp(c_last - c_mid)
        kk = kkraw_all[rs, ps]
        kk = kk * lax.rsqrt(head_sum(kk * kk) + L2_EPS)
        r_s = r * jnp.exp(cs - c_mid)
        kk_s = kk * jnp.exp(cs - lw - c_mid)
        al_s = kk * a_all[rs, ps] * e_out
        k_s = k * e_out
        pre.append(dict(
            lhs=jnp.concatenate([kk_s, r_s], axis=0).astype(BF16),
            rhs=jnp.concatenate([stack(al_s), stack(k_s)], axis=0),
            v_st=stack(v), v=v.astype(BF16), kk2_st=stack(kk_s * e_mid),
            al2=(al_s * e_last_mid).astype(BF16), k2=(k_s * e_last_mid).astype(BF16),
            r2=r_s * e_mid, e_last=jnp.exp(c_last), rkr=r * k * rk_ref[:, ps]))
        yield _ELEMENTWISE_UNIT
    yield _PHASE_END

    grams, t_invs, bvs, tkws, zs, qfulls, x2s, akvs = [], [], [], [], [], [], [], []
    yield from _staged(grams, lambda d: _dot_nt(d["lhs"], d["rhs"]), [pre], 2.0)
    a_mats = [jnp.where(strict, g[0:CHUNK, 0:LANE], 0.0) for g in grams]
    yield from _unit_lower_inverse(t_invs, a_mats, eye, stack)
    yield from _staged(bvs, lambda g, d: _dot(jnp.where(strict, g[0:CHUNK, LANE:2 * LANE], 0.0), d["v_st"]),
                       [grams, pre], 1.0)
    yield from _staged(tkws, lambda t, d, bv: _dot(t, jnp.concatenate([d["kk2_st"], stack(bv)], axis=1)),
                       [t_invs, pre, bvs], 1.5)
    yield from _staged(zs, lambda tkw, d: _dot_tn(tkw, d["al2"]), [tkws, pre], 2.0)
    p_mats = [jnp.where(same_head, -z[0:LANE], 0.0).astype(BF16) for z in zs]
    yield from _staged(qfulls, lambda d: _dot_tn(d["v"], d["k2"]), [pre], 1.0)
    q_mats = [fold(qf) - fold(z[LANE:2 * LANE]) for qf, z in zip(qfulls, zs)]
    yield from _staged(
        x2s, lambda g, tkw: _dot(jnp.where(incl, g[CHUNK:2 * CHUNK, 0:LANE], 0.0),
                                 jnp.concatenate([stack(tkw[:, 0:LANE]), stack(tkw[:, LANE:2 * LANE])], axis=1)),
        [grams, tkws], 1.5)
    r_mats = [d["r2"] - x2[:, 0:LANE] for d, x2 in zip(pre, x2s)]
    yield from _staged(akvs, lambda g, d: _dot(jnp.where(incl, g[CHUNK:2 * CHUNK, LANE:2 * LANE], 0.0), d["v_st"]),
                       [grams, pre], 1.0)
    o_intras = [akv - x2[:, LANE:2 * LANE] for akv, x2 in zip(akvs, x2s)]
    yield _PHASE_END

    o_wide = []
    for i, (pair, c) in enumerate(probs):
        st = st_ref[pair]
        o_wide.append(_dot_nt(r_mats[i], stack(st)) + o_intras[i])
        st_ref[pair] = st * pre[i]["e_last"] + _dot(st, p_mats[i]) + q_mats[i]
        yield 2.0

    for i, (pair, c) in enumerate(probs):
        if not first:
            nxt.spread(len(probs) - i)
        rs, ps = cslice(c), pslice(pair)
        o = o_wide[i]
        d = o - head_sum(o) * inv_hd
        y = d * lax.rsqrt(head_sum(d * d) * inv_hd + RWKV_GN_EPS)
        bonus = head_sum(pre[i]["rkr"]) * v_all[rs, ps]
        o_ref[rs, ps] = ((y * lng_ref[:, ps] + lnb_ref[:, ps] + bonus) * g_all[rs, ps]).astype(o_ref.dtype)
        yield 0.4 * _ELEMENTWISE_UNIT
    nxt.flush()


def _rwkv_kernel(x0_ref, xn_ref, w_ref, mu_ref, w0_ref, w2_ref, a0_ref, a2_ref, g2_ref, kk_ref, ka_ref,
                 rk_ref, lng_ref, lnb_ref, tri_ref, o_ref, pbuf_ref, st_ref, buf_ref, *, tb):
    _mixer_prologue(x0_ref, w_ref, pbuf_ref, [st_ref, buf_ref])
    _run_staggered([_rwkv_stream(xn_ref.at[b, rows], w_ref, mu_ref, w0_ref, w2_ref, a0_ref, a2_ref, g2_ref, kk_ref,
                                 ka_ref, rk_ref, lng_ref, lnb_ref, tri_ref, o_ref.at[b, rows],
                                 pbuf_ref.at[b, rows], st_ref.at[b], buf_ref.at[b], tb, first=(j == 0))
                    for j, b, rows in _stream_blocks(o_ref, tb)])


def _rwkv(x, w, mu, w0, w2, a0, a2, g2, k_k, k_a, r_k, ln_g, ln_b, tb=256):
    params = [mu, w0, w2, a0, a2, g2, k_k, k_a, r_k, ln_g, ln_b, _block_tril(tb)]
    scratch = lambda bsz: [pltpu.VMEM((bsz, RWKV_WIDTH // LANE, CHUNK, LANE), F32),
                           pltpu.VMEM((bsz, tb + SUBLANE, w.shape[1]), F32)]
    return _mixer_call(_rwkv_kernel, x, w, params, RWKV_WIDTH, scratch, tb, 1, "rwkv")


def _merge_kernel(x_ref, wgate_ref, ogla_ref, ogdn_ref, orwkv_ref, wgla_ref, wgdn_ref, wrwkv_ref, wout_ref,
                  g_ref, b_ref, o_ref):
    d = D_MODEL
    x = x_ref[...]
    xb = x.astype(BF16)
    gate = lambda i: _sigmoid(jnp.dot(xb, wgate_ref[:, i * d:(i + 1) * d], preferred_element_type=F32))
    merged = (gate(0) * _dot(ogla_ref[...], wgla_ref[...])
              + gate(1) * _dot(ogdn_ref[...], wgdn_ref[...])
              + gate(2) * _dot(orwkv_ref[...], wrwkv_ref[...]))
    mix = _dot(merged, wout_ref[...])
    o_ref[...] = _layer_norm(DN_ALPHA * x + mix, g_ref[...], b_ref[...])


def _merge(x, wgate, ogla, ogdn, orwkv, wgla, wgdn, wrwkv, wout, g, b, layer, tm=512):
    t = x.shape[0]
    d = D_MODEL
    tok = lambda w: pl.BlockSpec((tm, w), lambda i: (i, 0))
    full = lambda shape: pl.BlockSpec(shape, lambda i: (0, 0))
    resident = lambda rows: pl.BlockSpec((None, rows, d), lambda i: (layer, 0, 0),
                                         pipeline_mode=pl.Buffered(1))
    return pl.pallas_call(
        _merge_kernel,
        grid=(t // tm,),
        in_specs=[tok(d),
                  pl.BlockSpec((d, N_BRANCH * d), lambda i: (0, 0), pipeline_mode=pl.Buffered(1)),
                  tok(GLA_DV), tok(GDN_WIDTH), tok(RWKV_WIDTH),
                  resident(GLA_DV), resident(GDN_WIDTH), resident(RWKV_WIDTH), resident(d),
                  full((1, d)), full((1, d))],
        out_specs=tok(d),
        out_shape=jax.ShapeDtypeStruct((t, d), F32),
        compiler_params=pltpu.CompilerParams(dimension_semantics=("parallel",),
                                             vmem_limit_bytes=VMEM_LIMIT),
        name="merge",
    )(x, wgate, ogla, ogdn, orwkv, wgla, wgdn, wrwkv, wout, g, b)


def _mlp_kernel(x_ref, wu_ref, wd_ref, g_ref, b_ref, o_ref, *, tf):
    x = x_ref[...]
    xb = x.astype(BF16)
    acc = None
    for j in range(D_FF // tf):
        hid = jnp.maximum(jnp.dot(xb, wu_ref[:, j * tf:(j + 1) * tf], preferred_element_type=F32), 0.0)
        part = jnp.dot((hid * hid).astype(BF16), wd_ref[j * tf:(j + 1) * tf, :], preferred_element_type=F32)
        acc = part if acc is None else acc + part
    o_ref[...] = _layer_norm(DN_ALPHA * x + acc, g_ref[...], b_ref[...])


def _mlp(x, wu, wd, g, b, layer, tm=512, tf=1024):
    t = x.shape[0]
    d = D_MODEL
    resident = lambda shape: pl.BlockSpec((None,) + shape, lambda i: (layer, 0, 0),
                                          pipeline_mode=pl.Buffered(1))
    return pl.pallas_call(
        functools.partial(_mlp_kernel, tf=tf),
        grid=(t // tm,),
        in_specs=[pl.BlockSpec((tm, d), lambda i: (i, 0)),
                  resident((d, D_FF)), resident((D_FF, d)),
                  pl.BlockSpec((1, d), lambda i: (0, 0)),
                  pl.BlockSpec((1, d), lambda i: (0, 0))],
        out_specs=pl.BlockSpec((tm, d), lambda i: (i, 0)),
        out_shape=jax.ShapeDtypeStruct((t, d), F32),
        compiler_params=pltpu.CompilerParams(dimension_semantics=("parallel",),
                                             vmem_limit_bytes=VMEM_LIMIT),
        name="mlp",
    )(x, wu, wd, g, b)


def _pad_rows(w, height):
    return jnp.pad(w, ((0, height - w.shape[0]), (0, 0)))


def _relayout_plans():
    o = _IN_OFFS
    s = 3 * RWKV_WIDTH
    gla = (2 * GLA_DK + 2 * GLA_DV + LANE,
           [(0, o[0], 2 * GLA_DK + GLA_DV), (2 * GLA_DK + GLA_DV, o[4], GLA_DV),
            (2 * GLA_DK + 2 * GLA_DV, o[3], GLA_GATE_RANK)])
    gdn = (4 * GDN_WIDTH + LANE,
           [(0, o[5], 3 * GDN_WIDTH), (3 * GDN_WIDTH, o[8], GDN_WIDTH), (4 * GDN_WIDTH, o[6], 2 * GDN_HEADS)])
    rwkv = (s + 3 * LANE,
            [(0, o[9], s), (s, o[9] + s, RWKV_DECAY_RANK), (s + LANE, o[9] + s + RWKV_DECAY_RANK, RWKV_A_RANK),
             (s + 2 * LANE, o[9] + s + RWKV_DECAY_RANK + RWKV_A_RANK, RWKV_GATE_RANK)])
    gates = (N_BRANCH * D_MODEL, [(0, o[10], N_BRANCH * D_MODEL)])
    return gla, gdn, rwkv, gates


def _relayout_kernel(w_ref, *out_refs):
    for out_ref, (width, pieces) in zip(out_refs, _relayout_plans()):
        covered = 0
        for dst, src, n in pieces:
            if dst > covered:
                out_ref[:, covered:dst] = jnp.zeros((out_ref.shape[0], dst - covered), out_ref.dtype)
            lo = src // LANE * LANE
            hi = min(-(-(src + n) // LANE) * LANE, w_ref.shape[1])
            x = w_ref[:, lo:hi]
            out_ref[:, dst:dst + n] = x[:, src - lo:src - lo + n].astype(out_ref.dtype)
            covered = dst + n
        if width > covered:
            out_ref[:, covered:width] = jnp.zeros((out_ref.shape[0], width - covered), out_ref.dtype)


def _split_w_in(w_in, layer, tk=128):
    depth, k, n_in = w_in.shape
    widths = [width for width, _ in _relayout_plans()]
    return pl.pallas_call(
        _relayout_kernel,
        grid=(k // tk,),
        in_specs=[pl.BlockSpec((tk, n_in), lambda i: (layer * (k // tk) + i, 0))],
        out_specs=[pl.BlockSpec((tk, width), lambda i: (i, 0)) for width in widths],
        out_shape=[jax.ShapeDtypeStruct((k, width), BF16) for width in widths],
        compiler_params=pltpu.CompilerParams(dimension_semantics=("parallel",),
                                             vmem_limit_bytes=VMEM_LIMIT),
        name="w_in_relayout",
    )(w_in.reshape(depth * k, n_in))


def _pad_mu(mu):
    s = 3 * RWKV_WIDTH
    z = jnp.zeros((LANE - RWKV_DECAY_RANK,), F32)
    return jnp.concatenate([mu[:s], mu[s:s + RWKV_DECAY_RANK], z,
                            mu[s + RWKV_DECAY_RANK:s + RWKV_DECAY_RANK + RWKV_A_RANK], z,
                            mu[s + RWKV_DECAY_RANK + RWKV_A_RANK:]])[None, :]


def kernel(x, w_in, gla_gate_up, gla_gate_bias, gla_norm_g, gdn_conv, gdn_a_log, gdn_dt_bias, gdn_norm_g, rwkv_mu, rwkv_w0, rwkv_w2, rwkv_a0, rwkv_a2, rwkv_g2, rwkv_k_k, rwkv_k_a, rwkv_r_k, rwkv_ln_g, rwkv_ln_b, w_br_gla, w_br_gdn, w_br_rwkv, w_out, ln1_g, ln1_b, w_up, w_down, ln2_g, ln2_b):
    bsz, seq, d = x.shape
    xt = x.reshape(bsz * seq, d)
    as_rows = lambda o: o.reshape(bsz * seq, o.shape[-1])
    row = lambda v: v[None, :]
    w_up_b = w_up.astype(BF16)
    w_down_b = w_down.astype(BF16)
    w_br_b = [w.astype(BF16) for w in (w_br_gla, w_br_gdn, w_br_rwkv)]
    w_out_b = w_out.astype(BF16)
    for l in range(DEPTH):
        w_gla, w_gdn, w_rwkv, w_gates = _split_w_in(w_in, l)
        xs = xt.reshape(bsz, seq, d)
        o_gla = _gla(xs, w_gla, _pad_rows(gla_gate_up[l], LANE).astype(BF16), row(gla_gate_bias[l]),
                     row(gla_norm_g[l]))
        o_gdn = _gdn(xs, w_gdn, gdn_conv[l], gdn_a_log[l], gdn_dt_bias[l], row(gdn_norm_g[l]))
        o_rwkv = _rwkv(
            xs, w_rwkv, _pad_mu(rwkv_mu[l]), row(rwkv_w0[l]), _pad_rows(rwkv_w2[l], LANE).astype(BF16),
            row(rwkv_a0[l]), _pad_rows(rwkv_a2[l], LANE).astype(BF16), rwkv_g2[l].astype(BF16),
            row(rwkv_k_k[l]), row(rwkv_k_a[l]), row(rwkv_r_k[l]), row(rwkv_ln_g[l]), row(rwkv_ln_b[l]))

        xt = _merge(xt, w_gates, as_rows(o_gla), as_rows(o_gdn), as_rows(o_rwkv), w_br_b[0], w_br_b[1], w_br_b[2],
                    w_out_b, row(ln1_g[l]), row(ln1_b[l]), l)
        xt = _mlp(xt, w_up_b, w_down_b, row(ln2_g[l]), row(ln2_b[l]), l)
    return xt.reshape(bsz, seq, d)
```

```python
import functools

import jax
import jax.numpy as jnp
from jax import lax
from jax.experimental import pallas as pl
from jax.experimental.pallas import tpu as pltpu

F32 = jnp.float32
BF16 = jnp.bfloat16

D_MODEL = 1024
DEPTH = 2
CHUNK = 64
GLA_HEADS = 4
GLA_DK = 512
GLA_DV = 1024
GLA_HK = 128
GLA_HV = 256
GLA_GATE_RANK = 16
GLA_GATE_TEMP = 16.0
GDN_HEADS = 4
GDN_HD = 128
GDN_WIDTH = 512
GDN_CONV = 4
RWKV_HD = 64
RWKV_WIDTH = 512
RWKV_HEADS = 8
RWKV_DECAY_RANK = 64
RWKV_A_RANK = 64
RWKV_GATE_RANK = 128
RWKV_GN_EPS = 64e-5
N_BRANCH = 3
D_FF = 4 * D_MODEL
DN_ALPHA = (2 * DEPTH) ** 0.25
LN_EPS = 1e-5
RMS_EPS = 1e-6
L2_EPS = 1e-6

LANE = 128
SUBLANE = 8
VMEM_LIMIT = 56 * 1024 * 1024
UNIT = 2 * CHUNK

_IN_WIDTHS = (GLA_DK, GLA_DK, GLA_DV, GLA_GATE_RANK, GLA_DV,
              3 * GDN_WIDTH, GDN_HEADS, GDN_HEADS, GDN_WIDTH,
              3 * RWKV_WIDTH + RWKV_DECAY_RANK + RWKV_A_RANK + RWKV_GATE_RANK,
              N_BRANCH * D_MODEL)
_IN_OFFS = [0]
for _w in _IN_WIDTHS:
    _IN_OFFS.append(_IN_OFFS[-1] + _w)


def _dot(a, b):
    return jnp.dot(a.astype(BF16), b.astype(BF16), preferred_element_type=F32)


def _dot_nt(a, b):
    return lax.dot_general(a.astype(BF16), b.astype(BF16), (((1,), (1,)), ((), ())),
                           preferred_element_type=F32)


def _dot_tn(a, b):
    return lax.dot_general(a.astype(BF16), b.astype(BF16), (((0,), (0,)), ((), ())),
                           preferred_element_type=F32)


def _split_bf16(x):
    hi = x.astype(BF16)
    return hi, (x - hi.astype(F32)).astype(BF16)


def _cumsum_rows(tri, x):
    hi, lo = _split_bf16(x)
    return (jnp.dot(tri, hi, preferred_element_type=F32)
            + jnp.dot(tri, lo, preferred_element_type=F32))


def _cumsum_lanes(x, tri):
    hi, lo = _split_bf16(x)
    dims = (((1,), (1,)), ((), ()))
    return (lax.dot_general(hi, tri, dims, preferred_element_type=F32)
            + lax.dot_general(lo, tri, dims, preferred_element_type=F32))


def _sigmoid(x):
    return 1.0 / (1.0 + jnp.exp(-x))


def _silu(x):
    return x * _sigmoid(x)


def _softplus(x):
    return jnp.maximum(x, 0.0) + jnp.log1p(jnp.exp(-jnp.abs(x)))


def _log_sigmoid(x):
    return -_softplus(-x)


def _chunk_masks(n):
    row = lax.broadcasted_iota(jnp.int32, (n, n), 0)
    col = lax.broadcasted_iota(jnp.int32, (n, n), 1)
    same = (row // CHUNK) == (col // CHUNK)
    return same & (row >= col), same & (row > col), row == col


def _per_chunk_rows(x, offset):
    w = x.shape[1]
    return jnp.concatenate(
        [jnp.broadcast_to(x[c * CHUNK + offset:c * CHUNK + offset + 1], (CHUNK, w))
         for c in range(UNIT // CHUNK)], axis=0)


def _unit_lower_inverse(out, a_list, eye, as_rhs=lambda p: p):
    n = eye.shape[0]
    ps = [-a for a in a_list]
    ts = [eye + p for p in ps]
    squares = []
    yield from _staged(squares, lambda p: _dot(p, as_rhs(p)), [ps], 1.0)
    ps = squares
    for _ in range(CHUNK.bit_length() - 3):
        prods = []
        yield from _staged(prods, lambda t, p: _dot(jnp.concatenate([t, p], axis=0), as_rhs(p)), [ts, ps], 2.0)
        ts = [t + pr[0:n] for t, pr in zip(ts, prods)]
        ps = [pr[n:2 * n] for pr in prods]
    yield from _staged(out, lambda t, p: t + _dot(t, as_rhs(p)), [ts, ps], 1.0)


def _layer_norm(y, g, b):
    mu = jnp.mean(y, -1, keepdims=True)
    d = y - mu
    var = jnp.mean(d * d, -1, keepdims=True)
    return d * lax.rsqrt(var + LN_EPS) * g + b


def _block_tril(n):
    idx = jnp.arange(n)
    same = (idx[:, None] // CHUNK) == (idx[None, :] // CHUNK)
    return (same & (idx[:, None] >= idx[None, :])).astype(BF16)


_PHASE_END = "phase-end"
_ELEMENTWISE_UNIT = 10.0


def _run_staggered(streams):
    spent = [0.0] * len(streams)
    finished = [False] * len(streams)
    slot = 0
    while not all(finished):
        running = [j for j in range(len(streams)) if j <= slot and not finished[j]]
        while running:
            j = min(running, key=lambda i: spent[i])
            try:
                cost = next(streams[j])
            except StopIteration:
                finished[j] = True
                running.remove(j)
                continue
            if cost == _PHASE_END:
                running.remove(j)
            else:
                spent[j] += cost
        top = max(spent)
        spent = [top] * len(streams)
        slot += 1


def _staged(out, fn, arg_lists, cost):
    for args in zip(*arg_lists):
        out.append(fn(*args))
        yield cost


class _NextBlockProjection:
    def __init__(self, xn_ref, w_ref, pbuf_ref, chunk=2 * LANE):
        self.xb = xn_ref[...].astype(BF16)
        self.w_ref = w_ref
        self.pbuf_ref = pbuf_ref
        width = w_ref.shape[1]
        self.bounds = [(lo, min(lo + chunk, width)) for lo in range(0, width, chunk)]

    def emit(self, count=1):
        for _ in range(min(count, len(self.bounds))):
            lo, hi = self.bounds.pop(0)
            self.pbuf_ref[:, lo:hi] = jnp.dot(self.xb, self.w_ref[:, lo:hi], preferred_element_type=F32)

    def spread(self, units_left):
        self.emit(-(-len(self.bounds) // max(units_left, 1)))

    def flush(self):
        self.emit(len(self.bounds))


def _mixer_call(kernel_fn, x, w, params, out_width, scratch, tb, blocks, name):
    bsz, seq, d = x.shape
    rows = blocks * tb
    nt = seq // rows
    width = w.shape[1]
    full = lambda a: pl.BlockSpec(a.shape, lambda t: (0,) * a.ndim)
    return pl.pallas_call(
        functools.partial(kernel_fn, tb=tb),
        grid=(nt,),
        in_specs=[pl.BlockSpec((bsz, rows, d), lambda t: (0, 0, 0), pipeline_mode=pl.Buffered(1)),
                  pl.BlockSpec((bsz, rows, d), lambda t: (0, jnp.minimum(t + 1, nt - 1), 0)),
                  pl.BlockSpec((d, width), lambda t: (0, 0), pipeline_mode=pl.Buffered(1))]
                 + [full(a) for a in params],
        out_specs=pl.BlockSpec((bsz, rows, out_width), lambda t: (0, t, 0)),
        out_shape=jax.ShapeDtypeStruct((bsz, seq, out_width), BF16),
        scratch_shapes=[pltpu.VMEM((bsz, rows, width), F32)] + scratch(bsz),
        compiler_params=pltpu.CompilerParams(dimension_semantics=("arbitrary",),
                                             vmem_limit_bytes=VMEM_LIMIT),
        name=name,
    )(x, x, w, *params)


def _stream_blocks(o_ref, tb):
    bsz, rows = o_ref.shape[0], o_ref.shape[1]
    return [(blk * bsz + b, b, pl.ds(blk * tb, tb)) for blk in range(rows // tb) for b in range(bsz)]


def _mixer_prologue(x0_ref, w_ref, pbuf_ref, zero_refs):
    @pl.when(pl.program_id(0) == 0)
    def _():
        for ref in zero_refs:
            ref[...] = jnp.zeros_like(ref)
        for b in range(x0_ref.shape[0]):
            pbuf_ref[b] = jnp.dot(x0_ref[b].astype(BF16), w_ref[...], preferred_element_type=F32)


def _gla_stream(xn_ref, w_ref, gup_ref, gb_ref, ng_ref, tri_ref, o_ref, pbuf_ref, st_ref, tb):
    dk, dv, hk, hv = GLA_DK, GLA_DV, GLA_HK, GLA_HV
    p_ref = pbuf_ref
    nxt = _NextBlockProjection(xn_ref, w_ref, pbuf_ref)
    incl, _, _ = _chunk_masks(UNIT)
    gd = p_ref[:, 2 * dk + 2 * dv:2 * dk + 2 * dv + LANE]
    log_a = _log_sigmoid(_dot(gd, gup_ref[...]) + gb_ref[...]) * (1.0 / GLA_GATE_TEMP)
    b_all = _cumsum_rows(tri_ref[...], log_a)
    mid = CHUNK // 2
    nchunk = UNIT // CHUNK
    probs = [(h, u) for u in range(tb // UNIT) for h in range(GLA_HEADS)]
    yield _ELEMENTWISE_UNIT

    pre = []
    for h, u in probs:
        rows = slice(u * UNIT, (u + 1) * UNIT)
        b = b_all[rows, h * hk:(h + 1) * hk]
        b_mid = _per_chunk_rows(b, mid)
        b_last = _per_chunk_rows(b, CHUNK - 1)
        qs = p_ref[rows, h * hk:(h + 1) * hk] * (hk ** -0.5) * jnp.exp(b - b_mid)
        ks = p_ref[rows, dk + h * hk:dk + (h + 1) * hk] * jnp.exp(b_mid - b)
        pre.append(dict(qs=qs, ks=ks, qe=qs * jnp.exp(b_mid), kd=ks * jnp.exp(b_last - b_mid),
                        e_last=jnp.exp(b_last),
                        v=p_ref[rows, 2 * dk + h * hv:2 * dk + (h + 1) * hv],
                        og=p_ref[rows, 2 * dk + dv + h * hv:2 * dk + dv + (h + 1) * hv]))
        yield _ELEMENTWISE_UNIT
    yield _PHASE_END

    attns, intras, kvs = [], [], []
    yield from _staged(attns, lambda d: jnp.where(incl, _dot_nt(d["qs"], d["ks"]), 0.0), [pre], 1.0)
    yield from _staged(intras, lambda a, d: _dot(a, d["v"]), [attns, pre], 2.0)
    yield from _staged(
        kvs, lambda d: [_dot_tn(d["v"][c * CHUNK:(c + 1) * CHUNK], d["kd"][c * CHUNK:(c + 1) * CHUNK])
                        for c in range(nchunk)], [pre], 2.0)
    yield _PHASE_END

    states = []
    for i, (h, u) in enumerate(probs):
        st = st_ref[h]
        per_chunk = []
        for c in range(nchunk):
            per_chunk.append(st)
            st = st * pre[i]["e_last"][c * CHUNK:c * CHUNK + 1] + kvs[i][c]
        st_ref[h] = st
        states.append(per_chunk)
        yield 0.3 * _ELEMENTWISE_UNIT
    units = len(probs) * nchunk
    for i, (h, u) in enumerate(probs):
        for c in range(nchunk):
            nxt.spread(units - (i * nchunk + c))
            rc = slice(c * CHUNK, (c + 1) * CHUNK)
            o = _dot_nt(pre[i]["qe"][rc], states[i][c]) + intras[i][rc]
            o = o * lax.rsqrt(jnp.mean(o * o, -1, keepdims=True) + RMS_EPS) * ng_ref[...]
            out_rows = slice(u * UNIT + c * CHUNK, u * UNIT + (c + 1) * CHUNK)
            o_ref[out_rows, h * hv:(h + 1) * hv] = (o * _silu(pre[i]["og"][rc])).astype(o_ref.dtype)
            yield 0.5 * _ELEMENTWISE_UNIT
    nxt.flush()


def _gla_kernel(x0_ref, xn_ref, w_ref, gup_ref, gb_ref, ng_ref, tri_ref, o_ref, pbuf_ref, st_ref, *, tb):
    _mixer_prologue(x0_ref, w_ref, pbuf_ref, [st_ref])
    _run_staggered([_gla_stream(xn_ref.at[b, rows], w_ref, gup_ref, gb_ref, ng_ref, tri_ref, o_ref.at[b, rows],
                                pbuf_ref.at[b, rows], st_ref.at[b], tb)
                    for _, b, rows in _stream_blocks(o_ref, tb)])


def _gla(x, w, gup, gb, ng, tb=256):
    scratch = lambda bsz: [pltpu.VMEM((bsz, GLA_HEADS, GLA_HV, GLA_HK), F32)]
    return _mixer_call(_gla_kernel, x, w, [gup, gb, ng, _block_tril(tb)], GLA_DV, scratch, tb, 2, "gla")


def _gdn_stream(xn_ref, w_ref, cw_ref, alog_row_ref, dtb_row_ref, alog_col_ref, dtb_col_ref, ng_ref, tri_ref,
                o_ref, pbuf_ref, st_ref, buf_ref, tb, first):
    wq = 3 * GDN_WIDTH
    hd = GDN_HD
    x = pbuf_ref[:, 0:wq]
    z_all = pbuf_ref[:, wq:wq + GDN_WIDTH]
    ab = pbuf_ref[:, wq + GDN_WIDTH:wq + GDN_WIDTH + LANE]
    nxt = _NextBlockProjection(xn_ref, w_ref, pbuf_ref)
    probs = [(h, u) for u in range(tb // UNIT) for h in range(GDN_HEADS)]
    n_groups = wq // (2 * LANE)
    early_units = n_groups + len(probs)

    buf_ref[SUBLANE:SUBLANE + tb, :] = x
    groups = []
    for g in range(n_groups):
        if first:
            nxt.spread(early_units - g)
        cols = slice(g * 2 * LANE, (g + 1) * 2 * LANE)
        y = x[:, cols] * cw_ref[GDN_CONV - 1:GDN_CONV, cols]
        for j in range(1, GDN_CONV):
            y = y + buf_ref[SUBLANE - j:SUBLANE - j + tb, cols] * cw_ref[GDN_CONV - 1 - j:GDN_CONV - j, cols]
        groups.append(_silu(y))
        yield _ELEMENTWISE_UNIT
    buf_ref[0:SUBLANE, :] = x[tb - SUBLANE:tb]
    qkv = jnp.concatenate(groups, axis=1)

    g_cols = -jnp.exp(alog_row_ref[...]) * _softplus(ab + dtb_row_ref[...])
    g_rows = -jnp.exp(alog_col_ref[...]) * _softplus(ab.T[0:SUBLANE] + dtb_col_ref[...])
    gam_cols = _cumsum_rows(tri_ref[...], g_cols)
    gam_rows = _cumsum_lanes(g_rows, tri_ref[...])
    beta_cols = _sigmoid(ab)
    yield 0.5 * _ELEMENTWISE_UNIT

    incl, strict, diag = _chunk_masks(UNIT)
    eye = diag.astype(F32)
    nchunk = UNIT // CHUNK

    pre = []
    for n, (h, u) in enumerate(probs):
        if first:
            nxt.spread(len(probs) - n)
        rows = slice(u * UNIT, (u + 1) * UNIT)
        q = qkv[rows, h * hd:(h + 1) * hd]
        k = qkv[rows, GDN_WIDTH + h * hd:GDN_WIDTH + (h + 1) * hd]
        v = qkv[rows, 2 * GDN_WIDTH + h * hd:2 * GDN_WIDTH + (h + 1) * hd]
        q = q * lax.rsqrt(jnp.sum(q * q, -1, keepdims=True) + L2_EPS) * (hd ** -0.5)
        k = k * lax.rsqrt(jnp.sum(k * k, -1, keepdims=True) + L2_EPS)
        gam = gam_cols[rows, h:h + 1]
        gam_r = gam_rows[h:h + 1, rows]
        beta = beta_cols[rows, GDN_HEADS + h:GDN_HEADS + h + 1]
        g_last = _per_chunk_rows(gam, CHUNK - 1)
        e_gam = jnp.exp(gam)
        kb = k * beta
        pre.append(dict(
            q=q, k=k, kb=kb, qe=q * e_gam,
            decay=jnp.where(incl, jnp.exp(jnp.minimum(gam - gam_r, 0.0)), 0.0),
            rhs=jnp.concatenate([v * beta, kb * e_gam], axis=1),
            kd=k * jnp.exp(g_last - gam), e_last=jnp.exp(g_last)))
        yield _ELEMENTWISE_UNIT
    yield _PHASE_END

    kqs, a_mats, t_invs, uws, ros, qps = [], [], [], [], [], []
    yield from _staged(kqs, lambda d: _dot_nt(jnp.concatenate([d["kb"], d["q"]], axis=0), d["k"]), [pre], 2.0)
    a_mats = [jnp.where(strict, kq[0:UNIT] * d["decay"], 0.0) for kq, d in zip(kqs, pre)]
    yield from _unit_lower_inverse(t_invs, a_mats, eye)
    yield from _staged(uws, lambda t, d: _dot(t, d["rhs"]), [t_invs, pre], 2.0)
    yield from _staged(ros, lambda kq, d, uw: _dot(kq[UNIT:2 * UNIT] * d["decay"], uw), [kqs, pre, uws], 2.0)
    r_mats = [d["qe"] - ro[:, hd:2 * hd] for d, ro in zip(pre, ros)]
    yield from _staged(
        qps, lambda d, uw: [_dot_tn(d["kd"][c * CHUNK:(c + 1) * CHUNK],
                                    jnp.concatenate([uw[c * CHUNK:(c + 1) * CHUNK, 0:hd],
                                                     -uw[c * CHUNK:(c + 1) * CHUNK, hd:2 * hd]], axis=1))
                            for c in range(nchunk)], [pre, uws], 4.0)
    yield _PHASE_END

    outs = {}
    for u in range(tb // UNIT):
        for c in range(nchunk):
            rc = slice(c * CHUNK, (c + 1) * CHUNK)
            for h in range(GDN_HEADS):
                i = u * GDN_HEADS + h
                st = st_ref[h]
                outs[(i, c)] = _dot(r_mats[i][rc], st) + ros[i][rc, 0:hd]
                qp = qps[i][c]
                st_ref[h] = st * pre[i]["e_last"][c * CHUNK:c * CHUNK + 1] + _dot(qp[:, hd:2 * hd], st) + qp[:, 0:hd]
                yield 2.0

    units = len(probs) * nchunk
    for i, (h, u) in enumerate(probs):
        for c in range(nchunk):
            if not first:
                nxt.spread(units - (i * nchunk + c))
            o = outs[(i, c)]
            o = o * lax.rsqrt(jnp.mean(o * o, -1, keepdims=True) + RMS_EPS) * ng_ref[...]
            out_rows = slice(u * UNIT + c * CHUNK, u * UNIT + (c + 1) * CHUNK)
            z = z_all[out_rows, h * hd:(h + 1) * hd]
            o_ref[out_rows, h * hd:(h + 1) * hd] = (o * _silu(z)).astype(o_ref.dtype)
            yield 0.3 * _ELEMENTWISE_UNIT
    nxt.flush()


def _gdn_kernel(x0_ref, xn_ref, w_ref, cw_ref, alog_row_ref, dtb_row_ref, alog_col_ref, dtb_col_ref, ng_ref,
                tri_ref, o_ref, pbuf_ref, st_ref, buf_ref, *, tb):
    _mixer_prologue(x0_ref, w_ref, pbuf_ref, [st_ref, buf_ref])
    _run_staggered([_gdn_stream(xn_ref.at[b, rows], w_ref, cw_ref, alog_row_ref, dtb_row_ref, alog_col_ref,
                                dtb_col_ref, ng_ref, tri_ref, o_ref.at[b, rows], pbuf_ref.at[b, rows],
                                st_ref.at[b], buf_ref.at[b], tb, first=(j == 0))
                    for j, b, rows in _stream_blocks(o_ref, tb)])


def _gdn(x, w, cw, alog, dtb, ng, tb=256):
    lane_row = lambda v: jnp.pad(v, (0, LANE - v.shape[0]))[None, :]
    sub_col = lambda v: jnp.pad(v, (0, SUBLANE - v.shape[0]))[:, None]
    params = [cw, lane_row(alog), lane_row(dtb), sub_col(alog), sub_col(dtb), ng, _block_tril(tb)]
    scratch = lambda bsz: [pltpu.VMEM((bsz, GDN_HEADS, GDN_HD, GDN_HD), F32),
                           pltpu.VMEM((bsz, tb + SUBLANE, 3 * GDN_WIDTH), F32)]
    return _mixer_call(_gdn_kernel, x, w, params, GDN_WIDTH, scratch, tb, 2, "gdn")


def _stack_heads(x, head0):
    return jnp.concatenate([jnp.where(head0, x, 0.0), jnp.where(head0, 0.0, x)], axis=0)


def _rwkv_stream(xn_ref, w_ref, mu_ref, w0_ref, w2_ref, a0_ref, a2_ref, g2_ref, kk_ref, ka_ref,
                 rk_ref, lng_ref, lnb_ref, tri_ref, o_ref, pbuf_ref, st_ref, buf_ref, tb, first):
    buf_ref[SUBLANE:SUBLANE + tb, :] = pbuf_ref[...]
    nxt = _NextBlockProjection(xn_ref, w_ref, pbuf_ref)
    npair = RWKV_WIDTH // LANE
    nchunk = tb // CHUNK
    probs = [(pair, c) for c in range(nchunk) for pair in range(npair)]
    if first:
        nxt.spread(len(probs) // 4)
    p = buf_ref[SUBLANE:SUBLANE + tb, :]
    prev = buf_ref[SUBLANE - 1:SUBLANE - 1 + tb, :]
    buf_ref[0:SUBLANE, :] = p[tb - SUBLANE:tb]
    p = p + (prev - p) * mu_ref[...]
    wd = RWKV_WIDTH
    r_all = p[:, 0:wd]
    k_in = p[:, wd:2 * wd]
    v_all = p[:, 2 * wd:3 * wd]
    d_in = p[:, 3 * wd:3 * wd + LANE]
    a_in = p[:, 3 * wd + LANE:3 * wd + 2 * LANE]
    g_in = p[:, 3 * wd + 2 * LANE:3 * wd + 3 * LANE]
    lw_all = -jnp.exp(-_softplus(-(w0_ref[...] + _dot(jnp.tanh(d_in), w2_ref[...]))) - 0.5)
    a_all = _sigmoid(a0_ref[...] + _dot(a_in, a2_ref[...]))
    g_all = _dot(_sigmoid(g_in), g2_ref[...])
    kkraw_all = k_in * kk_ref[...]
    k_all = k_in * (1.0 + (a_all - 1.0) * ka_ref[...])
    cs_all = _cumsum_rows(tri_ref[...], lw_all)
    yield 4 * _ELEMENTWISE_UNIT

    row = lax.broadcasted_iota(jnp.int32, (CHUNK, LANE), 0)
    lane = lax.broadcasted_iota(jnp.int32, (CHUNK, LANE), 1)
    head0 = lane < RWKV_HD
    pos = lane % RWKV_HD
    incl, strict = row >= pos, row > pos
    eye = (row == pos).astype(F32)
    brow = lax.broadcasted_iota(jnp.int32, (LANE, LANE), 0) < RWKV_HD
    bcol = lax.broadcasted_iota(jnp.int32, (LANE, LANE), 1) < RWKV_HD
    same_head = brow == bcol
    mid = CHUNK // 2
    inv_hd = 1.0 / RWKV_HD
    stack = lambda x: _stack_heads(x.astype(BF16), head0)
    fold = lambda x: jnp.where(head0, x[0:CHUNK], x[CHUNK:LANE])

    def head_sum(x):
        s0 = jnp.sum(jnp.where(head0, x, 0.0), -1, keepdims=True)
        s1 = jnp.sum(jnp.where(head0, 0.0, x), -1, keepdims=True)
        return jnp.where(head0, s0, s1)

    pslice = lambda pair: slice(pair * LANE, (pair + 1) * LANE)
    cslice = lambda c: slice(c * CHUNK, (c + 1) * CHUNK)

    pre = []
    for n, (pair, c) in enumerate(probs):
        if first:
            nxt.spread(len(probs) - n)
        rs, ps = cslice(c), pslice(pair)
        r = r_all[rs, ps]
        k = k_all[rs, ps]
        v = v_all[rs, ps]
        lw = lw_all[rs, ps]
        cs = cs_all[rs, ps]
        c_mid = cs[mid:mid + 1]
        c_last = cs[CHUNK - 1:CHUNK]
        e_out = jnp.exp(c_mid - cs)
        e_mid = jnp.exp(c_mid)
        e_last_mid = jnp.exp(c_last - c_mid)
        kk = kkraw_all[rs, ps]
        kk = kk * lax.rsqrt(head_sum(kk * kk) + L2_EPS)
        r_s = r * jnp.exp(cs - c_mid)
        kk_s = kk * jnp.exp(cs - lw - c_mid)
        al_s = kk * a_all[rs, ps] * e_out
        k_s = k * e_out
        pre.append(dict(
            lhs=jnp.concatenate([kk_s, r_s], axis=0).astype(BF16),
            rhs=jnp.concatenate([stack(al_s), stack(k_s)], axis=0),
            v_st=stack(v), v=v.astype(BF16), kk2_st=stack(kk_s * e_mid),
            al2=(al_s * e_last_mid).astype(BF16), k2=(k_s * e_last_mid).astype(BF16),
            r2=r_s * e_mid, e_last=jnp.exp(c_last), rkr=r * k * rk_ref[:, ps]))
        yield _ELEMENTWISE_UNIT
    yield _PHASE_END

    grams, t_invs, bvs, tkws, zs, qfulls, x2s, akvs = [], [], [], [], [], [], [], []
    yield from _staged(grams, lambda d: _dot_nt(d["lhs"], d["rhs"]), [pre], 2.0)
    a_mats = [jnp.where(strict, g[0:CHUNK, 0:LANE], 0.0) for g in grams]
    yield from _unit_lower_inverse(t_invs, a_mats, eye, stack)
    yield from _staged(bvs, lambda g, d: _dot(jnp.where(strict, g[0:CHUNK, LANE:2 * LANE], 0.0), d["v_st"]),
                       [grams, pre], 1.0)
    yield from _staged(tkws, lambda t, d, bv: _dot(t, jnp.concatenate([d["kk2_st"], stack(bv)], axis=1)),
                       [t_invs, pre, bvs], 1.5)
    yield from _staged(zs, lambda tkw, d: _dot_tn(tkw, d["al2"]), [tkws, pre], 2.0)
    p_mats = [jnp.where(same_head, -z[0:LANE], 0.0).astype(BF16) for z in zs]
    yield from _staged(qfulls, lambda d: _dot_tn(d["v"], d["k2"]), [pre], 1.0)
    q_mats = [fold(qf) - fold(z[LANE:2 * LANE]) for qf, z in zip(qfulls, zs)]
    yield from _staged(
        x2s, lambda g, tkw: _dot(jnp.where(incl, g[CHUNK:2 * CHUNK, 0:LANE], 0.0),
                                 jnp.concatenate([stack(tkw[:, 0:LANE]), stack(tkw[:, LANE:2 * LANE])], axis=1)),
        [grams, tkws], 1.5)
    r_mats = [d["r2"] - x2[:, 0:LANE] for d, x2 in zip(pre, x2s)]
    yield from _staged(akvs, lambda g, d: _dot(jnp.where(incl, g[CHUNK:2 * CHUNK, LANE:2 * LANE], 0.0), d["v_st"]),
                       [grams, pre], 1.0)
    o_intras = [akv - x2[:, LANE:2 * LANE] for akv, x2 in zip(akvs, x2s)]
    yield _PHASE_END

    o_wide = []
    for i, (pair, c) in enumerate(probs):
        st = st_ref[pair]
        o_wide.append(_dot_nt(r_mats[i], stack(st)) + o_intras[i])
        st_ref[pair] = st * pre[i]["e_last"] + _dot(st, p_mats[i]) + q_mats[i]
        yield 2.0

    for i, (pair, c) in enumerate(probs):
        if not first:
            nxt.spread(len(probs) - i)
        rs, ps = cslice(c), pslice(pair)
        o = o_wide[i]
        d = o - head_sum(o) * inv_hd
        y = d * lax.rsqrt(head_sum(d * d) * inv_hd + RWKV_GN_EPS)
        bonus = head_sum(pre[i]["rkr"]) * v_all[rs, ps]
        o_ref[rs, ps] = ((y * lng_ref[:, ps] + lnb_ref[:, ps] + bonus) * g_all[rs, ps]).astype(o_ref.dtype)
        yield 0.4 * _ELEMENTWISE_UNIT
    nxt.flush()


def _rwkv_kernel(x0_ref, xn_ref, w_ref, mu_ref, w0_ref, w2_ref, a0_ref, a2_ref, g2_ref, kk_ref, ka_ref,
                 rk_ref, lng_ref, lnb_ref, tri_ref, o_ref, pbuf_ref, st_ref, buf_ref, *, tb):
    _mixer_prologue(x0_ref, w_ref, pbuf_ref, [st_ref, buf_ref])
    _run_staggered([_rwkv_stream(xn_ref.at[b, rows], w_ref, mu_ref, w0_ref, w2_ref, a0_ref, a2_ref, g2_ref, kk_ref,
                                 ka_ref, rk_ref, lng_ref, lnb_ref, tri_ref, o_ref.at[b, rows],
                                 pbuf_ref.at[b, rows], st_ref.at[b], buf_ref.at[b], tb, first=(j == 0))
                    for j, b, rows in _stream_blocks(o_ref, tb)])


def _rwkv(x, w, mu, w0, w2, a0, a2, g2, k_k, k_a, r_k, ln_g, ln_b, tb=256):
    params = [mu, w0, w2, a0, a2, g2, k_k, k_a, r_k, ln_g, ln_b, _block_tril(tb)]
    scratch = lambda bsz: [pltpu.VMEM((bsz, RWKV_WIDTH // LANE, CHUNK, LANE), F32),
                           pltpu.VMEM((bsz, tb + SUBLANE, w.shape[1]), F32)]
    return _mixer_call(_rwkv_kernel, x, w, params, RWKV_WIDTH, scratch, tb, 1, "rwkv")


def _merge_kernel(x_ref, wgate_ref, ogla_ref, ogdn_ref, orwkv_ref, wgla_ref, wgdn_ref, wrwkv_ref, wout_ref,
                  g_ref, b_ref, o_ref):
    d = D_MODEL
    x = x_ref[...]
    xb = x.astype(BF16)
    gate = lambda i: _sigmoid(jnp.dot(xb, wgate_ref[:, i * d:(i + 1) * d], preferred_element_type=F32))
    merged = (gate(0) * _dot(ogla_ref[...], wgla_ref[...])
              + gate(1) * _dot(ogdn_ref[...], wgdn_ref[...])
              + gate(2) * _dot(orwkv_ref[...], wrwkv_ref[...]))
    mix = _dot(merged, wout_ref[...])
    o_ref[...] = _layer_norm(DN_ALPHA * x + mix, g_ref[...], b_ref[...])


def _merge(x, wgate, ogla, ogdn, orwkv, wgla, wgdn, wrwkv, wout, g, b, layer, tm=512):
    t = x.shape[0]
    d = D_MODEL
    tok = lambda w: pl.BlockSpec((tm, w), lambda i: (i, 0))
    full = lambda shape: pl.BlockSpec(shape, lambda i: (0, 0))
    resident = lambda rows: pl.BlockSpec((None, rows, d), lambda i: (layer, 0, 0),
                                         pipeline_mode=pl.Buffered(1))
    return pl.pallas_call(
        _merge_kernel,
        grid=(t // tm,),
        in_specs=[tok(d),
                  pl.BlockSpec((d, N_BRANCH * d), lambda i: (0, 0), pipeline_mode=pl.Buffered(1)),
                  tok(GLA_DV), tok(GDN_WIDTH), tok(RWKV_WIDTH),
                  resident(GLA_DV), resident(GDN_WIDTH), resident(RWKV_WIDTH), resident(d),
                  full((1, d)), full((1, d))],
        out_specs=tok(d),
        out_shape=jax.ShapeDtypeStruct((t, d), F32),
        compiler_params=pltpu.CompilerParams(dimension_semantics=("parallel",),
                                             vmem_limit_bytes=VMEM_LIMIT),
        name="merge",
    )(x, wgate, ogla, ogdn, orwkv, wgla, wgdn, wrwkv, wout, g, b)


def _mlp_kernel(x_ref, wu_ref, wd_ref, g_ref, b_ref, o_ref, *, tf):
    x = x_ref[...]
    xb = x.astype(BF16)
    acc = None
    for j in range(D_FF // tf):
        hid = jnp.maximum(jnp.dot(xb, wu_ref[:, j * tf:(j + 1) * tf], preferred_element_type=F32), 0.0)
        part = jnp.dot((hid * hid).astype(BF16), wd_ref[j * tf:(j + 1) * tf, :], preferred_element_type=F32)
        acc = part if acc is None else acc + part
    o_ref[...] = _layer_norm(DN_ALPHA * x + acc, g_ref[...], b_ref[...])


def _mlp(x, wu, wd, g, b, layer, tm=512, tf=1024):
    t = x.shape[0]
    d = D_MODEL
    resident = lambda shape: pl.BlockSpec((None,) + shape, lambda i: (layer, 0, 0),
                                          pipeline_mode=pl.Buffered(1))
    return pl.pallas_call(
        functools.partial(_mlp_kernel, tf=tf),
        grid=(t // tm,),
        in_specs=[pl.BlockSpec((tm, d), lambda i: (i, 0)),
                  resident((d, D_FF)), resident((D_FF, d)),
                  pl.BlockSpec((1, d), lambda i: (0, 0)),
                  pl.BlockSpec((1, d), lambda i: (0, 0))],
        out_specs=pl.BlockSpec((tm, d), lambda i: (i, 0)),
        out_shape=jax.ShapeDtypeStruct((t, d), F32),
        compiler_params=pltpu.CompilerParams(dimension_semantics=("parallel",),
                                             vmem_limit_bytes=VMEM_LIMIT),
        name="mlp",
    )(x, wu, wd, g, b)


def _pad_rows(w, height):
    return jnp.pad(w, ((0, height - w.shape[0]), (0, 0)))


def _relayout_plans():
    o = _IN_OFFS
    s = 3 * RWKV_WIDTH
    gla = (2 * GLA_DK + 2 * GLA_DV + LANE,
           [(0, o[0], 2 * GLA_DK + GLA_DV), (2 * GLA_DK + GLA_DV, o[4], GLA_DV),
            (2 * GLA_DK + 2 * GLA_DV, o[3], GLA_GATE_RANK)])
    gdn = (4 * GDN_WIDTH + LANE,
           [(0, o[5], 3 * GDN_WIDTH), (3 * GDN_WIDTH, o[8], GDN_WIDTH), (4 * GDN_WIDTH, o[6], 2 * GDN_HEADS)])
    rwkv = (s + 3 * LANE,
            [(0, o[9], s), (s, o[9] + s, RWKV_DECAY_RANK), (s + LANE, o[9] + s + RWKV_DECAY_RANK, RWKV_A_RANK),
             (s + 2 * LANE, o[9] + s + RWKV_DECAY_RANK + RWKV_A_RANK, RWKV_GATE_RANK)])
    gates = (N_BRANCH * D_MODEL, [(0, o[10], N_BRANCH * D_MODEL)])
    return gla, gdn, rwkv, gates


def _relayout_kernel(wt_ref, *out_refs):
    tk = wt_ref.shape[1]
    for out_ref, (width, pieces) in zip(out_refs, _relayout_plans()):
        covered = 0
        for dst, src, n in pieces:
            if dst > covered:
                out_ref[:, covered:dst] = jnp.zeros((tk, dst - covered), out_ref.dtype)
            for off in range(0, n, LANE):
                m = min(LANE, n - off)
                tile = wt_ref[src + off:src + off + LANE, :].T
                out_ref[:, dst + off:dst + off + m] = tile[:, 0:m].astype(out_ref.dtype)
            covered = dst + n
        if width > covered:
            out_ref[:, covered:width] = jnp.zeros((tk, width - covered), out_ref.dtype)


def _split_w_in(w_in_t, layer, tk=128):
    _, n_in, k = w_in_t.shape
    assert all(src % SUBLANE == 0 and (n % LANE == 0 or src + n // LANE * LANE + LANE <= n_in)
               for _, pieces in _relayout_plans() for _, src, n in pieces)
    widths = [width for width, _ in _relayout_plans()]
    return pl.pallas_call(
        _relayout_kernel,
        grid=(k // tk,),
        in_specs=[pl.BlockSpec((None, n_in, tk), lambda i: (layer, 0, i))],
        out_specs=[pl.BlockSpec((tk, width), lambda i: (i, 0)) for width in widths],
        out_shape=[jax.ShapeDtypeStruct((k, width), BF16) for width in widths],
        compiler_params=pltpu.CompilerParams(dimension_semantics=("parallel",),
                                             vmem_limit_bytes=VMEM_LIMIT),
        name="w_in_relayout",
    )(w_in_t)


def _pad_mu(mu):
    s = 3 * RWKV_WIDTH
    z = jnp.zeros((LANE - RWKV_DECAY_RANK,), F32)
    return jnp.concatenate([mu[:s], mu[s:s + RWKV_DECAY_RANK], z,
                            mu[s + RWKV_DECAY_RANK:s + RWKV_DECAY_RANK + RWKV_A_RANK], z,
                            mu[s + RWKV_DECAY_RANK + RWKV_A_RANK:]])[None, :]


def kernel(x, w_in, gla_gate_up, gla_gate_bias, gla_norm_g, gdn_conv, gdn_a_log, gdn_dt_bias, gdn_norm_g, rwkv_mu, rwkv_w0, rwkv_w2, rwkv_a0, rwkv_a2, rwkv_g2, rwkv_k_k, rwkv_k_a, rwkv_r_k, rwkv_ln_g, rwkv_ln_b, w_br_gla, w_br_gdn, w_br_rwkv, w_out, ln1_g, ln1_b, w_up, w_down, ln2_g, ln2_b):
    bsz, seq, d = x.shape
    xt = x.reshape(bsz * seq, d)
    as_rows = lambda o: o.reshape(bsz * seq, o.shape[-1])
    row = lambda v: v[None, :]
    w_up_b = w_up.astype(BF16)
    w_down_b = w_down.astype(BF16)
    w_br_b = [w.astype(BF16) for w in (w_br_gla, w_br_gdn, w_br_rwkv)]
    w_out_b = w_out.astype(BF16)
    w_in_t = jnp.swapaxes(w_in, 1, 2)
    for l in range(DEPTH):
        w_gla, w_gdn, w_rwkv, w_gates = _split_w_in(w_in_t, l)
        xs = xt.reshape(bsz, seq, d)
        o_gla = _gla(xs, w_gla, _pad_rows(gla_gate_up[l], LANE).astype(BF16), row(gla_gate_bias[l]),
                     row(gla_norm_g[l]))
        o_gdn = _gdn(xs, w_gdn, gdn_conv[l], gdn_a_log[l], gdn_dt_bias[l], row(gdn_norm_g[l]))
        o_rwkv = _rwkv(
            xs, w_rwkv, _pad_mu(rwkv_mu[l]), row(rwkv_w0[l]), _pad_rows(rwkv_w2[l], LANE).astype(BF16),
            row(rwkv_a0[l]), _pad_rows(rwkv_a2[l], LANE).astype(BF16), rwkv_g2[l].astype(BF16),
            row(rwkv_k_k[l]), row(rwkv_k_a[l]), row(rwkv_r_k[l]), row(rwkv_ln_g[l]), row(rwkv_ln_b[l]))

        xt = _merge(xt, w_gates, as_rows(o_gla), as_rows(o_gdn), as_rows(o_rwkv), w_br_b[0], w_br_b[1], w_br_b[2],
                    w_out_b, row(ln1_g[l]), row(ln1_b[l]), l)
        xt = _mlp(xt, w_up_b, w_down_b, row(ln2_g[l]), row(ln2_b[l]), l)
    return xt.reshape(bsz, seq, d)
```

```python
import functools

import jax
import jax.numpy as jnp
from jax import lax
from jax.experimental import pallas as pl
from jax.experimental.pallas import tpu as pltpu

F32 = jnp.float32
BF16 = jnp.bfloat16

D_MODEL = 1024
DEPTH = 2
CHUNK = 64
GLA_HEADS = 4
GLA_DK = 512
GLA_DV = 1024
GLA_HK = 128
GLA_HV = 256
GLA_GATE_RANK = 16
GLA_GATE_TEMP = 16.0
GDN_HEADS = 4
GDN_HD = 128
GDN_WIDTH = 512
GDN_CONV = 4
RWKV_HD = 64
RWKV_WIDTH = 512
RWKV_HEADS = 8
RWKV_DECAY_RANK = 64
RWKV_A_RANK = 64
RWKV_GATE_RANK = 128
RWKV_GN_EPS = 64e-5
N_BRANCH = 3
D_FF = 4 * D_MODEL
DN_ALPHA = (2 * DEPTH) ** 0.25
LN_EPS = 1e-5
RMS_EPS = 1e-6
L2_EPS = 1e-6

LANE = 128
SUBLANE = 8
VMEM_LIMIT = 56 * 1024 * 1024
UNIT = 2 * CHUNK

_IN_WIDTHS = (GLA_DK, GLA_DK, GLA_DV, GLA_GATE_RANK, GLA_DV,
              3 * GDN_WIDTH, GDN_HEADS, GDN_HEADS, GDN_WIDTH,
              3 * RWKV_WIDTH + RWKV_DECAY_RANK + RWKV_A_RANK + RWKV_GATE_RANK,
              N_BRANCH * D_MODEL)
_IN_OFFS = [0]
for _w in _IN_WIDTHS:
    _IN_OFFS.append(_IN_OFFS[-1] + _w)


def _dot(a, b):
    return jnp.dot(a.astype(BF16), b.astype(BF16), preferred_element_type=F32)


def _dot_nt(a, b):
    return lax.dot_general(a.astype(BF16), b.astype(BF16), (((1,), (1,)), ((), ())),
                           preferred_element_type=F32)


def _dot_tn(a, b):
    return lax.dot_general(a.astype(BF16), b.astype(BF16), (((0,), (0,)), ((), ())),
                           preferred_element_type=F32)


def _split_bf16(x):
    hi = x.astype(BF16)
    return hi, (x - hi.astype(F32)).astype(BF16)


def _cumsum_rows(tri, x):
    hi, lo = _split_bf16(x)
    return (jnp.dot(tri, hi, preferred_element_type=F32)
            + jnp.dot(tri, lo, preferred_element_type=F32))


def _cumsum_lanes(x, tri):
    hi, lo = _split_bf16(x)
    dims = (((1,), (1,)), ((), ()))
    return (lax.dot_general(hi, tri, dims, preferred_element_type=F32)
            + lax.dot_general(lo, tri, dims, preferred_element_type=F32))


def _sigmoid(x):
    return 1.0 / (1.0 + jnp.exp(-x))


def _silu(x):
    return x * _sigmoid(x)


def _softplus(x):
    return jnp.maximum(x, 0.0) + jnp.log1p(jnp.exp(-jnp.abs(x)))


def _log_sigmoid(x):
    return -_softplus(-x)


def _chunk_masks(n):
    row = lax.broadcasted_iota(jnp.int32, (n, n), 0)
    col = lax.broadcasted_iota(jnp.int32, (n, n), 1)
    same = (row // CHUNK) == (col // CHUNK)
    return same & (row >= col), same & (row > col), row == col


def _per_chunk_rows(x, offset):
    w = x.shape[1]
    return jnp.concatenate(
        [jnp.broadcast_to(x[c * CHUNK + offset:c * CHUNK + offset + 1], (CHUNK, w))
         for c in range(UNIT // CHUNK)], axis=0)


def _unit_lower_inverse(out, a_list, eye, as_rhs=lambda p: p):
    n = eye.shape[0]
    ps = [-a for a in a_list]
    ts = [eye + p for p in ps]
    squares = []
    yield from _staged(squares, lambda p: _dot(p, as_rhs(p)), [ps], 1.0)
    ps = squares
    for _ in range(CHUNK.bit_length() - 3):
        prods = []
        yield from _staged(prods, lambda t, p: _dot(jnp.concatenate([t, p], axis=0), as_rhs(p)), [ts, ps], 2.0)
        ts = [t + pr[0:n] for t, pr in zip(ts, prods)]
        ps = [pr[n:2 * n] for pr in prods]
    yield from _staged(out, lambda t, p: t + _dot(t, as_rhs(p)), [ts, ps], 1.0)


def _layer_norm(y, g, b):
    mu = jnp.mean(y, -1, keepdims=True)
    d = y - mu
    var = jnp.mean(d * d, -1, keepdims=True)
    return d * lax.rsqrt(var + LN_EPS) * g + b


def _block_tril(n):
    idx = jnp.arange(n)
    same = (idx[:, None] // CHUNK) == (idx[None, :] // CHUNK)
    return (same & (idx[:, None] >= idx[None, :])).astype(BF16)


_PHASE_END = "phase-end"
_ELEMENTWISE_UNIT = 10.0


def _run_staggered(streams):
    spent = [0.0] * len(streams)
    finished = [False] * len(streams)
    slot = 0
    while not all(finished):
        running = [j for j in range(len(streams)) if j <= slot and not finished[j]]
        while running:
            j = min(running, key=lambda i: spent[i])
            try:
                cost = next(streams[j])
            except StopIteration:
                finished[j] = True
                running.remove(j)
                continue
            if cost == _PHASE_END:
                running.remove(j)
            else:
                spent[j] += cost
        top = max(spent)
        spent = [top] * len(streams)
        slot += 1


def _staged(out, fn, arg_lists, cost):
    for args in zip(*arg_lists):
        out.append(fn(*args))
        yield cost


class _NextBlockProjection:
    def __init__(self, xn_ref, w_ref, pbuf_ref, chunk=2 * LANE):
        self.xb = xn_ref[...].astype(BF16)
        self.w_ref = w_ref
        self.pbuf_ref = pbuf_ref
        width = w_ref.shape[1]
        self.bounds = [(lo, min(lo + chunk, width)) for lo in range(0, width, chunk)]

    def emit(self, count=1):
        for _ in range(min(count, len(self.bounds))):
            lo, hi = self.bounds.pop(0)
            self.pbuf_ref[:, lo:hi] = jnp.dot(self.xb, self.w_ref[:, lo:hi], preferred_element_type=F32)

    def spread(self, units_left):
        self.emit(-(-len(self.bounds) // max(units_left, 1)))

    def flush(self):
        self.emit(len(self.bounds))


def _mixer_call(kernel_fn, x, w, params, out_width, scratch, tb, blocks, name):
    bsz, seq, d = x.shape
    rows = blocks * tb
    nt = seq // rows
    width = w.shape[1]
    full = lambda a: pl.BlockSpec(a.shape, lambda t: (0,) * a.ndim)
    return pl.pallas_call(
        functools.partial(kernel_fn, tb=tb),
        grid=(nt,),
        in_specs=[pl.BlockSpec((bsz, rows, d), lambda t: (0, 0, 0), pipeline_mode=pl.Buffered(1)),
                  pl.BlockSpec((bsz, rows, d), lambda t: (0, jnp.minimum(t + 1, nt - 1), 0)),
                  pl.BlockSpec((d, width), lambda t: (0, 0), pipeline_mode=pl.Buffered(1))]
                 + [full(a) for a in params],
        out_specs=pl.BlockSpec((bsz, rows, out_width), lambda t: (0, t, 0)),
        out_shape=jax.ShapeDtypeStruct((bsz, seq, out_width), BF16),
        scratch_shapes=[pltpu.VMEM((bsz, rows, width), F32)] + scratch(bsz),
        compiler_params=pltpu.CompilerParams(dimension_semantics=("arbitrary",),
                                             vmem_limit_bytes=VMEM_LIMIT),
        name=name,
    )(x, x, w, *params)


def _stream_blocks(o_ref, tb):
    bsz, rows = o_ref.shape[0], o_ref.shape[1]
    return [(blk * bsz + b, b, pl.ds(blk * tb, tb)) for blk in range(rows // tb) for b in range(bsz)]


def _mixer_prologue(x0_ref, w_ref, pbuf_ref, zero_refs):
    @pl.when(pl.program_id(0) == 0)
    def _():
        for ref in zero_refs:
            ref[...] = jnp.zeros_like(ref)
        for b in range(x0_ref.shape[0]):
            pbuf_ref[b] = jnp.dot(x0_ref[b].astype(BF16), w_ref[...], preferred_element_type=F32)


def _gla_stream(xn_ref, w_ref, gup_ref, gb_ref, ng_ref, tri_ref, o_ref, pbuf_ref, st_ref, tb):
    dk, dv, hk, hv = GLA_DK, GLA_DV, GLA_HK, GLA_HV
    p_ref = pbuf_ref
    nxt = _NextBlockProjection(xn_ref, w_ref, pbuf_ref)
    incl, _, _ = _chunk_masks(UNIT)
    gd = p_ref[:, 2 * dk + 2 * dv:2 * dk + 2 * dv + LANE]
    log_a = _log_sigmoid(_dot(gd, gup_ref[...]) + gb_ref[...]) * (1.0 / GLA_GATE_TEMP)
    b_all = _cumsum_rows(tri_ref[...], log_a)
    mid = CHUNK // 2
    nchunk = UNIT // CHUNK
    probs = [(h, u) for u in range(tb // UNIT) for h in range(GLA_HEADS)]
    yield _ELEMENTWISE_UNIT

    pre = []
    for h, u in probs:
        rows = slice(u * UNIT, (u + 1) * UNIT)
        b = b_all[rows, h * hk:(h + 1) * hk]
        b_mid = _per_chunk_rows(b, mid)
        b_last = _per_chunk_rows(b, CHUNK - 1)
        qs = p_ref[rows, h * hk:(h + 1) * hk] * (hk ** -0.5) * jnp.exp(b - b_mid)
        ks = p_ref[rows, dk + h * hk:dk + (h + 1) * hk] * jnp.exp(b_mid - b)
        pre.append(dict(qs=qs, ks=ks, qe=qs * jnp.exp(b_mid), kd=ks * jnp.exp(b_last - b_mid),
                        e_last=jnp.exp(b_last),
                        v=p_ref[rows, 2 * dk + h * hv:2 * dk + (h + 1) * hv],
                        og=p_ref[rows, 2 * dk + dv + h * hv:2 * dk + dv + (h + 1) * hv]))
        yield _ELEMENTWISE_UNIT
    yield _PHASE_END

    attns, intras, kvs = [], [], []
    yield from _staged(attns, lambda d: jnp.where(incl, _dot_nt(d["qs"], d["ks"]), 0.0), [pre], 1.0)
    yield from _staged(intras, lambda a, d: _dot(a, d["v"]), [attns, pre], 2.0)
    yield from _staged(
        kvs, lambda d: [_dot_tn(d["v"][c * CHUNK:(c + 1) * CHUNK], d["kd"][c * CHUNK:(c + 1) * CHUNK])
                        for c in range(nchunk)], [pre], 2.0)
    yield _PHASE_END

    states = []
    for i, (h, u) in enumerate(probs):
        st = st_ref[h]
        per_chunk = []
        for c in range(nchunk):
            per_chunk.append(st)
            st = st * pre[i]["e_last"][c * CHUNK:c * CHUNK + 1] + kvs[i][c]
        st_ref[h] = st
        states.append(per_chunk)
        yield 0.3 * _ELEMENTWISE_UNIT
    units = len(probs) * nchunk
    for i, (h, u) in enumerate(probs):
        for c in range(nchunk):
            nxt.spread(units - (i * nchunk + c))
            rc = slice(c * CHUNK, (c + 1) * CHUNK)
            o = _dot_nt(pre[i]["qe"][rc], states[i][c]) + intras[i][rc]
            o = o * lax.rsqrt(jnp.mean(o * o, -1, keepdims=True) + RMS_EPS) * ng_ref[...]
            out_rows = slice(u * UNIT + c * CHUNK, u * UNIT + (c + 1) * CHUNK)
            o_ref[out_rows, h * hv:(h + 1) * hv] = (o * _silu(pre[i]["og"][rc])).astype(o_ref.dtype)
            yield 0.5 * _ELEMENTWISE_UNIT
    nxt.flush()


def _gla_kernel(x0_ref, xn_ref, w_ref, gup_ref, gb_ref, ng_ref, tri_ref, o_ref, pbuf_ref, st_ref, *, tb):
    _mixer_prologue(x0_ref, w_ref, pbuf_ref, [st_ref])
    _run_staggered([_gla_stream(xn_ref.at[b, rows], w_ref, gup_ref, gb_ref, ng_ref, tri_ref, o_ref.at[b, rows],
                                pbuf_ref.at[b, rows], st_ref.at[b], tb)
                    for _, b, rows in _stream_blocks(o_ref, tb)])


def _gla(x, w, gup, gb, ng, tb=256):
    scratch = lambda bsz: [pltpu.VMEM((bsz, GLA_HEADS, GLA_HV, GLA_HK), F32)]
    return _mixer_call(_gla_kernel, x, w, [gup, gb, ng, _block_tril(tb)], GLA_DV, scratch, tb, 2, "gla")


def _gdn_stream(xn_ref, w_ref, cw_ref, alog_row_ref, dtb_row_ref, alog_col_ref, dtb_col_ref, ng_ref, tri_ref,
                o_ref, pbuf_ref, st_ref, buf_ref, tb, first):
    wq = 3 * GDN_WIDTH
    hd = GDN_HD
    x = pbuf_ref[:, 0:wq]
    z_all = pbuf_ref[:, wq:wq + GDN_WIDTH]
    ab = pbuf_ref[:, wq + GDN_WIDTH:wq + GDN_WIDTH + LANE]
    nxt = _NextBlockProjection(xn_ref, w_ref, pbuf_ref)
    probs = [(h, u) for u in range(tb // UNIT) for h in range(GDN_HEADS)]
    n_groups = wq // (2 * LANE)
    early_units = n_groups + len(probs)

    buf_ref[SUBLANE:SUBLANE + tb, :] = x
    groups = []
    for g in range(n_groups):
        if first:
            nxt.spread(early_units - g)
        cols = slice(g * 2 * LANE, (g + 1) * 2 * LANE)
        y = x[:, cols] * cw_ref[GDN_CONV - 1:GDN_CONV, cols]
        for j in range(1, GDN_CONV):
            y = y + buf_ref[SUBLANE - j:SUBLANE - j + tb, cols] * cw_ref[GDN_CONV - 1 - j:GDN_CONV - j, cols]
        groups.append(_silu(y))
        yield _ELEMENTWISE_UNIT
    buf_ref[0:SUBLANE, :] = x[tb - SUBLANE:tb]
    qkv = jnp.concatenate(groups, axis=1)

    g_cols = -jnp.exp(alog_row_ref[...]) * _softplus(ab + dtb_row_ref[...])
    g_rows = -jnp.exp(alog_col_ref[...]) * _softplus(ab.T[0:SUBLANE] + dtb_col_ref[...])
    gam_cols = _cumsum_rows(tri_ref[...], g_cols)
    gam_rows = _cumsum_lanes(g_rows, tri_ref[...])
    beta_cols = _sigmoid(ab)
    yield 0.5 * _ELEMENTWISE_UNIT

    incl, strict, diag = _chunk_masks(UNIT)
    eye = diag.astype(F32)
    nchunk = UNIT // CHUNK

    pre = []
    for n, (h, u) in enumerate(probs):
        if first:
            nxt.spread(len(probs) - n)
        rows = slice(u * UNIT, (u + 1) * UNIT)
        q = qkv[rows, h * hd:(h + 1) * hd]
        k = qkv[rows, GDN_WIDTH + h * hd:GDN_WIDTH + (h + 1) * hd]
        v = qkv[rows, 2 * GDN_WIDTH + h * hd:2 * GDN_WIDTH + (h + 1) * hd]
        q = q * lax.rsqrt(jnp.sum(q * q, -1, keepdims=True) + L2_EPS) * (hd ** -0.5)
        k = k * lax.rsqrt(jnp.sum(k * k, -1, keepdims=True) + L2_EPS)
        gam = gam_cols[rows, h:h + 1]
        gam_r = gam_rows[h:h + 1, rows]
        beta = beta_cols[rows, GDN_HEADS + h:GDN_HEADS + h + 1]
        g_last = _per_chunk_rows(gam, CHUNK - 1)
        e_gam = jnp.exp(gam)
        kb = k * beta
        pre.append(dict(
            q=q, k=k, kb=kb, qe=q * e_gam,
            decay=jnp.where(incl, jnp.exp(jnp.minimum(gam - gam_r, 0.0)), 0.0),
            rhs=jnp.concatenate([v * beta, kb * e_gam], axis=1),
            kd=k * jnp.exp(g_last - gam), e_last=jnp.exp(g_last)))
        yield _ELEMENTWISE_UNIT
    yield _PHASE_END

    kqs, a_mats, t_invs, uws, ros, qps = [], [], [], [], [], []
    yield from _staged(kqs, lambda d: _dot_nt(jnp.concatenate([d["kb"], d["q"]], axis=0), d["k"]), [pre], 2.0)
    a_mats = [jnp.where(strict, kq[0:UNIT] * d["decay"], 0.0) for kq, d in zip(kqs, pre)]
    yield from _unit_lower_inverse(t_invs, a_mats, eye)
    yield from _staged(uws, lambda t, d: _dot(t, d["rhs"]), [t_invs, pre], 2.0)
    yield from _staged(ros, lambda kq, d, uw: _dot(kq[UNIT:2 * UNIT] * d["decay"], uw), [kqs, pre, uws], 2.0)
    r_mats = [d["qe"] - ro[:, hd:2 * hd] for d, ro in zip(pre, ros)]
    yield from _staged(
        qps, lambda d, uw: [_dot_tn(d["kd"][c * CHUNK:(c + 1) * CHUNK],
                                    jnp.concatenate([uw[c * CHUNK:(c + 1) * CHUNK, 0:hd],
                                                     -uw[c * CHUNK:(c + 1) * CHUNK, hd:2 * hd]], axis=1))
                            for c in range(nchunk)], [pre, uws], 4.0)
    yield _PHASE_END

    outs = {}
    for u in range(tb // UNIT):
        for c in range(nchunk):
            rc = slice(c * CHUNK, (c + 1) * CHUNK)
            for h in range(GDN_HEADS):
                i = u * GDN_HEADS + h
                st = st_ref[h]
                outs[(i, c)] = _dot(r_mats[i][rc], st) + ros[i][rc, 0:hd]
                qp = qps[i][c]
                st_ref[h] = st * pre[i]["e_last"][c * CHUNK:c * CHUNK + 1] + _dot(qp[:, hd:2 * hd], st) + qp[:, 0:hd]
                yield 2.0

    units = len(probs) * nchunk
    for i, (h, u) in enumerate(probs):
        for c in range(nchunk):
            if not first:
                nxt.spread(units - (i * nchunk + c))
            o = outs[(i, c)]
            o = o * lax.rsqrt(jnp.mean(o * o, -1, keepdims=True) + RMS_EPS) * ng_ref[...]
            out_rows = slice(u * UNIT + c * CHUNK, u * UNIT + (c + 1) * CHUNK)
            z = z_all[out_rows, h * hd:(h + 1) * hd]
            o_ref[out_rows, h * hd:(h + 1) * hd] = (o * _silu(z)).astype(o_ref.dtype)
            yield 0.3 * _ELEMENTWISE_UNIT
    nxt.flush()


def _gdn_kernel(x0_ref, xn_ref, w_ref, cw_ref, alog_row_ref, dtb_row_ref, alog_col_ref, dtb_col_ref, ng_ref,
                tri_ref, o_ref, pbuf_ref, st_ref, buf_ref, *, tb):
    _mixer_prologue(x0_ref, w_ref, pbuf_ref, [st_ref, buf_ref])
    _run_staggered([_gdn_stream(xn_ref.at[b, rows], w_ref, cw_ref, alog_row_ref, dtb_row_ref, alog_col_ref,
                                dtb_col_ref, ng_ref, tri_ref, o_ref.at[b, rows], pbuf_ref.at[b, rows],
                                st_ref.at[b], buf_ref.at[b], tb, first=(j == 0))
                    for j, b, rows in _stream_blocks(o_ref, tb)])


def _gdn(x, w, cw, alog, dtb, ng, tb=256):
    lane_row = lambda v: jnp.pad(v, (0, LANE - v.shape[0]))[None, :]
    sub_col = lambda v: jnp.pad(v, (0, SUBLANE - v.shape[0]))[:, None]
    params = [cw, lane_row(alog), lane_row(dtb), sub_col(alog), sub_col(dtb), ng, _block_tril(tb)]
    scratch = lambda bsz: [pltpu.VMEM((bsz, GDN_HEADS, GDN_HD, GDN_HD), F32),
                           pltpu.VMEM((bsz, tb + SUBLANE, 3 * GDN_WIDTH), F32)]
    return _mixer_call(_gdn_kernel, x, w, params, GDN_WIDTH, scratch, tb, 2, "gdn")


def _stack_heads(x, head0):
    return jnp.concatenate([jnp.where(head0, x, 0.0), jnp.where(head0, 0.0, x)], axis=0)


def _rwkv_stream(xn_ref, w_ref, mu_ref, w0_ref, w2_ref, a0_ref, a2_ref, g2_ref, kk_ref, ka_ref,
                 rk_ref, lng_ref, lnb_ref, tri_ref, o_ref, pbuf_ref, st_ref, buf_ref, tb, first):
    buf_ref[SUBLANE:SUBLANE + tb, :] = pbuf_ref[...]
    nxt = _NextBlockProjection(xn_ref, w_ref, pbuf_ref)
    npair = RWKV_WIDTH // LANE
    nchunk = tb // CHUNK
    probs = [(pair, c) for c in range(nchunk) for pair in range(npair)]
    if first:
        nxt.spread(len(probs) // 4)
    p = buf_ref[SUBLANE:SUBLANE + tb, :]
    prev = buf_ref[SUBLANE - 1:SUBLANE - 1 + tb, :]
    buf_ref[0:SUBLANE, :] = p[tb - SUBLANE:tb]
    p = p + (prev - p) * mu_ref[...]
    wd = RWKV_WIDTH
    r_all = p[:, 0:wd]
    k_in = p[:, wd:2 * wd]
    v_all = p[:, 2 * wd:3 * wd]
    d_in = p[:, 3 * wd:3 * wd + LANE]
    a_in = p[:, 3 * wd + LANE:3 * wd + 2 * LANE]
    g_in = p[:, 3 * wd + 2 * LANE:3 * wd + 3 * LANE]
    lw_all = -jnp.exp(-_softplus(-(w0_ref[...] + _dot(jnp.tanh(d_in), w2_ref[...]))) - 0.5)
    a_all = _sigmoid(a0_ref[...] + _dot(a_in, a2_ref[...]))
    g_all = _dot(_sigmoid(g_in), g2_ref[...])
    kkraw_all = k_in * kk_ref[...]
    k_all = k_in * (1.0 + (a_all - 1.0) * ka_ref[...])
    cs_all = _cumsum_rows(tri_ref[...], lw_all)
    yield 4 * _ELEMENTWISE_UNIT

    row = lax.broadcasted_iota(jnp.int32, (CHUNK, LANE), 0)
    lane = lax.broadcasted_iota(jnp.int32, (CHUNK, LANE), 1)
    head0 = lane < RWKV_HD
    pos = lane % RWKV_HD
    incl, strict = row >= pos, row > pos
    eye = (row == pos).astype(F32)
    brow = lax.broadcasted_iota(jnp.int32, (LANE, LANE), 0) < RWKV_HD
    bcol = lax.broadcasted_iota(jnp.int32, (LANE, LANE), 1) < RWKV_HD
    same_head = brow == bcol
    mid = CHUNK // 2
    inv_hd = 1.0 / RWKV_HD
    stack = lambda x: _stack_heads(x.astype(BF16), head0)
    fold = lambda x: jnp.where(head0, x[0:CHUNK], x[CHUNK:LANE])

    def head_sum(x):
        s0 = jnp.sum(jnp.where(head0, x, 0.0), -1, keepdims=True)
        s1 = jnp.sum(jnp.where(head0, 0.0, x), -1, keepdims=True)
        return jnp.where(head0, s0, s1)

    pslice = lambda pair: slice(pair * LANE, (pair + 1) * LANE)
    cslice = lambda c: slice(c * CHUNK, (c + 1) * CHUNK)

    pre = []
    for n, (pair, c) in enumerate(probs):
        if first:
            nxt.spread(len(probs) - n)
        rs, ps = cslice(c), pslice(pair)
        r = r_all[rs, ps]
        k = k_all[rs, ps]
        v = v_all[rs, ps]
        lw = lw_all[rs, ps]
        cs = cs_all[rs, ps]
        c_mid = cs[mid:mid + 1]
        c_last = cs[CHUNK - 1:CHUNK]
        e_out = jnp.exp(c_mid - cs)
        e_mid = jnp.exp(c_mid)
        e_last_mid = jnp.exp(c_last - c_mid)
        kk = kkraw_all[rs, ps]
        kk = kk * lax.rsqrt(head_sum(kk * kk) + L2_EPS)
        r_s = r * jnp.exp(cs - c_mid)
        kk_s = kk * jnp.exp(cs - lw - c_mid)
        al_s = kk * a_all[rs, ps] * e_out
        k_s = k * e_out
        pre.append(dict(
            lhs=jnp.concatenate([kk_s, r_s], axis=0).astype(BF16),
            rhs=jnp.concatenate([stack(al_s), stack(k_s)], axis=0),
            v_st=stack(v), v=v.astype(BF16), kk2_st=stack(kk_s * e_mid),
            al2=(al_s * e_last_mid).astype(BF16), k2=(k_s * e_last_mid).astype(BF16),
            r2=r_s * e_mid, e_last=jnp.exp(c_last), rkr=r * k * rk_ref[:, ps]))
        yield _ELEMENTWISE_UNIT
    yield _PHASE_END

    grams, t_invs, bvs, tkws, zs, qfulls, x2s, akvs = [], [], [], [], [], [], [], []
    yield from _staged(grams, lambda d: _dot_nt(d["lhs"], d["rhs"]), [pre], 2.0)
    a_mats = [jnp.where(strict, g[0:CHUNK, 0:LANE], 0.0) for g in grams]
    yield from _unit_lower_inverse(t_invs, a_mats, eye, stack)
    yield from _staged(bvs, lambda g, d: _dot(jnp.where(strict, g[0:CHUNK, LANE:2 * LANE], 0.0), d["v_st"]),
                       [grams, pre], 1.0)
    yield from _staged(tkws, lambda t, d, bv: _dot(t, jnp.concatenate([d["kk2_st"], stack(bv)], axis=1)),
                       [t_invs, pre, bvs], 1.5)
    yield from _staged(zs, lambda tkw, d: _dot_tn(tkw, d["al2"]), [tkws, pre], 2.0)
    p_mats = [jnp.where(same_head, -z[0:LANE], 0.0).astype(BF16) for z in zs]
    yield from _staged(qfulls, lambda d: _dot_tn(d["v"], d["k2"]), [pre], 1.0)
    q_mats = [fold(qf) - fold(z[LANE:2 * LANE]) for qf, z in zip(qfulls, zs)]
    yield from _staged(
        x2s, lambda g, tkw: _dot(jnp.where(incl, g[CHUNK:2 * CHUNK, 0:LANE], 0.0),
                                 jnp.concatenate([stack(tkw[:, 0:LANE]), stack(tkw[:, LANE:2 * LANE])], axis=1)),
        [grams, tkws], 1.5)
    r_mats = [d["r2"] - x2[:, 0:LANE] for d, x2 in zip(pre, x2s)]
    yield from _staged(akvs, lambda g, d: _dot(jnp.where(incl, g[CHUNK:2 * CHUNK, LANE:2 * LANE], 0.0), d["v_st"]),
                       [grams, pre], 1.0)
    o_intras = [akv - x2[:, LANE:2 * LANE] for akv, x2 in zip(akvs, x2s)]
    yield _PHASE_END

    o_wide = []
    for i, (pair, c) in enumerate(probs):
        st = st_ref[pair]
        o_wide.append(_dot_nt(r_mats[i], stack(st)) + o_intras[i])
        st_ref[pair] = st * pre[i]["e_last"] + _dot(st, p_mats[i]) + q_mats[i]
        yield 2.0

    for i, (pair, c) in enumerate(probs):
        if not first:
            nxt.spread(len(probs) - i)
        rs, ps = cslice(c), pslice(pair)
        o = o_wide[i]
        d = o - head_sum(o) * inv_hd
        y = d * lax.rsqrt(head_sum(d * d) * inv_hd + RWKV_GN_EPS)
        bonus = head_sum(pre[i]["rkr"]) * v_all[rs, ps]
        o_ref[rs, ps] = ((y * lng_ref[:, ps] + lnb_ref[:, ps] + bonus) * g_all[rs, ps]).astype(o_ref.dtype)
        yield 0.4 * _ELEMENTWISE_UNIT
    nxt.flush()


def _rwkv_kernel(x0_ref, xn_ref, w_ref, mu_ref, w0_ref, w2_ref, a0_ref, a2_ref, g2_ref, kk_ref, ka_ref,
                 rk_ref, lng_ref, lnb_ref, tri_ref, o_ref, pbuf_ref, st_ref, buf_ref, *, tb):
    _mixer_prologue(x0_ref, w_ref, pbuf_ref, [st_ref, buf_ref])
    _run_staggered([_rwkv_stream(xn_ref.at[b, rows], w_ref, mu_ref, w0_ref, w2_ref, a0_ref, a2_ref, g2_ref, kk_ref,
                                 ka_ref, rk_ref, lng_ref, lnb_ref, tri_ref, o_ref.at[b, rows],
                                 pbuf_ref.at[b, rows], st_ref.at[b], buf_ref.at[b], tb, first=(j == 0))
                    for j, b, rows in _stream_blocks(o_ref, tb)])


def _rwkv(x, w, mu, w0, w2, a0, a2, g2, k_k, k_a, r_k, ln_g, ln_b, tb=256):
    params = [mu, w0, w2, a0, a2, g2, k_k, k_a, r_k, ln_g, ln_b, _block_tril(tb)]
    scratch = lambda bsz: [pltpu.VMEM((bsz, RWKV_WIDTH // LANE, CHUNK, LANE), F32),
                           pltpu.VMEM((bsz, tb + SUBLANE, w.shape[1]), F32)]
    return _mixer_call(_rwkv_kernel, x, w, params, RWKV_WIDTH, scratch, tb, 1, "rwkv")


def _merge_kernel(x_ref, wgate_ref, ogla_ref, ogdn_ref, orwkv_ref, wgla_ref, wgdn_ref, wrwkv_ref, wout_ref,
                  g_ref, b_ref, o_ref):
    d = D_MODEL
    x = x_ref[...]
    xb = x.astype(BF16)
    gate = lambda i: _sigmoid(jnp.dot(xb, wgate_ref[:, i * d:(i + 1) * d], preferred_element_type=F32))
    merged = (gate(0) * _dot(ogla_ref[...], wgla_ref[...])
              + gate(1) * _dot(ogdn_ref[...], wgdn_ref[...])
              + gate(2) * _dot(orwkv_ref[...], wrwkv_ref[...]))
    mix = _dot(merged, wout_ref[...])
    o_ref[...] = _layer_norm(DN_ALPHA * x + mix, g_ref[...], b_ref[...])


def _merge(x, wgate, ogla, ogdn, orwkv, wgla, wgdn, wrwkv, wout, g, b, layer, tm=1024):
    t = x.shape[0]
    d = D_MODEL
    tok = lambda w: pl.BlockSpec((tm, w), lambda i: (i, 0))
    full = lambda shape: pl.BlockSpec(shape, lambda i: (0, 0))
    resident = lambda rows: pl.BlockSpec((None, rows, d), lambda i: (layer, 0, 0),
                                         pipeline_mode=pl.Buffered(1))
    return pl.pallas_call(
        _merge_kernel,
        grid=(t // tm,),
        in_specs=[tok(d),
                  pl.BlockSpec((d, N_BRANCH * d), lambda i: (0, 0), pipeline_mode=pl.Buffered(1)),
                  tok(GLA_DV), tok(GDN_WIDTH), tok(RWKV_WIDTH),
                  resident(GLA_DV), resident(GDN_WIDTH), resident(RWKV_WIDTH), resident(d),
                  full((1, d)), full((1, d))],
        out_specs=tok(d),
        out_shape=jax.ShapeDtypeStruct((t, d), F32),
        compiler_params=pltpu.CompilerParams(dimension_semantics=("parallel",),
                                             vmem_limit_bytes=VMEM_LIMIT),
        name="merge",
    )(x, wgate, ogla, ogdn, orwkv, wgla, wgdn, wrwkv, wout, g, b)


def _mlp_kernel(x_ref, wu_ref, wd_ref, g_ref, b_ref, o_ref, *, tf):
    x = x_ref[...]
    xb = x.astype(BF16)
    acc = None
    for j in range(D_FF // tf):
        hid = jnp.maximum(jnp.dot(xb, wu_ref[:, j * tf:(j + 1) * tf], preferred_element_type=F32), 0.0)
        part = jnp.dot((hid * hid).astype(BF16), wd_ref[j * tf:(j + 1) * tf, :], preferred_element_type=F32)
        acc = part if acc is None else acc + part
    o_ref[...] = _layer_norm(DN_ALPHA * x + acc, g_ref[...], b_ref[...])


def _mlp(x, wu, wd, g, b, layer, tm=1024, tf=1024):
    t = x.shape[0]
    d = D_MODEL
    resident = lambda shape: pl.BlockSpec((None,) + shape, lambda i: (layer, 0, 0),
                                          pipeline_mode=pl.Buffered(1))
    return pl.pallas_call(
        functools.partial(_mlp_kernel, tf=tf),
        grid=(t // tm,),
        in_specs=[pl.BlockSpec((tm, d), lambda i: (i, 0)),
                  resident((d, D_FF)), resident((D_FF, d)),
                  pl.BlockSpec((1, d), lambda i: (0, 0)),
                  pl.BlockSpec((1, d), lambda i: (0, 0))],
        out_specs=pl.BlockSpec((tm, d), lambda i: (i, 0)),
        out_shape=jax.ShapeDtypeStruct((t, d), F32),
        compiler_params=pltpu.CompilerParams(dimension_semantics=("parallel",),
                                             vmem_limit_bytes=VMEM_LIMIT),
        name="mlp",
    )(x, wu, wd, g, b)


def _pad_rows(w, height):
    return jnp.pad(w, ((0, height - w.shape[0]), (0, 0)))


def _relayout_plans():
    o = _IN_OFFS
    s = 3 * RWKV_WIDTH
    gla = (2 * GLA_DK + 2 * GLA_DV + LANE,
           [(0, o[0], 2 * GLA_DK + GLA_DV), (2 * GLA_DK + GLA_DV, o[4], GLA_DV),
            (2 * GLA_DK + 2 * GLA_DV, o[3], GLA_GATE_RANK)])
    gdn = (4 * GDN_WIDTH + LANE,
           [(0, o[5], 3 * GDN_WIDTH), (3 * GDN_WIDTH, o[8], GDN_WIDTH), (4 * GDN_WIDTH, o[6], 2 * GDN_HEADS)])
    rwkv = (s + 3 * LANE,
            [(0, o[9], s), (s, o[9] + s, RWKV_DECAY_RANK), (s + LANE, o[9] + s + RWKV_DECAY_RANK, RWKV_A_RANK),
             (s + 2 * LANE, o[9] + s + RWKV_DECAY_RANK + RWKV_A_RANK, RWKV_GATE_RANK)])
    gates = (N_BRANCH * D_MODEL, [(0, o[10], N_BRANCH * D_MODEL)])
    return gla, gdn, rwkv, gates


def _relayout_kernel(wt_ref, *out_refs):
    tk = wt_ref.shape[1]
    for out_ref, (width, pieces) in zip(out_refs, _relayout_plans()):
        covered = 0
        for dst, src, n in pieces:
            if dst > covered:
                out_ref[:, covered:dst] = jnp.zeros((tk, dst - covered), out_ref.dtype)
            for off in range(0, n, LANE):
                m = min(LANE, n - off)
                tile = wt_ref[src + off:src + off + LANE, :].T
                out_ref[:, dst + off:dst + off + m] = tile[:, 0:m].astype(out_ref.dtype)
            covered = dst + n
        if width > covered:
            out_ref[:, covered:width] = jnp.zeros((tk, width - covered), out_ref.dtype)


def _split_w_in(w_in_t, layer, tk=128):
    _, n_in, k = w_in_t.shape
    assert all(src % SUBLANE == 0 and (n % LANE == 0 or src + n // LANE * LANE + LANE <= n_in)
               for _, pieces in _relayout_plans() for _, src, n in pieces)
    widths = [width for width, _ in _relayout_plans()]
    return pl.pallas_call(
        _relayout_kernel,
        grid=(k // tk,),
        in_specs=[pl.BlockSpec((None, n_in, tk), lambda i: (layer, 0, i))],
        out_specs=[pl.BlockSpec((tk, width), lambda i: (i, 0)) for width in widths],
        out_shape=[jax.ShapeDtypeStruct((k, width), BF16) for width in widths],
        compiler_params=pltpu.CompilerParams(dimension_semantics=("parallel",),
                                             vmem_limit_bytes=VMEM_LIMIT),
        name="w_in_relayout",
    )(w_in_t)


def _pad_mu(mu):
    s = 3 * RWKV_WIDTH
    z = jnp.zeros((LANE - RWKV_DECAY_RANK,), F32)
    return jnp.concatenate([mu[:s], mu[s:s + RWKV_DECAY_RANK], z,
                            mu[s + RWKV_DECAY_RANK:s + RWKV_DECAY_RANK + RWKV_A_RANK], z,
                            mu[s + RWKV_DECAY_RANK + RWKV_A_RANK:]])[None, :]


def kernel(x, w_in, gla_gate_up, gla_gate_bias, gla_norm_g, gdn_conv, gdn_a_log, gdn_dt_bias, gdn_norm_g, rwkv_mu, rwkv_w0, rwkv_w2, rwkv_a0, rwkv_a2, rwkv_g2, rwkv_k_k, rwkv_k_a, rwkv_r_k, rwkv_ln_g, rwkv_ln_b, w_br_gla, w_br_gdn, w_br_rwkv, w_out, ln1_g, ln1_b, w_up, w_down, ln2_g, ln2_b):
    bsz, seq, d = x.shape
    xt = x.reshape(bsz * seq, d)
    as_rows = lambda o: o.reshape(bsz * seq, o.shape[-1])
    row = lambda v: v[None, :]
    w_up_b = w_up.astype(BF16)
    w_down_b = w_down.astype(BF16)
    w_br_b = [w.astype(BF16) for w in (w_br_gla, w_br_gdn, w_br_rwkv)]
    w_out_b = w_out.astype(BF16)
    w_in_t = jnp.swapaxes(w_in, 1, 2)
    for l in range(DEPTH):
        w_gla, w_gdn, w_rwkv, w_gates = _split_w_in(w_in_t, l)
        xs = xt.reshape(bsz, seq, d)
        o_gla = _gla(xs, w_gla, _pad_rows(gla_gate_up[l], LANE).astype(BF16), row(gla_gate_bias[l]),
                     row(gla_norm_g[l]))
        o_gdn = _gdn(xs, w_gdn, gdn_conv[l], gdn_a_log[l], gdn_dt_bias[l], row(gdn_norm_g[l]))
        o_rwkv = _rwkv(
            xs, w_rwkv, _pad_mu(rwkv_mu[l]), row(rwkv_w0[l]), _pad_rows(rwkv_w2[l], LANE).astype(BF16),
            row(rwkv_a0[l]), _pad_rows(rwkv_a2[l], LANE).astype(BF16), rwkv_g2[l].astype(BF16),
            row(rwkv_k_k[l]), row(rwkv_k_a[l]), row(rwkv_r_k[l]), row(rwkv_ln_g[l]), row(rwkv_ln_b[l]))

        xt = _merge(xt, w_gates, as_rows(o_gla), as_rows(o_gdn), as_rows(o_rwkv), w_br_b[0], w_br_b[1], w_br_b[2],
                    w_out_b, row(ln1_g[l]), row(ln1_b[l]), l)
        xt = _mlp(xt, w_up_b, w_down_b, row(ln2_g[l]), row(ln2_b[l]), l)
    return xt.reshape(bsz, seq, d)
```

```python
import functools

import jax
import jax.numpy as jnp
from jax import lax
from jax.experimental import pallas as pl
from jax.experimental.pallas import tpu as pltpu

F32 = jnp.float32
BF16 = jnp.bfloat16

D_MODEL = 1024
DEPTH = 2
CHUNK = 64
GLA_HEADS = 4
GLA_DK = 512
GLA_DV = 1024
GLA_HK = 128
GLA_HV = 256
GLA_GATE_RANK = 16
GLA_GATE_TEMP = 16.0
GDN_HEADS = 4
GDN_HD = 128
GDN_WIDTH = 512
GDN_CONV = 4
RWKV_HD = 64
RWKV_WIDTH = 512
RWKV_DECAY_RANK = 64
RWKV_A_RANK = 64
RWKV_GATE_RANK = 128
RWKV_GN_EPS = 64e-5
N_BRANCH = 3
D_FF = 4 * D_MODEL
DN_ALPHA = (2 * DEPTH) ** 0.25
LN_EPS = 1e-5
RMS_EPS = 1e-6
L2_EPS = 1e-6

LANE = 128
SUBLANE = 8
MXU_COLS = 256
VMEM_LIMIT = 56 * 1024 * 1024
UNIT = 2 * CHUNK

_IN_WIDTHS = (GLA_DK, GLA_DK, GLA_DV, GLA_GATE_RANK, GLA_DV,
              3 * GDN_WIDTH, GDN_HEADS, GDN_HEADS, GDN_WIDTH,
              3 * RWKV_WIDTH + RWKV_DECAY_RANK + RWKV_A_RANK + RWKV_GATE_RANK,
              N_BRANCH * D_MODEL)
_IN_OFFS = [0]
for _w in _IN_WIDTHS:
    _IN_OFFS.append(_IN_OFFS[-1] + _w)


def _dot(a, b):
    return jnp.dot(a.astype(BF16), b.astype(BF16), preferred_element_type=F32)


def _dot_nt(a, b):
    return lax.dot_general(a.astype(BF16), b.astype(BF16), (((1,), (1,)), ((), ())),
                           preferred_element_type=F32)


def _dot_tn(a, b):
    return lax.dot_general(a.astype(BF16), b.astype(BF16), (((0,), (0,)), ((), ())),
                           preferred_element_type=F32)


def _split_bf16(x):
    hi = x.astype(BF16)
    return hi, (x - hi.astype(F32)).astype(BF16)


def _cumsum_rows(tri, x):
    hi, lo = _split_bf16(x)
    return (jnp.dot(tri, hi, preferred_element_type=F32)
            + jnp.dot(tri, lo, preferred_element_type=F32))


def _cumsum_lanes(x, tri):
    hi, lo = _split_bf16(x)
    dims = (((1,), (1,)), ((), ()))
    return (lax.dot_general(hi, tri, dims, preferred_element_type=F32)
            + lax.dot_general(lo, tri, dims, preferred_element_type=F32))


def _sigmoid(x):
    return 1.0 / (1.0 + jnp.exp(-x))


def _silu(x):
    return x * _sigmoid(x)


def _softplus(x):
    return jnp.maximum(x, 0.0) + jnp.log1p(jnp.exp(-jnp.abs(x)))


def _log_sigmoid(x):
    return -_softplus(-x)


def _chunk_masks(n):
    row = lax.broadcasted_iota(jnp.int32, (n, n), 0)
    col = lax.broadcasted_iota(jnp.int32, (n, n), 1)
    same = (row // CHUNK) == (col // CHUNK)
    return same & (row >= col), same & (row > col), row == col


def _per_chunk_rows(x, offset):
    w = x.shape[1]
    return jnp.concatenate(
        [jnp.broadcast_to(x[c * CHUNK + offset:c * CHUNK + offset + 1], (CHUNK, w))
         for c in range(UNIT // CHUNK)], axis=0)


def _unit_lower_inverse(out, a_list, eye, as_rhs=lambda p: p):
    n = eye.shape[0]
    ps = [-a for a in a_list]
    ts = [eye + p for p in ps]
    squares = []
    yield from _staged(squares, lambda p: _dot(p, as_rhs(p)), [ps], 1.0)
    ps = squares
    for _ in range(CHUNK.bit_length() - 3):
        prods = []
        yield from _staged(prods, lambda t, p: _dot(jnp.concatenate([t, p], axis=0), as_rhs(p)), [ts, ps], 2.0)
        ts = [t + pr[0:n] for t, pr in zip(ts, prods)]
        ps = [pr[n:2 * n] for pr in prods]
    yield from _staged(out, lambda t, p: t + _dot(t, as_rhs(p)), [ts, ps], 1.0)


def _layer_norm(y, g, b):
    mu = jnp.mean(y, -1, keepdims=True)
    d = y - mu
    var = jnp.mean(d * d, -1, keepdims=True)
    return d * lax.rsqrt(var + LN_EPS) * g + b


def _block_tril(n):
    idx = jnp.arange(n)
    same = (idx[:, None] // CHUNK) == (idx[None, :] // CHUNK)
    return (same & (idx[:, None] >= idx[None, :])).astype(BF16)


_PHASE_END = "phase-end"
_ELEMENTWISE_UNIT = 5.0


def _run_staggered(streams):
    spent = [0.0] * len(streams)
    finished = [False] * len(streams)
    slot = 0
    while not all(finished):
        running = [j for j in range(len(streams)) if j <= slot and not finished[j]]
        while running:
            j = min(running, key=lambda i: spent[i])
            try:
                cost = next(streams[j])
            except StopIteration:
                finished[j] = True
                running.remove(j)
                continue
            if cost == _PHASE_END:
                running.remove(j)
            else:
                spent[j] += cost
        top = max(spent)
        spent = [top] * len(streams)
        slot += 1


def _staged(out, fn, arg_lists, cost):
    for args in zip(*arg_lists):
        out.append(fn(*args))
        yield cost


class _NextBlockProjection:
    def __init__(self, xn_ref, w_ref, pbuf_ref, chunk=MXU_COLS):
        self.xb = xn_ref[...].astype(BF16)
        self.w_ref = w_ref
        self.pbuf_ref = pbuf_ref
        width = w_ref.shape[1]
        self.bounds = [(lo, min(lo + chunk, width)) for lo in range(0, width, chunk)]

    def emit(self, count=1):
        for _ in range(min(count, len(self.bounds))):
            lo, hi = self.bounds.pop(0)
            self.pbuf_ref[:, lo:hi] = jnp.dot(self.xb, self.w_ref[:, lo:hi], preferred_element_type=F32)

    def spread(self, units_left):
        self.emit(-(-len(self.bounds) // max(units_left, 1)))

    def flush(self):
        self.emit(len(self.bounds))


def _mixer_call(kernel_fn, x, w, params, out_width, scratch, tb, blocks, name):
    bsz, seq, d = x.shape
    rows = blocks * tb
    nt = seq // rows
    width = w.shape[1]
    full = lambda a: pl.BlockSpec(a.shape, lambda t: (0,) * a.ndim)
    return pl.pallas_call(
        functools.partial(kernel_fn, tb=tb),
        grid=(nt,),
        in_specs=[pl.BlockSpec((bsz, rows, d), lambda t: (0, 0, 0), pipeline_mode=pl.Buffered(1)),
                  pl.BlockSpec((bsz, rows, d), lambda t: (0, jnp.minimum(t + 1, nt - 1), 0)),
                  pl.BlockSpec((d, width), lambda t: (0, 0), pipeline_mode=pl.Buffered(1))]
                 + [full(a) for a in params],
        out_specs=pl.BlockSpec((bsz, rows, out_width), lambda t: (0, t, 0)),
        out_shape=jax.ShapeDtypeStruct((bsz, seq, out_width), BF16),
        scratch_shapes=[pltpu.VMEM((bsz, rows, width), F32)] + scratch(bsz),
        compiler_params=pltpu.CompilerParams(dimension_semantics=("arbitrary",),
                                             vmem_limit_bytes=VMEM_LIMIT),
        name=name,
    )(x, x, w, *params)


def _stream_blocks(o_ref, tb):
    bsz, rows = o_ref.shape[0], o_ref.shape[1]
    return [(blk * bsz + b, b, pl.ds(blk * tb, tb)) for blk in range(rows // tb) for b in range(bsz)]


def _mixer_prologue(x0_ref, w_ref, pbuf_ref, zero_refs):
    @pl.when(pl.program_id(0) == 0)
    def _():
        for ref in zero_refs:
            ref[...] = jnp.zeros_like(ref)
        for b in range(x0_ref.shape[0]):
            pbuf_ref[b] = jnp.dot(x0_ref[b].astype(BF16), w_ref[...], preferred_element_type=F32)


def _gla_stream(xn_ref, w_ref, gup_ref, gb_ref, ng_ref, tri_ref, o_ref, pbuf_ref, st_ref, tb):
    dk, dv, hk, hv = GLA_DK, GLA_DV, GLA_HK, GLA_HV
    p_ref = pbuf_ref
    nxt = _NextBlockProjection(xn_ref, w_ref, pbuf_ref)
    incl, _, _ = _chunk_masks(UNIT)
    gd = p_ref[:, 2 * dk + 2 * dv:2 * dk + 2 * dv + LANE]
    log_a = _log_sigmoid(_dot(gd, gup_ref[...]) + gb_ref[...]) * (1.0 / GLA_GATE_TEMP)
    b_all = _cumsum_rows(tri_ref[...], log_a)
    mid = CHUNK // 2
    nchunk = UNIT // CHUNK
    probs = [(h, u) for u in range(tb // UNIT) for h in range(GLA_HEADS)]
    yield _ELEMENTWISE_UNIT

    pre = []
    for h, u in probs:
        rows = slice(u * UNIT, (u + 1) * UNIT)
        b = b_all[rows, h * hk:(h + 1) * hk]
        b_mid = _per_chunk_rows(b, mid)
        b_last = _per_chunk_rows(b, CHUNK - 1)
        qs = p_ref[rows, h * hk:(h + 1) * hk] * (hk ** -0.5) * jnp.exp(b - b_mid)
        ks = p_ref[rows, dk + h * hk:dk + (h + 1) * hk] * jnp.exp(b_mid - b)
        pre.append(dict(qs=qs, ks=ks, qe=qs * jnp.exp(b_mid), kd=ks * jnp.exp(b_last - b_mid),
                        e_last=jnp.exp(b_last),
                        v=p_ref[rows, 2 * dk + h * hv:2 * dk + (h + 1) * hv],
                        og=p_ref[rows, 2 * dk + dv + h * hv:2 * dk + dv + (h + 1) * hv]))
        yield _ELEMENTWISE_UNIT
    yield _PHASE_END

    attns, intras, kvs = [], [], []
    yield from _staged(attns, lambda d: jnp.where(incl, _dot_nt(d["qs"], d["ks"]), 0.0), [pre], 1.0)
    yield from _staged(intras, lambda a, d: _dot(a, d["v"]), [attns, pre], 2.0)
    yield from _staged(
        kvs, lambda d: [_dot_tn(d["v"][c * CHUNK:(c + 1) * CHUNK], d["kd"][c * CHUNK:(c + 1) * CHUNK])
                        for c in range(nchunk)], [pre], 2.0)
    yield _PHASE_END

    states = []
    for i, (h, u) in enumerate(probs):
        st = st_ref[h]
        per_chunk = []
        for c in range(nchunk):
            per_chunk.append(st)
            st = st * pre[i]["e_last"][c * CHUNK:c * CHUNK + 1] + kvs[i][c]
        st_ref[h] = st
        states.append(per_chunk)
        yield 0.3 * _ELEMENTWISE_UNIT
    units = len(probs) * nchunk
    for i, (h, u) in enumerate(probs):
        for c in range(nchunk):
            nxt.spread(units - (i * nchunk + c))
            rc = slice(c * CHUNK, (c + 1) * CHUNK)
            o = _dot_nt(pre[i]["qe"][rc], states[i][c]) + intras[i][rc]
            o = o * lax.rsqrt(jnp.mean(o * o, -1, keepdims=True) + RMS_EPS) * ng_ref[...]
            out_rows = slice(u * UNIT + c * CHUNK, u * UNIT + (c + 1) * CHUNK)
            o_ref[out_rows, h * hv:(h + 1) * hv] = (o * _silu(pre[i]["og"][rc])).astype(o_ref.dtype)
            yield 0.5 * _ELEMENTWISE_UNIT
    nxt.flush()


def _gla_kernel(x0_ref, xn_ref, w_ref, gup_ref, gb_ref, ng_ref, tri_ref, o_ref, pbuf_ref, st_ref, *, tb):
    _mixer_prologue(x0_ref, w_ref, pbuf_ref, [st_ref])
    _run_staggered([_gla_stream(xn_ref.at[b, rows], w_ref, gup_ref, gb_ref, ng_ref, tri_ref, o_ref.at[b, rows],
                                pbuf_ref.at[b, rows], st_ref.at[b], tb)
                    for _, b, rows in _stream_blocks(o_ref, tb)])


def _gla(x, w, gup, gb, ng, tb=256):
    scratch = lambda bsz: [pltpu.VMEM((bsz, GLA_HEADS, GLA_HV, GLA_HK), F32)]
    return _mixer_call(_gla_kernel, x, w, [gup, gb, ng, _block_tril(tb)], GLA_DV, scratch, tb, 2, "gla")


def _gdn_stream(xn_ref, w_ref, cw_ref, alog_row_ref, dtb_row_ref, alog_col_ref, dtb_col_ref, ng_ref, tri_ref,
                o_ref, pbuf_ref, st_ref, buf_ref, tb, first):
    wq = 3 * GDN_WIDTH
    hd = GDN_HD
    x = pbuf_ref[:, 0:wq]
    z_all = pbuf_ref[:, wq:wq + GDN_WIDTH]
    ab = pbuf_ref[:, wq + GDN_WIDTH:wq + GDN_WIDTH + LANE]
    nxt = _NextBlockProjection(xn_ref, w_ref, pbuf_ref)
    probs = [(h, u) for u in range(tb // UNIT) for h in range(GDN_HEADS)]
    n_groups = wq // (2 * LANE)
    early_units = n_groups + len(probs)

    buf_ref[SUBLANE:SUBLANE + tb, :] = x
    groups = []
    for g in range(n_groups):
        if first:
            nxt.spread(early_units - g)
        cols = slice(g * 2 * LANE, (g + 1) * 2 * LANE)
        y = x[:, cols] * cw_ref[GDN_CONV - 1:GDN_CONV, cols]
        for j in range(1, GDN_CONV):
            y = y + buf_ref[SUBLANE - j:SUBLANE - j + tb, cols] * cw_ref[GDN_CONV - 1 - j:GDN_CONV - j, cols]
        groups.append(_silu(y))
        yield _ELEMENTWISE_UNIT
    buf_ref[0:SUBLANE, :] = x[tb - SUBLANE:tb]
    qkv = jnp.concatenate(groups, axis=1)

    g_cols = -jnp.exp(alog_row_ref[...]) * _softplus(ab + dtb_row_ref[...])
    g_rows = -jnp.exp(alog_col_ref[...]) * _softplus(ab.T[0:SUBLANE] + dtb_col_ref[...])
    gam_cols = _cumsum_rows(tri_ref[...], g_cols)
    gam_rows = _cumsum_lanes(g_rows, tri_ref[...])
    beta_cols = _sigmoid(ab)
    yield 0.5 * _ELEMENTWISE_UNIT

    incl, strict, diag = _chunk_masks(UNIT)
    eye = diag.astype(F32)
    nchunk = UNIT // CHUNK

    pre = []
    for n, (h, u) in enumerate(probs):
        if first:
            nxt.spread(len(probs) - n)
        rows = slice(u * UNIT, (u + 1) * UNIT)
        q = qkv[rows, h * hd:(h + 1) * hd]
        k = qkv[rows, GDN_WIDTH + h * hd:GDN_WIDTH + (h + 1) * hd]
        v = qkv[rows, 2 * GDN_WIDTH + h * hd:2 * GDN_WIDTH + (h + 1) * hd]
        q = q * lax.rsqrt(jnp.sum(q * q, -1, keepdims=True) + L2_EPS) * (hd ** -0.5)
        k = k * lax.rsqrt(jnp.sum(k * k, -1, keepdims=True) + L2_EPS)
        gam = gam_cols[rows, h:h + 1]
        gam_r = gam_rows[h:h + 1, rows]
        beta = beta_cols[rows, GDN_HEADS + h:GDN_HEADS + h + 1]
        g_last = _per_chunk_rows(gam, CHUNK - 1)
        e_gam = jnp.exp(gam)
        kb = k * beta
        pre.append(dict(
            q=q, k=k, kb=kb, qe=q * e_gam,
            decay=jnp.where(incl, jnp.exp(jnp.minimum(gam - gam_r, 0.0)), 0.0),
            rhs=jnp.concatenate([v * beta, kb * e_gam], axis=1),
            kd=k * jnp.exp(g_last - gam), e_last=jnp.exp(g_last)))
        yield _ELEMENTWISE_UNIT
    yield _PHASE_END

    kqs, t_invs, uws, ros, qps = [], [], [], [], []
    yield from _staged(kqs, lambda d: _dot_nt(jnp.concatenate([d["kb"], d["q"]], axis=0), d["k"]), [pre], 2.0)
    a_mats = [jnp.where(strict, kq[0:UNIT] * d["decay"], 0.0) for kq, d in zip(kqs, pre)]
    yield from _unit_lower_inverse(t_invs, a_mats, eye)
    yield from _staged(uws, lambda t, d: _dot(t, d["rhs"]), [t_invs, pre], 2.0)
    yield from _staged(ros, lambda kq, d, uw: _dot(kq[UNIT:2 * UNIT] * d["decay"], uw), [kqs, pre, uws], 2.0)
    r_mats = [d["qe"] - ro[:, hd:2 * hd] for d, ro in zip(pre, ros)]
    yield from _staged(
        qps, lambda d, uw: [_dot_tn(d["kd"][c * CHUNK:(c + 1) * CHUNK],
                                    jnp.concatenate([uw[c * CHUNK:(c + 1) * CHUNK, 0:hd],
                                                     -uw[c * CHUNK:(c + 1) * CHUNK, hd:2 * hd]], axis=1))
                            for c in range(nchunk)], [pre, uws], 4.0)
    yield _PHASE_END

    outs = {}
    for u in range(tb // UNIT):
        for c in range(nchunk):
            rc = slice(c * CHUNK, (c + 1) * CHUNK)
            for h in range(GDN_HEADS):
                i = u * GDN_HEADS + h
                st = st_ref[h]
                outs[(i, c)] = _dot(r_mats[i][rc], st) + ros[i][rc, 0:hd]
                qp = qps[i][c]
                st_ref[h] = st * pre[i]["e_last"][c * CHUNK:c * CHUNK + 1] + _dot(qp[:, hd:2 * hd], st) + qp[:, 0:hd]
                yield 2.0

    units = len(probs) * nchunk
    for i, (h, u) in enumerate(probs):
        for c in range(nchunk):
            if not first:
                nxt.spread(units - (i * nchunk + c))
            o = outs[(i, c)]
            o = o * lax.rsqrt(jnp.mean(o * o, -1, keepdims=True) + RMS_EPS) * ng_ref[...]
            out_rows = slice(u * UNIT + c * CHUNK, u * UNIT + (c + 1) * CHUNK)
            z = z_all[out_rows, h * hd:(h + 1) * hd]
            o_ref[out_rows, h * hd:(h + 1) * hd] = (o * _silu(z)).astype(o_ref.dtype)
            yield 0.3 * _ELEMENTWISE_UNIT
    nxt.flush()


def _gdn_kernel(x0_ref, xn_ref, w_ref, cw_ref, alog_row_ref, dtb_row_ref, alog_col_ref, dtb_col_ref, ng_ref,
                tri_ref, o_ref, pbuf_ref, st_ref, buf_ref, *, tb):
    _mixer_prologue(x0_ref, w_ref, pbuf_ref, [st_ref, buf_ref])
    _run_staggered([_gdn_stream(xn_ref.at[b, rows], w_ref, cw_ref, alog_row_ref, dtb_row_ref, alog_col_ref,
                                dtb_col_ref, ng_ref, tri_ref, o_ref.at[b, rows], pbuf_ref.at[b, rows],
                                st_ref.at[b], buf_ref.at[b], tb, first=(j == 0))
                    for j, b, rows in _stream_blocks(o_ref, tb)])


def _gdn(x, w, cw, alog, dtb, ng, tb=256):
    lane_row = lambda v: jnp.pad(v, (0, LANE - v.shape[0]))[None, :]
    sub_col = lambda v: jnp.pad(v, (0, SUBLANE - v.shape[0]))[:, None]
    params = [cw, lane_row(alog), lane_row(dtb), sub_col(alog), sub_col(dtb), ng, _block_tril(tb)]
    scratch = lambda bsz: [pltpu.VMEM((bsz, GDN_HEADS, GDN_HD, GDN_HD), F32),
                           pltpu.VMEM((bsz, tb + SUBLANE, 3 * GDN_WIDTH), F32)]
    return _mixer_call(_gdn_kernel, x, w, params, GDN_WIDTH, scratch, tb, 2, "gdn")


def _stack_heads(x, head0):
    return jnp.concatenate([jnp.where(head0, x, 0.0), jnp.where(head0, 0.0, x)], axis=0)


def _rwkv_stream(xn_ref, w_ref, mu_ref, w0_ref, w2_ref, a0_ref, a2_ref, g2_ref, kk_ref, ka_ref,
                 rk_ref, lng_ref, lnb_ref, tri_ref, o_ref, pbuf_ref, st_ref, buf_ref, tb, first):
    buf_ref[SUBLANE:SUBLANE + tb, :] = pbuf_ref[...]
    nxt = _NextBlockProjection(xn_ref, w_ref, pbuf_ref)
    npair = RWKV_WIDTH // LANE
    nchunk = tb // CHUNK
    probs = [(pair, c) for c in range(nchunk) for pair in range(npair)]
    if first:
        nxt.spread(len(probs) // 4)
    p = buf_ref[SUBLANE:SUBLANE + tb, :]
    prev = buf_ref[SUBLANE - 1:SUBLANE - 1 + tb, :]
    buf_ref[0:SUBLANE, :] = p[tb - SUBLANE:tb]
    p = p + (prev - p) * mu_ref[...]
    wd = RWKV_WIDTH
    r_all = p[:, 0:wd]
    k_in = p[:, wd:2 * wd]
    v_all = p[:, 2 * wd:3 * wd]
    d_in = p[:, 3 * wd:3 * wd + LANE]
    a_in = p[:, 3 * wd + LANE:3 * wd + 2 * LANE]
    g_in = p[:, 3 * wd + 2 * LANE:3 * wd + 3 * LANE]
    lw_all = -jnp.exp(-_softplus(-(w0_ref[...] + _dot(jnp.tanh(d_in), w2_ref[...]))) - 0.5)
    a_all = _sigmoid(a0_ref[...] + _dot(a_in, a2_ref[...]))
    g_all = _dot(_sigmoid(g_in), g2_ref[...])
    kkraw_all = k_in * kk_ref[...]
    k_all = k_in * (1.0 + (a_all - 1.0) * ka_ref[...])
    cs_all = _cumsum_rows(tri_ref[...], lw_all)
    yield 4 * _ELEMENTWISE_UNIT

    row = lax.broadcasted_iota(jnp.int32, (CHUNK, LANE), 0)
    lane = lax.broadcasted_iota(jnp.int32, (CHUNK, LANE), 1)
    head0 = lane < RWKV_HD
    pos = lane % RWKV_HD
    incl, strict = row >= pos, row > pos
    eye = (row == pos).astype(F32)
    brow = lax.broadcasted_iota(jnp.int32, (LANE, LANE), 0) < RWKV_HD
    bcol = lax.broadcasted_iota(jnp.int32, (LANE, LANE), 1) < RWKV_HD
    same_head = brow == bcol
    mid = CHUNK // 2
    inv_hd = 1.0 / RWKV_HD
    stack = lambda x: _stack_heads(x.astype(BF16), head0)
    fold = lambda x: jnp.where(head0, x[0:CHUNK], x[CHUNK:LANE])

    def head_sum(x):
        s0 = jnp.sum(jnp.where(head0, x, 0.0), -1, keepdims=True)
        s1 = jnp.sum(jnp.where(head0, 0.0, x), -1, keepdims=True)
        return jnp.where(head0, s0, s1)

    pslice = lambda pair: slice(pair * LANE, (pair + 1) * LANE)
    cslice = lambda c: slice(c * CHUNK, (c + 1) * CHUNK)

    pre = []
    for n, (pair, c) in enumerate(probs):
        if first:
            nxt.spread(len(probs) - n)
        rs, ps = cslice(c), pslice(pair)
        r = r_all[rs, ps]
        k = k_all[rs, ps]
        v = v_all[rs, ps]
        lw = lw_all[rs, ps]
        cs = cs_all[rs, ps]
        c_mid = cs[mid:mid + 1]
        c_last = cs[CHUNK - 1:CHUNK]
        e_out = jnp.exp(c_mid - cs)
        e_mid = jnp.exp(c_mid)
        e_last_mid = jnp.exp(c_last - c_mid)
        kk = kkraw_all[rs, ps]
        kk = kk * lax.rsqrt(head_sum(kk * kk) + L2_EPS)
        r_s = r * jnp.exp(cs - c_mid)
        kk_s = kk * jnp.exp(cs - lw - c_mid)
        al_s = kk * a_all[rs, ps] * e_out
        k_s = k * e_out
        pre.append(dict(
            lhs=jnp.concatenate([kk_s, r_s], axis=0).astype(BF16),
            rhs=jnp.concatenate([stack(al_s), stack(k_s)], axis=0),
            v_st=stack(v), v=v.astype(BF16), kk2_st=stack(kk_s * e_mid),
            al2=(al_s * e_last_mid).astype(BF16), k2=(k_s * e_last_mid).astype(BF16),
            r2=r_s * e_mid, e_last=jnp.exp(c_last), rkr=r * k * rk_ref[:, ps]))
        yield _ELEMENTWISE_UNIT
    yield _PHASE_END

    grams, t_invs, bvs, tkws, zs, qfulls, x2s, akvs = [], [], [], [], [], [], [], []
    yield from _staged(grams, lambda d: _dot_nt(d["lhs"], d["rhs"]), [pre], 2.0)
    a_mats = [jnp.where(strict, g[0:CHUNK, 0:LANE], 0.0) for g in grams]
    yield from _unit_lower_inverse(t_invs, a_mats, eye, stack)
    yield from _staged(bvs, lambda g, d: _dot(jnp.where(strict, g[0:CHUNK, LANE:2 * LANE], 0.0), d["v_st"]),
                       [grams, pre], 1.0)
    yield from _staged(tkws, lambda t, d, bv: _dot(t, jnp.concatenate([d["kk2_st"], stack(bv)], axis=1)),
                       [t_invs, pre, bvs], 1.5)
    yield from _staged(zs, lambda tkw, d: _dot_tn(tkw, d["al2"]), [tkws, pre], 2.0)
    p_mats = [jnp.where(same_head, -z[0:LANE], 0.0).astype(BF16) for z in zs]
    yield from _staged(qfulls, lambda d: _dot_tn(d["v"], d["k2"]), [pre], 1.0)
    q_mats = [fold(qf) - fold(z[LANE:2 * LANE]) for qf, z in zip(qfulls, zs)]
    yield from _staged(
        x2s, lambda g, tkw: _dot(jnp.where(incl, g[CHUNK:2 * CHUNK, 0:LANE], 0.0),
                                 jnp.concatenate([stack(tkw[:, 0:LANE]), stack(tkw[:, LANE:2 * LANE])], axis=1)),
        [grams, tkws], 1.5)
    r_mats = [d["r2"] - x2[:, 0:LANE] for d, x2 in zip(pre, x2s)]
    yield from _staged(akvs, lambda g, d: _dot(jnp.where(incl, g[CHUNK:2 * CHUNK, LANE:2 * LANE], 0.0), d["v_st"]),
                       [grams, pre], 1.0)
    o_intras = [akv - x2[:, LANE:2 * LANE] for akv, x2 in zip(akvs, x2s)]
    yield _PHASE_END

    o_wide = []
    for i, (pair, c) in enumerate(probs):
        st = st_ref[pair]
        o_wide.append(_dot_nt(r_mats[i], stack(st)) + o_intras[i])
        st_ref[pair] = st * pre[i]["e_last"] + _dot(st, p_mats[i]) + q_mats[i]
        yield 2.0

    for i, (pair, c) in enumerate(probs):
        if not first:
            nxt.spread(len(probs) - i)
        rs, ps = cslice(c), pslice(pair)
        o = o_wide[i]
        d = o - head_sum(o) * inv_hd
        y = d * lax.rsqrt(head_sum(d * d) * inv_hd + RWKV_GN_EPS)
        bonus = head_sum(pre[i]["rkr"]) * v_all[rs, ps]
        o_ref[rs, ps] = ((y * lng_ref[:, ps] + lnb_ref[:, ps] + bonus) * g_all[rs, ps]).astype(o_ref.dtype)
        yield 0.4 * _ELEMENTWISE_UNIT
    nxt.flush()


def _rwkv_kernel(x0_ref, xn_ref, w_ref, mu_ref, w0_ref, w2_ref, a0_ref, a2_ref, g2_ref, kk_ref, ka_ref,
                 rk_ref, lng_ref, lnb_ref, tri_ref, o_ref, pbuf_ref, st_ref, buf_ref, *, tb):
    _mixer_prologue(x0_ref, w_ref, pbuf_ref, [st_ref, buf_ref])
    _run_staggered([_rwkv_stream(xn_ref.at[b, rows], w_ref, mu_ref, w0_ref, w2_ref, a0_ref, a2_ref, g2_ref, kk_ref,
                                 ka_ref, rk_ref, lng_ref, lnb_ref, tri_ref, o_ref.at[b, rows],
                                 pbuf_ref.at[b, rows], st_ref.at[b], buf_ref.at[b], tb, first=(j == 0))
                    for j, b, rows in _stream_blocks(o_ref, tb)])


def _rwkv(x, w, mu, w0, w2, a0, a2, g2, k_k, k_a, r_k, ln_g, ln_b, tb=256):
    params = [mu, w0, w2, a0, a2, g2, k_k, k_a, r_k, ln_g, ln_b, _block_tril(tb)]
    scratch = lambda bsz: [pltpu.VMEM((bsz, RWKV_WIDTH // LANE, CHUNK, LANE), F32),
                           pltpu.VMEM((bsz, tb + SUBLANE, w.shape[1]), F32)]
    return _mixer_call(_rwkv_kernel, x, w, params, RWKV_WIDTH, scratch, tb, 1, "rwkv")


def _merge_kernel(x_ref, wgate_ref, ogla_ref, ogdn_ref, orwkv_ref, wgla_ref, wgdn_ref, wrwkv_ref, wout_ref,
                  g_ref, b_ref, o_ref):
    d = D_MODEL
    x = x_ref[...]
    xb = x.astype(BF16)
    gate = lambda i: _sigmoid(jnp.dot(xb, wgate_ref[:, i * d:(i + 1) * d], preferred_element_type=F32))
    merged = (gate(0) * _dot(ogla_ref[...], wgla_ref[...])
              + gate(1) * _dot(ogdn_ref[...], wgdn_ref[...])
              + gate(2) * _dot(orwkv_ref[...], wrwkv_ref[...]))
    mix = _dot(merged, wout_ref[...])
    o_ref[...] = _layer_norm(DN_ALPHA * x + mix, g_ref[...], b_ref[...])


def _merge(x, wgate, ogla, ogdn, orwkv, wgla, wgdn, wrwkv, wout, g, b, layer, tm=1024):
    t = x.shape[0]
    d = D_MODEL
    tok = lambda w: pl.BlockSpec((tm, w), lambda i: (i, 0))
    full = lambda shape: pl.BlockSpec(shape, lambda i: (0, 0))
    resident = lambda rows: pl.BlockSpec((None, rows, d), lambda i: (layer, 0, 0),
                                         pipeline_mode=pl.Buffered(1))
    return pl.pallas_call(
        _merge_kernel,
        grid=(t // tm,),
        in_specs=[tok(d),
                  pl.BlockSpec((d, N_BRANCH * d), lambda i: (0, 0), pipeline_mode=pl.Buffered(1)),
                  tok(GLA_DV), tok(GDN_WIDTH), tok(RWKV_WIDTH),
                  resident(GLA_DV), resident(GDN_WIDTH), resident(RWKV_WIDTH), resident(d),
                  full((1, d)), full((1, d))],
        out_specs=tok(d),
        out_shape=jax.ShapeDtypeStruct((t, d), F32),
        compiler_params=pltpu.CompilerParams(dimension_semantics=("parallel",),
                                             vmem_limit_bytes=VMEM_LIMIT),
        name="merge",
    )(x, wgate, ogla, ogdn, orwkv, wgla, wgdn, wrwkv, wout, g, b)


def _mlp_kernel(x_ref, wu_ref, wd_ref, g_ref, b_ref, o_ref, *, tf):
    x = x_ref[...]
    xb = x.astype(BF16)
    acc = None
    for j in range(D_FF // tf):
        hid = jnp.maximum(jnp.dot(xb, wu_ref[:, j * tf:(j + 1) * tf], preferred_element_type=F32), 0.0)
        part = jnp.dot((hid * hid).astype(BF16), wd_ref[j * tf:(j + 1) * tf, :], preferred_element_type=F32)
        acc = part if acc is None else acc + part
    o_ref[...] = _layer_norm(DN_ALPHA * x + acc, g_ref[...], b_ref[...])


def _mlp(x, wu, wd, g, b, layer, tm=1024, tf=1024):
    t = x.shape[0]
    d = D_MODEL
    resident = lambda shape: pl.BlockSpec((None,) + shape, lambda i: (layer, 0, 0),
                                          pipeline_mode=pl.Buffered(1))
    return pl.pallas_call(
        functools.partial(_mlp_kernel, tf=tf),
        grid=(t // tm,),
        in_specs=[pl.BlockSpec((tm, d), lambda i: (i, 0)),
                  resident((d, D_FF)), resident((D_FF, d)),
                  pl.BlockSpec((1, d), lambda i: (0, 0)),
                  pl.BlockSpec((1, d), lambda i: (0, 0))],
        out_specs=pl.BlockSpec((tm, d), lambda i: (i, 0)),
        out_shape=jax.ShapeDtypeStruct((t, d), F32),
        compiler_params=pltpu.CompilerParams(dimension_semantics=("parallel",),
                                             vmem_limit_bytes=VMEM_LIMIT),
        name="mlp",
    )(x, wu, wd, g, b)


def _pad_rows(w, height):
    return jnp.pad(w, ((0, height - w.shape[0]), (0, 0)))


def _relayout_plans():
    o = _IN_OFFS
    s = 3 * RWKV_WIDTH
    gla = (2 * GLA_DK + 2 * GLA_DV + LANE,
           [(0, o[0], 2 * GLA_DK + GLA_DV), (2 * GLA_DK + GLA_DV, o[4], GLA_DV),
            (2 * GLA_DK + 2 * GLA_DV, o[3], GLA_GATE_RANK)])
    gdn = (4 * GDN_WIDTH + LANE,
           [(0, o[5], 3 * GDN_WIDTH), (3 * GDN_WIDTH, o[8], GDN_WIDTH), (4 * GDN_WIDTH, o[6], 2 * GDN_HEADS)])
    rwkv = (s + 3 * LANE,
            [(0, o[9], s), (s, o[9] + s, RWKV_DECAY_RANK), (s + LANE, o[9] + s + RWKV_DECAY_RANK, RWKV_A_RANK),
             (s + 2 * LANE, o[9] + s + RWKV_DECAY_RANK + RWKV_A_RANK, RWKV_GATE_RANK)])
    gates = (N_BRANCH * D_MODEL, [(0, o[10], N_BRANCH * D_MODEL)])
    return gla, gdn, rwkv, gates


def _relayout_kernel(wt_ref, *out_refs):
    tk = wt_ref.shape[1]
    for out_ref, (width, pieces) in zip(out_refs, _relayout_plans()):
        covered = 0
        for dst, src, n in pieces:
            if dst > covered:
                out_ref[:, covered:dst] = jnp.zeros((tk, dst - covered), out_ref.dtype)
            for off in range(0, n, LANE):
                m = min(LANE, n - off)
                tile = wt_ref[src + off:src + off + LANE, :].T
                out_ref[:, dst + off:dst + off + m] = tile[:, 0:m].astype(out_ref.dtype)
            covered = dst + n
        if width > covered:
            out_ref[:, covered:width] = jnp.zeros((tk, width - covered), out_ref.dtype)


def _split_w_in(w_in_t, layer, tk=128):
    _, n_in, k = w_in_t.shape
    assert all(src % SUBLANE == 0 and (n % LANE == 0 or src + n // LANE * LANE + LANE <= n_in)
               for _, pieces in _relayout_plans() for _, src, n in pieces)
    widths = [width for width, _ in _relayout_plans()]
    return pl.pallas_call(
        _relayout_kernel,
        grid=(k // tk,),
        in_specs=[pl.BlockSpec((None, n_in, tk), lambda i: (layer, 0, i))],
        out_specs=[pl.BlockSpec((tk, width), lambda i: (i, 0)) for width in widths],
        out_shape=[jax.ShapeDtypeStruct((k, width), BF16) for width in widths],
        compiler_params=pltpu.CompilerParams(dimension_semantics=("parallel",),
                                             vmem_limit_bytes=VMEM_LIMIT),
        name="w_in_relayout",
    )(w_in_t)


def _pad_mu(mu):
    s = 3 * RWKV_WIDTH
    z = jnp.zeros((LANE - RWKV_DECAY_RANK,), F32)
    return jnp.concatenate([mu[:s], mu[s:s + RWKV_DECAY_RANK], z,
                            mu[s + RWKV_DECAY_RANK:s + RWKV_DECAY_RANK + RWKV_A_RANK], z,
                            mu[s + RWKV_DECAY_RANK + RWKV_A_RANK:]])[None, :]


def kernel(x, w_in, gla_gate_up, gla_gate_bias, gla_norm_g, gdn_conv, gdn_a_log, gdn_dt_bias, gdn_norm_g, rwkv_mu, rwkv_w0, rwkv_w2, rwkv_a0, rwkv_a2, rwkv_g2, rwkv_k_k, rwkv_k_a, rwkv_r_k, rwkv_ln_g, rwkv_ln_b, w_br_gla, w_br_gdn, w_br_rwkv, w_out, ln1_g, ln1_b, w_up, w_down, ln2_g, ln2_b):
    bsz, seq, d = x.shape
    xt = x.reshape(bsz * seq, d)
    as_rows = lambda o: o.reshape(bsz * seq, o.shape[-1])
    row = lambda v: v[None, :]
    w_up_b = w_up.astype(BF16)
    w_down_b = w_down.astype(BF16)
    w_br_b = [w.astype(BF16) for w in (w_br_gla, w_br_gdn, w_br_rwkv)]
    w_out_b = w_out.astype(BF16)
    w_in_t = jnp.swapaxes(w_in, 1, 2)
    for l in range(DEPTH):
        w_gla, w_gdn, w_rwkv, w_gates = _split_w_in(w_in_t, l)
        xs = xt.reshape(bsz, seq, d)
        o_gla = _gla(xs, w_gla, _pad_rows(gla_gate_up[l], LANE).astype(BF16), row(gla_gate_bias[l]),
                     row(gla_norm_g[l]))
        o_gdn = _gdn(xs, w_gdn, gdn_conv[l], gdn_a_log[l], gdn_dt_bias[l], row(gdn_norm_g[l]))
        o_rwkv = _rwkv(
            xs, w_rwkv, _pad_mu(rwkv_mu[l]), row(rwkv_w0[l]), _pad_rows(rwkv_w2[l], LANE).astype(BF16),
            row(rwkv_a0[l]), _pad_rows(rwkv_a2[l], LANE).astype(BF16), rwkv_g2[l].astype(BF16),
            row(rwkv_k_k[l]), row(rwkv_k_a[l]), row(rwkv_r_k[l]), row(rwkv_ln_g[l]), row(rwkv_ln_b[l]))

        xt = _merge(xt, w_gates, as_rows(o_gla), as_rows(o_gdn), as_rows(o_rwkv), w_br_b[0], w_br_b[1], w_br_b[2],
                    w_out_b, row(ln1_g[l]), row(ln1_b[l]), l)
        xt = _mlp(xt, w_up_b, w_down_b, row(ln2_g[l]), row(ln2_b[l]), l)
    return xt.reshape(bsz, seq, d)
```

```python
import functools

import jax
import jax.numpy as jnp
from jax import lax
from jax.experimental import pallas as pl
from jax.experimental.pallas import tpu as pltpu

F32 = jnp.float32
BF16 = jnp.bfloat16

D_MODEL = 1024
DEPTH = 2
CHUNK = 64
GLA_HEADS = 4
GLA_DK = 512
GLA_DV = 1024
GLA_HK = 128
GLA_HV = 256
GLA_GATE_RANK = 16
GLA_GATE_TEMP = 16.0
GDN_HEADS = 4
GDN_HD = 128
GDN_WIDTH = 512
GDN_CONV = 4
RWKV_HD = 64
RWKV_WIDTH = 512
RWKV_DECAY_RANK = 64
RWKV_A_RANK = 64
RWKV_GATE_RANK = 128
RWKV_GN_EPS = 64e-5
N_BRANCH = 3
D_FF = 4 * D_MODEL
DN_ALPHA = (2 * DEPTH) ** 0.25
LN_EPS = 1e-5
RMS_EPS = 1e-6
L2_EPS = 1e-6

LANE = 128
SUBLANE = 8
MXU_COLS = 256
VMEM_LIMIT = 56 * 1024 * 1024
UNIT = 2 * CHUNK

_IN_WIDTHS = (GLA_DK, GLA_DK, GLA_DV, GLA_GATE_RANK, GLA_DV,
              3 * GDN_WIDTH, GDN_HEADS, GDN_HEADS, GDN_WIDTH,
              3 * RWKV_WIDTH + RWKV_DECAY_RANK + RWKV_A_RANK + RWKV_GATE_RANK,
              N_BRANCH * D_MODEL)
_IN_OFFS = [0]
for _w in _IN_WIDTHS:
    _IN_OFFS.append(_IN_OFFS[-1] + _w)


def _dot(a, b):
    return jnp.dot(a.astype(BF16), b.astype(BF16), preferred_element_type=F32)


def _dot_nt(a, b):
    return lax.dot_general(a.astype(BF16), b.astype(BF16), (((1,), (1,)), ((), ())),
                           preferred_element_type=F32)


def _dot_tn(a, b):
    return lax.dot_general(a.astype(BF16), b.astype(BF16), (((0,), (0,)), ((), ())),
                           preferred_element_type=F32)


def _split_bf16(x):
    hi = x.astype(BF16)
    return hi, (x - hi.astype(F32)).astype(BF16)


def _cumsum_rows(tri, x):
    hi, lo = _split_bf16(x)
    return (jnp.dot(tri, hi, preferred_element_type=F32)
            + jnp.dot(tri, lo, preferred_element_type=F32))


def _cumsum_lanes(x, tri):
    hi, lo = _split_bf16(x)
    dims = (((1,), (1,)), ((), ()))
    return (lax.dot_general(hi, tri, dims, preferred_element_type=F32)
            + lax.dot_general(lo, tri, dims, preferred_element_type=F32))


def _sigmoid(x):
    return 1.0 / (1.0 + jnp.exp(-x))


def _silu(x):
    return x * _sigmoid(x)


def _softplus(x):
    return jnp.maximum(x, 0.0) + jnp.log1p(jnp.exp(-jnp.abs(x)))


def _log_sigmoid(x):
    return -_softplus(-x)


def _chunk_masks(n):
    row = lax.broadcasted_iota(jnp.int32, (n, n), 0)
    col = lax.broadcasted_iota(jnp.int32, (n, n), 1)
    same = (row // CHUNK) == (col // CHUNK)
    return same & (row >= col), same & (row > col), row == col


def _per_chunk_rows(x, offset):
    w = x.shape[1]
    return jnp.concatenate(
        [jnp.broadcast_to(x[c * CHUNK + offset:c * CHUNK + offset + 1], (CHUNK, w))
         for c in range(UNIT // CHUNK)], axis=0)


def _unit_lower_inverse(out, a_list, eye, as_rhs=lambda p: p):
    n = eye.shape[0]
    ps = [-a for a in a_list]
    ts = [eye + p for p in ps]
    squares = []
    yield from _staged(squares, lambda p: _dot(p, as_rhs(p)), [ps], 1.0)
    ps = squares
    for _ in range(CHUNK.bit_length() - 3):
        prods = []
        yield from _staged(prods, lambda t, p: _dot(jnp.concatenate([t, p], axis=0), as_rhs(p)), [ts, ps], 2.0)
        ts = [t + pr[0:n] for t, pr in zip(ts, prods)]
        ps = [pr[n:2 * n] for pr in prods]
    yield from _staged(out, lambda t, p: t + _dot(t, as_rhs(p)), [ts, ps], 1.0)


def _layer_norm(y, g, b):
    mu = jnp.mean(y, -1, keepdims=True)
    d = y - mu
    var = jnp.mean(d * d, -1, keepdims=True)
    return d * lax.rsqrt(var + LN_EPS) * g + b


def _block_tril(n):
    idx = jnp.arange(n)
    same = (idx[:, None] // CHUNK) == (idx[None, :] // CHUNK)
    return (same & (idx[:, None] >= idx[None, :])).astype(BF16)


_PHASE_END = "phase-end"
_ELEMENTWISE_UNIT = 5.0


def _run_staggered(streams):
    spent = [0.0] * len(streams)
    finished = [False] * len(streams)
    slot = 0
    while not all(finished):
        running = [j for j in range(len(streams)) if j <= slot and not finished[j]]
        while running:
            j = min(running, key=lambda i: spent[i])
            try:
                cost = next(streams[j])
            except StopIteration:
                finished[j] = True
                running.remove(j)
                continue
            if cost == _PHASE_END:
                running.remove(j)
            else:
                spent[j] += cost
        top = max(spent)
        spent = [top] * len(streams)
        slot += 1


def _staged(out, fn, arg_lists, cost):
    for args in zip(*arg_lists):
        out.append(fn(*args))
        yield cost


class _NextBlockProjection:
    def __init__(self, xn_ref, w_ref, pbuf_ref, chunk=MXU_COLS):
        self.xb = xn_ref[...].astype(BF16)
        self.w_ref = w_ref
        self.pbuf_ref = pbuf_ref
        width = w_ref.shape[1]
        self.bounds = [(lo, min(lo + chunk, width)) for lo in range(0, width, chunk)]

    def emit(self, count=1):
        for _ in range(min(count, len(self.bounds))):
            lo, hi = self.bounds.pop(0)
            self.pbuf_ref[:, lo:hi] = jnp.dot(self.xb, self.w_ref[:, lo:hi], preferred_element_type=F32)

    def spread(self, units_left):
        self.emit(-(-len(self.bounds) // max(units_left, 1)))

    def flush(self):
        self.emit(len(self.bounds))


def _mixer_call(kernel_fn, x, w, params, out_width, scratch, tb, blocks, name):
    bsz, seq, d = x.shape
    rows = blocks * tb
    nt = seq // rows
    width = w.shape[1]
    full = lambda a: pl.BlockSpec(a.shape, lambda t: (0,) * a.ndim)
    return pl.pallas_call(
        functools.partial(kernel_fn, tb=tb),
        grid=(nt,),
        in_specs=[pl.BlockSpec((bsz, rows, d), lambda t: (0, 0, 0), pipeline_mode=pl.Buffered(1)),
                  pl.BlockSpec((bsz, rows, d), lambda t: (0, jnp.minimum(t + 1, nt - 1), 0)),
                  pl.BlockSpec((d, width), lambda t: (0, 0), pipeline_mode=pl.Buffered(1))]
                 + [full(a) for a in params],
        out_specs=pl.BlockSpec((bsz, rows, out_width), lambda t: (0, t, 0)),
        out_shape=jax.ShapeDtypeStruct((bsz, seq, out_width), BF16),
        scratch_shapes=[pltpu.VMEM((bsz, rows, width), F32)] + scratch(bsz),
        compiler_params=pltpu.CompilerParams(dimension_semantics=("arbitrary",),
                                             vmem_limit_bytes=VMEM_LIMIT),
        name=name,
    )(x, x, w, *params)


def _stream_blocks(o_ref, tb):
    bsz, rows = o_ref.shape[0], o_ref.shape[1]
    return [(blk * bsz + b, b, pl.ds(blk * tb, tb)) for blk in range(rows // tb) for b in range(bsz)]


def _mixer_prologue(x0_ref, w_ref, pbuf_ref, zero_refs):
    @pl.when(pl.program_id(0) == 0)
    def _():
        for ref in zero_refs:
            ref[...] = jnp.zeros_like(ref)
        for b in range(x0_ref.shape[0]):
            pbuf_ref[b] = jnp.dot(x0_ref[b].astype(BF16), w_ref[...], preferred_element_type=F32)


def _gla_stream(xn_ref, w_ref, gup_ref, gb_ref, ng_ref, tri_ref, o_ref, pbuf_ref, st_ref, tb):
    dk, dv, hk, hv = GLA_DK, GLA_DV, GLA_HK, GLA_HV
    p_ref = pbuf_ref
    nxt = _NextBlockProjection(xn_ref, w_ref, pbuf_ref)
    incl, _, _ = _chunk_masks(UNIT)
    gd = p_ref[:, 2 * dk + 2 * dv:2 * dk + 2 * dv + LANE]
    log_a = _log_sigmoid(_dot(gd, gup_ref[...]) + gb_ref[...]) * (1.0 / GLA_GATE_TEMP)
    b_all = _cumsum_rows(tri_ref[...], log_a)
    mid = CHUNK // 2
    nchunk = UNIT // CHUNK
    probs = [(h, u) for u in range(tb // UNIT) for h in range(GLA_HEADS)]
    yield _ELEMENTWISE_UNIT

    pre = []
    for h, u in probs:
        rows = slice(u * UNIT, (u + 1) * UNIT)
        b = b_all[rows, h * hk:(h + 1) * hk]
        b_mid = _per_chunk_rows(b, mid)
        b_last = _per_chunk_rows(b, CHUNK - 1)
        qs = p_ref[rows, h * hk:(h + 1) * hk] * (hk ** -0.5) * jnp.exp(b - b_mid)
        ks = p_ref[rows, dk + h * hk:dk + (h + 1) * hk] * jnp.exp(b_mid - b)
        pre.append(dict(qs=qs, ks=ks, qe=qs * jnp.exp(b_mid), kd=ks * jnp.exp(b_last - b_mid),
                        e_last=jnp.exp(b_last),
                        v=p_ref[rows, 2 * dk + h * hv:2 * dk + (h + 1) * hv],
                        og=p_ref[rows, 2 * dk + dv + h * hv:2 * dk + dv + (h + 1) * hv]))
        yield _ELEMENTWISE_UNIT
    yield _PHASE_END

    attns, intras, kvs = [], [], []
    yield from _staged(attns, lambda d: jnp.where(incl, _dot_nt(d["qs"], d["ks"]), 0.0), [pre], 1.0)
    yield from _staged(intras, lambda a, d: _dot(a, d["v"]), [attns, pre], 2.0)
    yield from _staged(
        kvs, lambda d: [_dot_tn(d["v"][c * CHUNK:(c + 1) * CHUNK], d["kd"][c * CHUNK:(c + 1) * CHUNK])
                        for c in range(nchunk)], [pre], 2.0)
    yield _PHASE_END

    states = []
    for i, (h, u) in enumerate(probs):
        st = st_ref[h]
        per_chunk = []
        for c in range(nchunk):
            per_chunk.append(st)
            st = st * pre[i]["e_last"][c * CHUNK:c * CHUNK + 1] + kvs[i][c]
        st_ref[h] = st
        states.append(per_chunk)
        yield 0.3 * _ELEMENTWISE_UNIT
    units = len(probs) * nchunk
    for i, (h, u) in enumerate(probs):
        for c in range(nchunk):
            nxt.spread(units - (i * nchunk + c))
            rc = slice(c * CHUNK, (c + 1) * CHUNK)
            o = _dot_nt(pre[i]["qe"][rc], states[i][c]) + intras[i][rc]
            o = o * lax.rsqrt(jnp.mean(o * o, -1, keepdims=True) + RMS_EPS) * ng_ref[...]
            out_rows = slice(u * UNIT + c * CHUNK, u * UNIT + (c + 1) * CHUNK)
            o_ref[out_rows, h * hv:(h + 1) * hv] = (o * _silu(pre[i]["og"][rc])).astype(o_ref.dtype)
            yield 0.5 * _ELEMENTWISE_UNIT
    nxt.flush()


def _gla_kernel(x0_ref, xn_ref, w_ref, gup_ref, gb_ref, ng_ref, tri_ref, o_ref, pbuf_ref, st_ref, *, tb):
    _mixer_prologue(x0_ref, w_ref, pbuf_ref, [st_ref])
    _run_staggered([_gla_stream(xn_ref.at[b, rows], w_ref, gup_ref, gb_ref, ng_ref, tri_ref, o_ref.at[b, rows],
                                pbuf_ref.at[b, rows], st_ref.at[b], tb)
                    for _, b, rows in _stream_blocks(o_ref, tb)])


def _gla(x, w, gup, gb, ng, tb=256):
    scratch = lambda bsz: [pltpu.VMEM((bsz, GLA_HEADS, GLA_HV, GLA_HK), F32)]
    return _mixer_call(_gla_kernel, x, w, [gup, gb, ng, _block_tril(tb)], GLA_DV, scratch, tb, 2, "gla")


def _gdn_stream(xn_ref, w_ref, cw_ref, alog_row_ref, dtb_row_ref, alog_col_ref, dtb_col_ref, ng_ref, tri_ref,
                o_ref, pbuf_ref, st_ref, buf_ref, tb, first):
    wq = 3 * GDN_WIDTH
    hd = GDN_HD
    x = pbuf_ref[:, 0:wq]
    z_all = pbuf_ref[:, wq:wq + GDN_WIDTH]
    ab = pbuf_ref[:, wq + GDN_WIDTH:wq + GDN_WIDTH + LANE]
    nxt = _NextBlockProjection(xn_ref, w_ref, pbuf_ref)
    probs = [(h, u) for u in range(tb // UNIT) for h in range(GDN_HEADS)]
    n_groups = wq // (2 * LANE)
    early_units = n_groups + len(probs)

    g_cols = -jnp.exp(alog_row_ref[...]) * _softplus(ab + dtb_row_ref[...])
    g_rows = -jnp.exp(alog_col_ref[...]) * _softplus(ab.T[0:SUBLANE] + dtb_col_ref[...])
    gam_cols = _cumsum_rows(tri_ref[...], g_cols)
    gam_rows = _cumsum_lanes(g_rows, tri_ref[...])
    beta_cols = _sigmoid(ab)
    yield 0.5 * _ELEMENTWISE_UNIT

    buf_ref[SUBLANE:SUBLANE + tb, :] = x
    groups = []
    for g in range(n_groups):
        if first:
            nxt.spread(early_units - g)
        cols = slice(g * 2 * LANE, (g + 1) * 2 * LANE)
        y = x[:, cols] * cw_ref[GDN_CONV - 1:GDN_CONV, cols]
        for j in range(1, GDN_CONV):
            y = y + buf_ref[SUBLANE - j:SUBLANE - j + tb, cols] * cw_ref[GDN_CONV - 1 - j:GDN_CONV - j, cols]
        groups.append(_silu(y))
        yield _ELEMENTWISE_UNIT
    buf_ref[0:SUBLANE, :] = x[tb - SUBLANE:tb]
    qkv = jnp.concatenate(groups, axis=1)

    incl, strict, diag = _chunk_masks(UNIT)
    eye = diag.astype(F32)
    nchunk = UNIT // CHUNK

    pre = []
    for n, (h, u) in enumerate(probs):
        if first:
            nxt.spread(len(probs) - n)
        rows = slice(u * UNIT, (u + 1) * UNIT)
        q = qkv[rows, h * hd:(h + 1) * hd]
        k = qkv[rows, GDN_WIDTH + h * hd:GDN_WIDTH + (h + 1) * hd]
        v = qkv[rows, 2 * GDN_WIDTH + h * hd:2 * GDN_WIDTH + (h + 1) * hd]
        q = q * lax.rsqrt(jnp.sum(q * q, -1, keepdims=True) + L2_EPS) * (hd ** -0.5)
        k = k * lax.rsqrt(jnp.sum(k * k, -1, keepdims=True) + L2_EPS)
        gam = gam_cols[rows, h:h + 1]
        gam_r = gam_rows[h:h + 1, rows]
        beta = beta_cols[rows, GDN_HEADS + h:GDN_HEADS + h + 1]
        g_last = _per_chunk_rows(gam, CHUNK - 1)
        e_gam = jnp.exp(gam)
        kb = k * beta
        pre.append(dict(
            q=q, k=k, kb=kb, qe=q * e_gam,
            decay=jnp.where(incl, jnp.exp(jnp.minimum(gam - gam_r, 0.0)), 0.0),
            rhs=jnp.concatenate([v * beta, kb * e_gam], axis=1),
            kd=k * jnp.exp(g_last - gam), e_last=jnp.exp(g_last)))
        yield _ELEMENTWISE_UNIT
    yield _PHASE_END

    kqs, t_invs, uws, ros, qps = [], [], [], [], []
    yield from _staged(kqs, lambda d: _dot_nt(jnp.concatenate([d["kb"], d["q"]], axis=0), d["k"]), [pre], 2.0)
    a_mats = [jnp.where(strict, kq[0:UNIT] * d["decay"], 0.0) for kq, d in zip(kqs, pre)]
    yield from _unit_lower_inverse(t_invs, a_mats, eye)
    yield from _staged(uws, lambda t, d: _dot(t, d["rhs"]), [t_invs, pre], 2.0)
    yield from _staged(ros, lambda kq, d, uw: _dot(kq[UNIT:2 * UNIT] * d["decay"], uw), [kqs, pre, uws], 2.0)
    r_mats = [d["qe"] - ro[:, hd:2 * hd] for d, ro in zip(pre, ros)]
    yield from _staged(
        qps, lambda d, uw: [_dot_tn(d["kd"][c * CHUNK:(c + 1) * CHUNK],
                                    jnp.concatenate([uw[c * CHUNK:(c + 1) * CHUNK, 0:hd],
                                                     -uw[c * CHUNK:(c + 1) * CHUNK, hd:2 * hd]], axis=1))
                            for c in range(nchunk)], [pre, uws], 4.0)
    yield _PHASE_END

    outs = {}
    for u in range(tb // UNIT):
        for c in range(nchunk):
            rc = slice(c * CHUNK, (c + 1) * CHUNK)
            for h in range(GDN_HEADS):
                i = u * GDN_HEADS + h
                st = st_ref[h]
                outs[(i, c)] = _dot(r_mats[i][rc], st) + ros[i][rc, 0:hd]
                qp = qps[i][c]
                st_ref[h] = st * pre[i]["e_last"][c * CHUNK:c * CHUNK + 1] + _dot(qp[:, hd:2 * hd], st) + qp[:, 0:hd]
                yield 2.0

    units = len(probs) * nchunk
    for i, (h, u) in enumerate(probs):
        for c in range(nchunk):
            if not first:
                nxt.spread(units - (i * nchunk + c))
            o = outs[(i, c)]
            o = o * lax.rsqrt(jnp.mean(o * o, -1, keepdims=True) + RMS_EPS) * ng_ref[...]
            out_rows = slice(u * UNIT + c * CHUNK, u * UNIT + (c + 1) * CHUNK)
            z = z_all[out_rows, h * hd:(h + 1) * hd]
            o_ref[out_rows, h * hd:(h + 1) * hd] = (o * _silu(z)).astype(o_ref.dtype)
            yield 0.3 * _ELEMENTWISE_UNIT
    nxt.flush()


def _gdn_kernel(x0_ref, xn_ref, w_ref, cw_ref, alog_row_ref, dtb_row_ref, alog_col_ref, dtb_col_ref, ng_ref,
                tri_ref, o_ref, pbuf_ref, st_ref, buf_ref, *, tb):
    _mixer_prologue(x0_ref, w_ref, pbuf_ref, [st_ref, buf_ref])
    _run_staggered([_gdn_stream(xn_ref.at[b, rows], w_ref, cw_ref, alog_row_ref, dtb_row_ref, alog_col_ref,
                                dtb_col_ref, ng_ref, tri_ref, o_ref.at[b, rows], pbuf_ref.at[b, rows],
                                st_ref.at[b], buf_ref.at[b], tb, first=(j == 0))
                    for j, b, rows in _stream_blocks(o_ref, tb)])


def _gdn(x, w, cw, alog, dtb, ng, tb=256):
    lane_row = lambda v: jnp.pad(v, (0, LANE - v.shape[0]))[None, :]
    sub_col = lambda v: jnp.pad(v, (0, SUBLANE - v.shape[0]))[:, None]
    params = [cw, lane_row(alog), lane_row(dtb), sub_col(alog), sub_col(dtb), ng, _block_tril(tb)]
    scratch = lambda bsz: [pltpu.VMEM((bsz, GDN_HEADS, GDN_HD, GDN_HD), F32),
                           pltpu.VMEM((bsz, tb + SUBLANE, 3 * GDN_WIDTH), F32)]
    return _mixer_call(_gdn_kernel, x, w, params, GDN_WIDTH, scratch, tb, 2, "gdn")


def _stack_heads(x, head0):
    return jnp.concatenate([jnp.where(head0, x, 0.0), jnp.where(head0, 0.0, x)], axis=0)


def _rwkv_stream(xn_ref, w_ref, mu_ref, w0_ref, w2_ref, a0_ref, a2_ref, g2_ref, kk_ref, ka_ref,
                 rk_ref, lng_ref, lnb_ref, tri_ref, o_ref, pbuf_ref, st_ref, buf_ref, tb, first):
    buf_ref[SUBLANE:SUBLANE + tb, :] = pbuf_ref[...]
    nxt = _NextBlockProjection(xn_ref, w_ref, pbuf_ref)
    npair = RWKV_WIDTH // LANE
    nchunk = tb // CHUNK
    probs = [(pair, c) for c in range(nchunk) for pair in range(npair)]
    p = buf_ref[SUBLANE:SUBLANE + tb, :]
    prev = buf_ref[SUBLANE - 1:SUBLANE - 1 + tb, :]
    buf_ref[0:SUBLANE, :] = p[tb - SUBLANE:tb]
    p = p + (prev - p) * mu_ref[...]
    wd = RWKV_WIDTH
    r_all = p[:, 0:wd]
    k_in = p[:, wd:2 * wd]
    v_all = p[:, 2 * wd:3 * wd]
    d_in = p[:, 3 * wd:3 * wd + LANE]
    a_in = p[:, 3 * wd + LANE:3 * wd + 2 * LANE]
    g_in = p[:, 3 * wd + 2 * LANE:3 * wd + 3 * LANE]
    lw_all = -jnp.exp(-_softplus(-(w0_ref[...] + _dot(jnp.tanh(d_in), w2_ref[...]))) - 0.5)
    a_all = _sigmoid(a0_ref[...] + _dot(a_in, a2_ref[...]))
    g_all = _dot(_sigmoid(g_in), g2_ref[...])
    kkraw_all = k_in * kk_ref[...]
    k_all = k_in * (1.0 + (a_all - 1.0) * ka_ref[...])
    cs_all = _cumsum_rows(tri_ref[...], lw_all)
    if first:
        nxt.spread(len(probs) // 4)
    yield 4 * _ELEMENTWISE_UNIT

    row = lax.broadcasted_iota(jnp.int32, (CHUNK, LANE), 0)
    lane = lax.broadcasted_iota(jnp.int32, (CHUNK, LANE), 1)
    head0 = lane < RWKV_HD
    pos = lane % RWKV_HD
    incl, strict = row >= pos, row > pos
    eye = (row == pos).astype(F32)
    brow = lax.broadcasted_iota(jnp.int32, (LANE, LANE), 0) < RWKV_HD
    bcol = lax.broadcasted_iota(jnp.int32, (LANE, LANE), 1) < RWKV_HD
    same_head = brow == bcol
    mid = CHUNK // 2
    inv_hd = 1.0 / RWKV_HD
    stack = lambda x: _stack_heads(x.astype(BF16), head0)
    fold = lambda x: jnp.where(head0, x[0:CHUNK], x[CHUNK:LANE])

    def head_sum(x):
        s0 = jnp.sum(jnp.where(head0, x, 0.0), -1, keepdims=True)
        s1 = jnp.sum(jnp.where(head0, 0.0, x), -1, keepdims=True)
        return jnp.where(head0, s0, s1)

    pslice = lambda pair: slice(pair * LANE, (pair + 1) * LANE)
    cslice = lambda c: slice(c * CHUNK, (c + 1) * CHUNK)

    pre = []
    for n, (pair, c) in enumerate(probs):
        if first:
            nxt.spread(len(probs) - n)
        rs, ps = cslice(c), pslice(pair)
        r = r_all[rs, ps]
        k = k_all[rs, ps]
        v = v_all[rs, ps]
        lw = lw_all[rs, ps]
        cs = cs_all[rs, ps]
        c_mid = cs[mid:mid + 1]
        c_last = cs[CHUNK - 1:CHUNK]
        e_out = jnp.exp(c_mid - cs)
        e_mid = jnp.exp(c_mid)
        e_last_mid = jnp.exp(c_last - c_mid)
        kk = kkraw_all[rs, ps]
        kk = kk * lax.rsqrt(head_sum(kk * kk) + L2_EPS)
        r_s = r * jnp.exp(cs - c_mid)
        kk_s = kk * jnp.exp(cs - lw - c_mid)
        al_s = kk * a_all[rs, ps] * e_out
        k_s = k * e_out
        pre.append(dict(
            lhs=jnp.concatenate([kk_s, r_s], axis=0).astype(BF16),
            rhs=jnp.concatenate([stack(al_s), stack(k_s)], axis=0),
            v_st=stack(v), v=v.astype(BF16), kk2_st=stack(kk_s * e_mid),
            al2=(al_s * e_last_mid).astype(BF16), k2=(k_s * e_last_mid).astype(BF16),
            r2=r_s * e_mid, e_last=jnp.exp(c_last), rkr=r * k * rk_ref[:, ps]))
        yield _ELEMENTWISE_UNIT
    yield _PHASE_END

    grams, t_invs, bvs, tkws, zs, qfulls, x2s, akvs = [], [], [], [], [], [], [], []
    yield from _staged(grams, lambda d: _dot_nt(d["lhs"], d["rhs"]), [pre], 2.0)
    a_mats = [jnp.where(strict, g[0:CHUNK, 0:LANE], 0.0) for g in grams]
    yield from _unit_lower_inverse(t_invs, a_mats, eye, stack)
    yield from _staged(bvs, lambda g, d: _dot(jnp.where(strict, g[0:CHUNK, LANE:2 * LANE], 0.0), d["v_st"]),
                       [grams, pre], 1.0)
    yield from _staged(tkws, lambda t, d, bv: _dot(t, jnp.concatenate([d["kk2_st"], stack(bv)], axis=1)),
                       [t_invs, pre, bvs], 1.5)
    yield from _staged(zs, lambda tkw, d: _dot_tn(tkw, d["al2"]), [tkws, pre], 2.0)
    p_mats = [jnp.where(same_head, -z[0:LANE], 0.0).astype(BF16) for z in zs]
    yield from _staged(qfulls, lambda d: _dot_tn(d["v"], d["k2"]), [pre], 1.0)
    q_mats = [fold(qf) - fold(z[LANE:2 * LANE]) for qf, z in zip(qfulls, zs)]
    yield from _staged(
        x2s, lambda g, tkw: _dot(jnp.where(incl, g[CHUNK:2 * CHUNK, 0:LANE], 0.0),
                                 jnp.concatenate([stack(tkw[:, 0:LANE]), stack(tkw[:, LANE:2 * LANE])], axis=1)),
        [grams, tkws], 1.5)
    r_mats = [d["r2"] - x2[:, 0:LANE] for d, x2 in zip(pre, x2s)]
    yield from _staged(akvs, lambda g, d: _dot(jnp.where(incl, g[CHUNK:2 * CHUNK, LANE:2 * LANE], 0.0), d["v_st"]),
                       [grams, pre], 1.0)
    o_intras = [akv - x2[:, LANE:2 * LANE] for akv, x2 in zip(akvs, x2s)]
    yield _PHASE_END

    o_wide = []
    for i, (pair, c) in enumerate(probs):
        st = st_ref[pair]
        o_wide.append(_dot_nt(r_mats[i], stack(st)) + o_intras[i])
        st_ref[pair] = st * pre[i]["e_last"] + _dot(st, p_mats[i]) + q_mats[i]
        yield 2.0

    for i, (pair, c) in enumerate(probs):
        if not first:
            nxt.spread(len(probs) - i)
        rs, ps = cslice(c), pslice(pair)
        o = o_wide[i]
        d = o - head_sum(o) * inv_hd
        y = d * lax.rsqrt(head_sum(d * d) * inv_hd + RWKV_GN_EPS)
        bonus = head_sum(pre[i]["rkr"]) * v_all[rs, ps]
        o_ref[rs, ps] = ((y * lng_ref[:, ps] + lnb_ref[:, ps] + bonus) * g_all[rs, ps]).astype(o_ref.dtype)
        yield 0.4 * _ELEMENTWISE_UNIT
    nxt.flush()


def _rwkv_kernel(x0_ref, xn_ref, w_ref, mu_ref, w0_ref, w2_ref, a0_ref, a2_ref, g2_ref, kk_ref, ka_ref,
                 rk_ref, lng_ref, lnb_ref, tri_ref, o_ref, pbuf_ref, st_ref, buf_ref, *, tb):
    _mixer_prologue(x0_ref, w_ref, pbuf_ref, [st_ref, buf_ref])
    _run_staggered([_rwkv_stream(xn_ref.at[b, rows], w_ref, mu_ref, w0_ref, w2_ref, a0_ref, a2_ref, g2_ref, kk_ref,
                                 ka_ref, rk_ref, lng_ref, lnb_ref, tri_ref, o_ref.at[b, rows],
                                 pbuf_ref.at[b, rows], st_ref.at[b], buf_ref.at[b], tb, first=(j == 0))
                    for j, b, rows in _stream_blocks(o_ref, tb)])


def _rwkv(x, w, mu, w0, w2, a0, a2, g2, k_k, k_a, r_k, ln_g, ln_b, tb=256):
    params = [mu, w0, w2, a0, a2, g2, k_k, k_a, r_k, ln_g, ln_b, _block_tril(tb)]
    scratch = lambda bsz: [pltpu.VMEM((bsz, RWKV_WIDTH // LANE, CHUNK, LANE), F32),
                           pltpu.VMEM((bsz, tb + SUBLANE, w.shape[1]), F32)]
    return _mixer_call(_rwkv_kernel, x, w, params, RWKV_WIDTH, scratch, tb, 1, "rwkv")


def _merge_kernel(x_ref, wgate_ref, ogla_ref, ogdn_ref, orwkv_ref, wgla_ref, wgdn_ref, wrwkv_ref, wout_ref,
                  g_ref, b_ref, o_ref):
    d = D_MODEL
    x = x_ref[...]
    xb = x.astype(BF16)
    gate = lambda i: _sigmoid(jnp.dot(xb, wgate_ref[:, i * d:(i + 1) * d], preferred_element_type=F32))
    merged = (gate(0) * _dot(ogla_ref[...], wgla_ref[...])
              + gate(1) * _dot(ogdn_ref[...], wgdn_ref[...])
              + gate(2) * _dot(orwkv_ref[...], wrwkv_ref[...]))
    mix = _dot(merged, wout_ref[...])
    o_ref[...] = _layer_norm(DN_ALPHA * x + mix, g_ref[...], b_ref[...])


def _merge(x, wgate, ogla, ogdn, orwkv, wgla, wgdn, wrwkv, wout, g, b, layer, tm=1024):
    t = x.shape[0]
    d = D_MODEL
    tok = lambda w: pl.BlockSpec((tm, w), lambda i: (i, 0))
    full = lambda shape: pl.BlockSpec(shape, lambda i: (0, 0))
    resident = lambda rows: pl.BlockSpec((None, rows, d), lambda i: (layer, 0, 0),
                                         pipeline_mode=pl.Buffered(1))
    return pl.pallas_call(
        _merge_kernel,
        grid=(t // tm,),
        in_specs=[tok(d),
                  pl.BlockSpec((d, N_BRANCH * d), lambda i: (0, 0), pipeline_mode=pl.Buffered(1)),
                  tok(GLA_DV), tok(GDN_WIDTH), tok(RWKV_WIDTH),
                  resident(GLA_DV), resident(GDN_WIDTH), resident(RWKV_WIDTH), resident(d),
                  full((1, d)), full((1, d))],
        out_specs=tok(d),
        out_shape=jax.ShapeDtypeStruct((t, d), F32),
        compiler_params=pltpu.CompilerParams(dimension_semantics=("parallel",),
                                             vmem_limit_bytes=VMEM_LIMIT),
        name="merge",
    )(x, wgate, ogla, ogdn, orwkv, wgla, wgdn, wrwkv, wout, g, b)


def _mlp_kernel(x_ref, wu_ref, wd_ref, g_ref, b_ref, o_ref, *, tf):
    x = x_ref[...]
    xb = x.astype(BF16)
    acc = None
    for j in range(D_FF // tf):
        hid = jnp.maximum(jnp.dot(xb, wu_ref[:, j * tf:(j + 1) * tf], preferred_element_type=F32), 0.0)
        part = jnp.dot((hid * hid).astype(BF16), wd_ref[j * tf:(j + 1) * tf, :], preferred_element_type=F32)
        acc = part if acc is None else acc + part
    o_ref[...] = _layer_norm(DN_ALPHA * x + acc, g_ref[...], b_ref[...])


def _mlp(x, wu, wd, g, b, layer, tm=1024, tf=1024):
    t = x.shape[0]
    d = D_MODEL
    resident = lambda shape: pl.BlockSpec((None,) + shape, lambda i: (layer, 0, 0),
                                          pipeline_mode=pl.Buffered(1))
    return pl.pallas_call(
        functools.partial(_mlp_kernel, tf=tf),
        grid=(t // tm,),
        in_specs=[pl.BlockSpec((tm, d), lambda i: (i, 0)),
                  resident((d, D_FF)), resident((D_FF, d)),
                  pl.BlockSpec((1, d), lambda i: (0, 0)),
                  pl.BlockSpec((1, d), lambda i: (0, 0))],
        out_specs=pl.BlockSpec((tm, d), lambda i: (i, 0)),
        out_shape=jax.ShapeDtypeStruct((t, d), F32),
        compiler_params=pltpu.CompilerParams(dimension_semantics=("parallel",),
                                             vmem_limit_bytes=VMEM_LIMIT),
        name="mlp",
    )(x, wu, wd, g, b)


def _pad_rows(w, height):
    return jnp.pad(w, ((0, height - w.shape[0]), (0, 0)))


def _relayout_plans():
    o = _IN_OFFS
    s = 3 * RWKV_WIDTH
    gla = (2 * GLA_DK + 2 * GLA_DV + LANE,
           [(0, o[0], 2 * GLA_DK + GLA_DV), (2 * GLA_DK + GLA_DV, o[4], GLA_DV),
            (2 * GLA_DK + 2 * GLA_DV, o[3], GLA_GATE_RANK)])
    gdn = (4 * GDN_WIDTH + LANE,
           [(0, o[5], 3 * GDN_WIDTH), (3 * GDN_WIDTH, o[8], GDN_WIDTH), (4 * GDN_WIDTH, o[6], 2 * GDN_HEADS)])
    rwkv = (s + 3 * LANE,
            [(0, o[9], s), (s, o[9] + s, RWKV_DECAY_RANK), (s + LANE, o[9] + s + RWKV_DECAY_RANK, RWKV_A_RANK),
             (s + 2 * LANE, o[9] + s + RWKV_DECAY_RANK + RWKV_A_RANK, RWKV_GATE_RANK)])
    gates = (N_BRANCH * D_MODEL, [(0, o[10], N_BRANCH * D_MODEL)])
    return gla, gdn, rwkv, gates


def _relayout_kernel(wt_ref, *out_refs):
    tk = wt_ref.shape[1]
    for out_ref, (width, pieces) in zip(out_refs, _relayout_plans()):
        covered = 0
        for dst, src, n in pieces:
            if dst > covered:
                out_ref[:, covered:dst] = jnp.zeros((tk, dst - covered), out_ref.dtype)
            for off in range(0, n, LANE):
                m = min(LANE, n - off)
                tile = wt_ref[src + off:src + off + LANE, :].T
                out_ref[:, dst + off:dst + off + m] = tile[:, 0:m].astype(out_ref.dtype)
            covered = dst + n
        if width > covered:
            out_ref[:, covered:width] = jnp.zeros((tk, width - covered), out_ref.dtype)


def _split_w_in(w_in_t, layer, tk=128):
    _, n_in, k = w_in_t.shape
    assert all(src % SUBLANE == 0 and (n % LANE == 0 or src + n // LANE * LANE + LANE <= n_in)
               for _, pieces in _relayout_plans() for _, src, n in pieces)
    widths = [width for width, _ in _relayout_plans()]
    return pl.pallas_call(
        _relayout_kernel,
        grid=(k // tk,),
        in_specs=[pl.BlockSpec((None, n_in, tk), lambda i: (layer, 0, i))],
        out_specs=[pl.BlockSpec((tk, width), lambda i: (i, 0)) for width in widths],
        out_shape=[jax.ShapeDtypeStruct((k, width), BF16) for width in widths],
        compiler_params=pltpu.CompilerParams(dimension_semantics=("parallel",),
                                             vmem_limit_bytes=VMEM_LIMIT),
        name="w_in_relayout",
    )(w_in_t)


def _pad_mu(mu):
    s = 3 * RWKV_WIDTH
    z = jnp.zeros((LANE - RWKV_DECAY_RANK,), F32)
    return jnp.concatenate([mu[:s], mu[s:s + RWKV_DECAY_RANK], z,
                            mu[s + RWKV_DECAY_RANK:s + RWKV_DECAY_RANK + RWKV_A_RANK], z,
                            mu[s + RWKV_DECAY_RANK + RWKV_A_RANK:]])[None, :]


def kernel(x, w_in, gla_gate_up, gla_gate_bias, gla_norm_g, gdn_conv, gdn_a_log, gdn_dt_bias, gdn_norm_g, rwkv_mu, rwkv_w0, rwkv_w2, rwkv_a0, rwkv_a2, rwkv_g2, rwkv_k_k, rwkv_k_a, rwkv_r_k, rwkv_ln_g, rwkv_ln_b, w_br_gla, w_br_gdn, w_br_rwkv, w_out, ln1_g, ln1_b, w_up, w_down, ln2_g, ln2_b):
    bsz, seq, d = x.shape
    xt = x.reshape(bsz * seq, d)
    as_rows = lambda o: o.reshape(bsz * seq, o.shape[-1])
    row = lambda v: v[None, :]
    w_up_b = w_up.astype(BF16)
    w_down_b = w_down.astype(BF16)
    w_br_b = [w.astype(BF16) for w in (w_br_gla, w_br_gdn, w_br_rwkv)]
    w_out_b = w_out.astype(BF16)
    w_in_t = jnp.swapaxes(w_in, 1, 2)
    for l in range(DEPTH):
        w_gla, w_gdn, w_rwkv, w_gates = _split_w_in(w_in_t, l)
        xs = xt.reshape(bsz, seq, d)
        o_gla = _gla(xs, w_gla, _pad_rows(gla_gate_up[l], LANE).astype(BF16), row(gla_gate_bias[l]),
                     row(gla_norm_g[l]))
        o_gdn = _gdn(xs, w_gdn, gdn_conv[l], gdn_a_log[l], gdn_dt_bias[l], row(gdn_norm_g[l]))
        o_rwkv = _rwkv(
            xs, w_rwkv, _pad_mu(rwkv_mu[l]), row(rwkv_w0[l]), _pad_rows(rwkv_w2[l], LANE).astype(BF16),
            row(rwkv_a0[l]), _pad_rows(rwkv_a2[l], LANE).astype(BF16), rwkv_g2[l].astype(BF16),
            row(rwkv_k_k[l]), row(rwkv_k_a[l]), row(rwkv_r_k[l]), row(rwkv_ln_g[l]), row(rwkv_ln_b[l]))

        xt = _merge(xt, w_gates, as_rows(o_gla), as_rows(o_gdn), as_rows(o_rwkv), w_br_b[0], w_br_b[1], w_br_b[2],
                    w_out_b, row(ln1_g[l]), row(ln1_b[l]), l)
        xt = _mlp(xt, w_up_b, w_down_b, row(ln2_g[l]), row(ln2_b[l]), l)
    return xt.reshape(bsz, seq, d)
```
